```python
import math
import jax, jax.numpy as jnp
from jax import lax
import numpy as np

D_MODEL = 1024
BATCH = 8
SEQ = 8192
DEPTH = 4

GRID_W = 64
CTX_LEN = 256
N_EVEN = (DEPTH + 1) // 2
N_ODD = DEPTH // 2
EPS = 1e-6
ROPE_BASE = 10000.0

LRU_WIDTH = D_MODEL // 2
LRU_BLOCKS = 8
LRU_BLOCK = LRU_WIDTH // LRU_BLOCKS
CONV_W = 4
CONV_LEFT = CONV_W // 2
LRU_C = 8.0
RET_HEADS = 4
RET_DK = 128
RET_DV = 128
RET_CHUNK = 128
EVEN_SPLITS = (LRU_WIDTH, LRU_WIDTH, RET_HEADS * RET_DK, RET_HEADS * RET_DK,
               RET_HEADS * RET_DV, RET_HEADS * RET_DV)
EVEN_IN = sum(EVEN_SPLITS)
EVEN_OUT = LRU_WIDTH + RET_HEADS * RET_DV
S5_WIDTH = D_MODEL // 4
S5_GROUP = 16
S5_GROUPS = S5_WIDTH // S5_GROUP
S5_STATE = 64
DT_MIN = 0.001
DT_MAX = 0.1
DIFF_HEADS = 6
DIFF_DH = 64
DIFF_DV = 2 * DIFF_DH
Q_BLOCK = 128
ODD_SPLITS = (S5_WIDTH, DIFF_HEADS * 2 * DIFF_DH, DIFF_HEADS * 2 * DIFF_DH, DIFF_HEADS * DIFF_DV)
ODD_IN = sum(ODD_SPLITS)
ODD_OUT = S5_WIDTH + DIFF_HEADS * DIFF_DV
D_FF = 3584
N_EXPERTS = 8
TOP_K = 2

kernel_name = "hybrid_lru_retention_s5_diffattn_moe_dit"

F32 = jnp.float32


def rmsnorm(x, g):
    xf = x.astype(F32)
    y = xf * lax.rsqrt(jnp.mean(xf * xf, axis=-1, keepdims=True) + EPS)
    return (y * g.astype(F32)).astype(x.dtype)


def modulate(h, shift, scale):
    return h * (1.0 + scale) + shift


def head_norm(y, g, center):
    if center:
        y = y - jnp.mean(y, axis=-1, keepdims=True)
    y = y * lax.rsqrt(jnp.mean(y * y, axis=-1, keepdims=True) + EPS)
    return y * g.astype(F32).reshape(y.shape[-2], y.shape[-1])


def split_cols(z, sizes):
    out, start = [], 0
    for s in sizes:
        out.append(z[..., start:start + s])
        start += s
    return out


def axial_rope(n_rows, dim):
    t = jnp.arange(n_rows * GRID_W)
    rows = (t // GRID_W).astype(F32)
    cols = (t % GRID_W).astype(F32)
    quarter = dim // 4
    inv = ROPE_BASE ** (-jnp.arange(quarter, dtype=F32) / quarter)
    ang = jnp.concatenate([rows[:, None] * inv, cols[:, None] * inv], axis=-1)
    return jnp.cos(ang), jnp.sin(ang)


def apply_rope(x, cos, sin):
    half = x.shape[-1] // 2
    shp = (cos.shape[0],) + (1,) * (x.ndim - 3) + (half,)
    c, s = cos.reshape(shp).astype(x.dtype), sin.reshape(shp).astype(x.dtype)
    x1, x2 = x[..., :half], x[..., half:]
    return jnp.concatenate([x1 * c - x2 * s, x1 * s + x2 * c], axis=-1)


def dwconv(u, w, b):
    T = u.shape[1]
    up = jnp.pad(u, ((0, 0), (CONV_LEFT, CONV_W - 1 - CONV_LEFT), (0, 0)))
    out = b
    for k in range(CONV_W):
        out = out + up[:, k:k + T] * w[k]
    return out


def linear_scan(a, b, h0):
    b = b.at[:, 0].add(a[:, 0] * h0)

    def combine(l, r):
        return (r[0] * l[0], r[0] * l[1] + r[1])

    _, h = lax.associative_scan(combine, (a, b), axis=1)
    return h


def flip_t(t):
    return None if t is None else jnp.flip(t, axis=1)


def run_bidirectional(dir_fn, c_in, x_in, p_fwd, p_bwd, ctx_out):
    yc_f, yx_f = dir_fn(c_in, x_in, p_fwd, ctx_out)
    yc_b, yx_b = dir_fn(tuple(flip_t(t) for t in c_in), tuple(flip_t(t) for t in x_in), p_bwd, ctx_out)
    yx = yx_f + flip_t(yx_b)
    yc = (yc_f + flip_t(yc_b)) if ctx_out else None
    return yc, yx


def rglru_coeffs(u, gate_w, gate_b, lam):
    ub = u.reshape(u.shape[0], u.shape[1], LRU_BLOCKS, LRU_BLOCK)
    g = jnp.einsum('btnk,gnkj->gbtnj', ub, gate_w.astype(F32)).reshape((2,) + u.shape)
    g = g + gate_b.astype(F32)[:, None, None, :]
    r = jax.nn.sigmoid(g[0])
    i = jax.nn.sigmoid(g[1])
    log_a = -LRU_C * r * jax.nn.softplus(-lam.astype(F32))
    a = jnp.exp(log_a)
    b = jnp.sqrt(-jnp.expm1(2.0 * log_a)) * (i * u)
    return a, b


def rglru_direction(c_in, x_in, p, ctx_out):
    gate_w, gate_b, lam = p
    (uc,), (ux,) = c_in, x_in
    a_c, b_c = rglru_coeffs(uc, gate_w, gate_b, lam)
    h_c = linear_scan(a_c, b_c, jnp.zeros_like(b_c[:, 0]))
    a_x, b_x = rglru_coeffs(ux, gate_w, gate_b, lam)
    h_x = linear_scan(a_x, b_x, h_c[:, -1])
    return (h_c if ctx_out else None), h_x


def retention_chunks(q, k, v, log_g, s0, with_output):
    Bn, T, H, dk = q.shape
    dv = v.shape[-1]
    C = RET_CHUNK
    N = T // C
    q = q.reshape(Bn, N, C, H, dk)
    k = k.reshape(Bn, N, C, H, dk)
    v = v.reshape(Bn, N, C, H, dv)
    j = jnp.arange(C, dtype=F32)
    k_dec = k * jnp.exp((C - 1 - j)[:, None] * log_g[None, :])[None, None, :, :, None]
    kv = jnp.einsum('bnjhd,bnjhe->nbhde', k_dec, v)
    g_C = jnp.exp(C * log_g)[None, :, None, None]

    def step(s, kv_n):
        return g_C * s + kv_n, s

    s_final, s_start = lax.scan(step, s0, kv)
    if not with_output:
        return None, s_final
    diff = j[:, None] - j[None, :]
    inner_decay = jnp.where(diff >= 0, jnp.exp(jnp.maximum(diff, 0.0)[None] * log_g[:, None, None]), 0.0)
    scores = jnp.einsum('bnihd,bnjhd->bnhij', q, k) * inner_decay[None, None]
    inner = jnp.einsum('bnhij,bnjhe->bnihe', scores, v)
    q_dec = q * jnp.exp((j + 1.0)[:, None] * log_g[None, :])[None, None, :, :, None]
    cross = jnp.einsum('bnihd,nbhde->bnihe', q_dec, s_start)
    return (inner + cross).reshape(Bn, T, H, dv), s_final


def retention_direction(c_in, x_in, theta, ctx_out):
    log_g = jax.nn.log_sigmoid(theta.astype(F32))
    qc = c_in[0]
    s0 = jnp.zeros((qc.shape[0], RET_HEADS, RET_DK, RET_DV), F32)
    y_c, s_c = retention_chunks(*c_in, log_g, s0, ctx_out)
    y_x, _ = retention_chunks(*x_in, log_g, s_c, True)
    return y_c, y_x


def s5_direction(c_in, x_in, p, ctx_out):
    lam_re, lam_im, log_dt, b_re, b_im, c_re, c_im = p
    lam = lax.complex(lam_re.astype(F32), lam_im.astype(F32))
    dt = jnp.exp(log_dt.astype(F32))[:, None]
    a_bar = jnp.exp(lam * dt)
    b_bar = ((a_bar - 1.0) / lam)[..., None] * lax.complex(b_re.astype(F32), b_im.astype(F32))
    c_mat = lax.complex(c_re.astype(F32), c_im.astype(F32))

    def states(u, h0):
        bu = jnp.einsum('gpi,btgi->btgp', b_bar, u.astype(jnp.complex64))
        a = jnp.broadcast_to(a_bar, (1, u.shape[1]) + a_bar.shape)
        return linear_scan(a, bu, h0)

    def readout(h):
        return jnp.einsum('gop,btgp->btgo', c_mat, h).real

    (uc,), (ux,) = c_in, x_in
    h_c = states(uc, jnp.zeros((uc.shape[0], S5_GROUPS, S5_STATE), jnp.complex64))
    h_x = states(ux, h_c[:, -1])
    return (readout(h_c) if ctx_out else None), readout(h_x)


def diff_attn(qb, k_all, v_all, lam):
    s = jnp.einsum('bqhmd,bkhmd->bhmqk', qb, k_all) * (DIFF_DH ** -0.5)
    p = jax.nn.softmax(s, axis=-1)
    w = p[:, :, 0] - lam * p[:, :, 1]
    return jnp.einsum('bhqk,bkhe->bqhe', w, v_all)


def swiglu(h, w_gate, w_up, w_down):
    return (jax.nn.silu(h @ w_gate) * (h @ w_up)) @ w_down


def moe_ffn(h, router_w, router_b, w_gate, w_up, w_down):
    logits = (h @ router_w).astype(F32) + router_b.astype(F32)
    top_v, top_i = lax.top_k(logits, TOP_K)
    top_p = jax.nn.softmax(top_v, axis=-1)
    gates = jnp.sum(jax.nn.one_hot(top_i, N_EXPERTS, dtype=F32) * top_p[..., None], axis=-2)
    out = jnp.zeros(h.shape, F32)
    for e in range(N_EXPERTS):
        out = out + gates[..., e:e + 1] * swiglu(h, w_gate[e], w_up[e], w_down[e]).astype(F32)
    return out.astype(h.dtype)


def even_mixer(hc, hx, cos, sin, w_in, conv_w, conv_b, gate_w, gate_b, lam, theta, gn, w_out, ctx_out):
    uc, gc, qc, kc, vc, oc = split_cols(hc @ w_in, EVEN_SPLITS)
    ux, gx, qx, kx, vx, ox = split_cols(hx @ w_in, EVEN_SPLITS)
    uc = dwconv(uc, conv_w, conv_b).astype(F32)
    ux = dwconv(ux, conv_w, conv_b).astype(F32)
    lru_c, lru_x = run_bidirectional(rglru_direction, (uc,), (ux,),
                                     (gate_w[0], gate_b[0], lam[0]), (gate_w[1], gate_b[1], lam[1]), ctx_out)

    def heads(t, d):
        return t.reshape(t.shape[0], t.shape[1], RET_HEADS, d).astype(F32)

    ks = RET_DK ** -0.5
    q_x = apply_rope(heads(qx, RET_DK), cos, sin)
    k_x = apply_rope(heads(kx, RET_DK), cos, sin) * ks
    ret_c, ret_x = run_bidirectional(retention_direction,
                                     (heads(qc, RET_DK), heads(kc, RET_DK) * ks, heads(vc, RET_DV)),
                                     (q_x, k_x, heads(vx, RET_DV)), theta[0], theta[1], ctx_out)

    def merge(h_lru, g_lru, ret, g_ret):
        a = h_lru * jax.nn.gelu(g_lru.astype(F32))
        r = head_norm(ret, gn, True)
        r = r.reshape(r.shape[0], r.shape[1], -1) * jax.nn.silu(g_ret.astype(F32))
        return jnp.concatenate([a, r], axis=-1).astype(hx.dtype) @ w_out

    mx = merge(lru_x, gx, ret_x, ox)
    mc = merge(lru_c, gc, ret_c, oc) if ctx_out else None
    return mc, mx


def odd_mixer(hc, hx, cos, sin, w_in, lam_re, lam_im, log_dt, b_re, b_im, c_re, c_im,
              d_skip, glu_w, glu_b, lam_p, gn, w_out, lam_init, ctx_out):
    uc, qc, kc, vc = split_cols(hc @ w_in, ODD_SPLITS)
    ux, qx, kx, vx = split_cols(hx @ w_in, ODD_SPLITS)

    def groups(u):
        return u.reshape(u.shape[0], u.shape[1], S5_GROUPS, S5_GROUP).astype(F32)

    def s5p(d):
        return (lam_re[d], lam_im[d], log_dt[d], b_re[d], b_im[d], c_re[d], c_im[d])

    s5_c, s5_x = run_bidirectional(s5_direction, (groups(uc),), (groups(ux),), s5p(0), s5p(1), ctx_out)

    def s5_out(y, u):
        y = y.reshape(u.shape) + d_skip.astype(F32) * u.astype(F32)
        z = jax.nn.gelu(y)
        return z * jax.nn.sigmoid(z @ glu_w.astype(F32) + glu_b.astype(F32))

    lp = lam_p.astype(F32)
    lam = jnp.exp(jnp.sum(lp[0] * lp[1])) - jnp.exp(jnp.sum(lp[2] * lp[3])) + lam_init

    def qk(t):
        return t.reshape(t.shape[0], t.shape[1], DIFF_HEADS, 2, DIFF_DH).astype(F32)

    def vh(t):
        return t.reshape(t.shape[0], t.shape[1], DIFF_HEADS, DIFF_DV).astype(F32)

    q_c, k_c, v_c = qk(qc), qk(kc), vh(vc)
    q_x = apply_rope(qk(qx), cos, sin)
    k_x = apply_rope(qk(kx), cos, sin)
    k_all = jnp.concatenate([k_c, k_x], axis=1)
    v_all = jnp.concatenate([v_c, vh(vx)], axis=1)
    Bn, T = q_x.shape[:2]
    qb = jnp.swapaxes(q_x.reshape(Bn, T // Q_BLOCK, Q_BLOCK, DIFF_HEADS, 2, DIFF_DH), 0, 1)
    att_x = lax.map(lambda blk: diff_attn(blk, k_all, v_all, lam), qb)
    att_x = jnp.swapaxes(att_x, 0, 1).reshape(Bn, T, DIFF_HEADS, DIFF_DV)

    def merge(y_s5, u, att):
        a = head_norm(att, gn, False) * (1.0 - lam_init)
        a = a.reshape(a.shape[0], a.shape[1], -1)
        return jnp.concatenate([s5_out(y_s5, u), a], axis=-1).astype(hx.dtype) @ w_out

    mx = merge(s5_x, ux, att_x)
    mc = merge(s5_c, uc, diff_attn(q_c, k_c, v_c, lam)) if ctx_out else None
    return mc, mx


def setup_inputs(seed: int = 0) -> dict:
    key = jax.random.key(seed)
    ks = iter(jax.random.split(key, 64))
    D = D_MODEL

    def nrm(shape, scale):
        return scale * jax.random.normal(next(ks), shape, F32)

    x = nrm((BATCH, SEQ, D), 1.0)
    c = nrm((BATCH, D), 1.0)
    ctx = nrm((BATCH, CTX_LEN, D), 1.0)
    c_ctx = nrm((D,), 1.0)
    mod_w = nrm((DEPTH, D, 6 * D), 0.5 * D ** -0.5)
    mod_b = nrm((DEPTH, 6 * D), 0.02)
    norm_g = 1.0 + nrm((DEPTH, 4, D), 0.02)
    ev_w_in = nrm((N_EVEN, D, EVEN_IN), D ** -0.5)
    ev_conv_w = nrm((N_EVEN, CONV_W, LRU_WIDTH), CONV_W ** -0.5)
    ev_conv_b = nrm((N_EVEN, LRU_WIDTH), 0.02)
    lru_gate_w = nrm((N_EVEN, 2, 2, LRU_BLOCKS, LRU_BLOCK, LRU_BLOCK), LRU_BLOCK ** -0.5)
    lru_gate_b = nrm((N_EVEN, 2, 2, LRU_WIDTH), 0.02)
    a_c = jax.random.uniform(next(ks), (N_EVEN, 2, LRU_WIDTH), F32, 0.9, 0.999)
    s = a_c ** (1.0 / LRU_C)
    lru_lam = jnp.log(s) - jnp.log1p(-s)
    hh = jnp.arange(RET_HEADS, dtype=F32)
    ret_theta = jnp.log(2.0 ** (5.0 + hh) - 1.0) + nrm((N_EVEN, 2, RET_HEADS), 0.01)
    ret_gn = 1.0 + nrm((N_EVEN, RET_HEADS * RET_DV), 0.02)
    ev_w_out = nrm((N_EVEN, EVEN_OUT, D), EVEN_OUT ** -0.5)
    ffd_w_gate = nrm((N_EVEN, D, D_FF), D ** -0.5)
    ffd_w_up = nrm((N_EVEN, D, D_FF), D ** -0.5)
    ffd_w_down = nrm((N_EVEN, D_FF, D), D_FF ** -0.5)
    od_w_in = nrm((N_ODD, D, ODD_IN), D ** -0.5)
    nn_ = jnp.arange(S5_STATE, dtype=F32)
    s5_lam_re = -0.5 + nrm((N_ODD, 2, S5_GROUPS, S5_STATE), 0.01)
    s5_lam_im = math.pi * nn_ + nrm((N_ODD, 2, S5_GROUPS, S5_STATE), 0.01)
    s5_log_dt = jax.random.uniform(next(ks), (N_ODD, 2, S5_GROUPS), F32, math.log(DT_MIN), math.log(DT_MAX))
    s5_b_re = nrm((N_ODD, 2, S5_GROUPS, S5_STATE, S5_GROUP), (2 * S5_GROUP) ** -0.5)
    s5_b_im = nrm((N_ODD, 2, S5_GROUPS, S5_STATE, S5_GROUP), (2 * S5_GROUP) ** -0.5)
    s5_c_re = nrm((N_ODD, 2, S5_GROUPS, S5_GROUP, S5_STATE), (2 * S5_STATE) ** -0.5)
    s5_c_im = nrm((N_ODD, 2, S5_GROUPS, S5_GROUP, S5_STATE), (2 * S5_STATE) ** -0.5)
    s5_d = nrm((N_ODD, S5_WIDTH), 1.0)
    s5_glu_w = nrm((N_ODD, S5_WIDTH, S5_WIDTH), S5_WIDTH ** -0.5)
    s5_glu_b = nrm((N_ODD, S5_WIDTH), 0.02)
    diff_lam = nrm((N_ODD, 4, DIFF_DH), 0.1)
    diff_gn = 1.0 + nrm((N_ODD, DIFF_HEADS * DIFF_DV), 0.02)
    od_w_out = nrm((N_ODD, ODD_OUT, D), ODD_OUT ** -0.5)
    moe_router_w = nrm((N_ODD, D, N_EXPERTS), D ** -0.5)
    moe_router_b = nrm((N_ODD, N_EXPERTS), 0.01)
    moe_w_gate = nrm((N_ODD, N_EXPERTS, D, D_FF), D ** -0.5)
    moe_w_up = nrm((N_ODD, N_EXPERTS, D, D_FF), D ** -0.5)
    moe_w_down = nrm((N_ODD, N_EXPERTS, D_FF, D), D_FF ** -0.5)
    return {"x": x, "c": c, "ctx": ctx, "c_ctx": c_ctx, "mod_w": mod_w, "mod_b": mod_b, "norm_g": norm_g,
            "ev_w_in": ev_w_in, "ev_conv_w": ev_conv_w, "ev_conv_b": ev_conv_b,
            "lru_gate_w": lru_gate_w, "lru_gate_b": lru_gate_b, "lru_lam": lru_lam,
            "ret_theta": ret_theta, "ret_gn": ret_gn, "ev_w_out": ev_w_out,
            "ffd_w_gate": ffd_w_gate, "ffd_w_up": ffd_w_up, "ffd_w_down": ffd_w_down,
            "od_w_in": od_w_in, "s5_lam_re": s5_lam_re, "s5_lam_im": s5_lam_im, "s5_log_dt": s5_log_dt,
            "s5_b_re": s5_b_re, "s5_b_im": s5_b_im, "s5_c_re": s5_c_re, "s5_c_im": s5_c_im,
            "s5_d": s5_d, "s5_glu_w": s5_glu_w, "s5_glu_b": s5_glu_b, "diff_lam": diff_lam,
            "diff_gn": diff_gn, "od_w_out": od_w_out, "moe_router_w": moe_router_w,
            "moe_router_b": moe_router_b, "moe_w_gate": moe_w_gate, "moe_w_up": moe_w_up,
            "moe_w_down": moe_w_down}


def reference(x, c, ctx, c_ctx, mod_w, mod_b, norm_g,
              ev_w_in, ev_conv_w, ev_conv_b, lru_gate_w, lru_gate_b, lru_lam, ret_theta, ret_gn, ev_w_out,
              ffd_w_gate, ffd_w_up, ffd_w_down,
              od_w_in, s5_lam_re, s5_lam_im, s5_log_dt, s5_b_re, s5_b_im, s5_c_re, s5_c_im,
              s5_d, s5_glu_w, s5_glu_b, diff_lam, diff_gn, od_w_out,
              moe_router_w, moe_router_b, moe_w_gate, moe_w_up, moe_w_down):
    ROWS = x.shape[1] // GRID_W
    cos_r, sin_r = axial_rope(ROWS, RET_DK)
    cos_d, sin_d = axial_rope(ROWS, DIFF_DH)

    def channel_mixer(l, i, h):
        if l % 2 == 0:
            return swiglu(h, ffd_w_gate[i], ffd_w_up[i], ffd_w_down[i])
        return moe_ffn(h, moe_router_w[i], moe_router_b[i], moe_w_gate[i], moe_w_up[i], moe_w_down[i])

    xx, xc = x, ctx
    for l in range(DEPTH):
        i = l // 2
        ctx_out = l < DEPTH - 1
        mod_x = jnp.split((jax.nn.silu(c) @ mod_w[l] + mod_b[l])[:, None, :], 6, axis=-1)
        mod_c = jnp.split((jax.nn.silu(c_ctx) @ mod_w[l] + mod_b[l])[None, None, :], 6, axis=-1)
        hx = modulate(rmsnorm(xx, norm_g[l, 0]), mod_x[0], mod_x[1])
        hc = modulate(rmsnorm(xc, norm_g[l, 0]), mod_c[0], mod_c[1])
        if l % 2 == 0:
            yc, yx = even_mixer(hc, hx, cos_r, sin_r, ev_w_in[i], ev_conv_w[i], ev_conv_b[i],
                                lru_gate_w[i], lru_gate_b[i], lru_lam[i], ret_theta[i], ret_gn[i],
                                ev_w_out[i], ctx_out)
        else:
            lam_init = 0.8 - 0.6 * math.exp(-0.3 * l)
            yc, yx = odd_mixer(hc, hx, cos_d, sin_d, od_w_in[i], s5_lam_re[i], s5_lam_im[i], s5_log_dt[i],
                               s5_b_re[i], s5_b_im[i], s5_c_re[i], s5_c_im[i], s5_d[i], s5_glu_w[i],
                               s5_glu_b[i], diff_lam[i], diff_gn[i], od_w_out[i], lam_init, ctx_out)
        xx = xx + mod_x[2] * rmsnorm(yx, norm_g[l, 1])
        hx = modulate(rmsnorm(xx, norm_g[l, 2]), mod_x[3], mod_x[4])
        xx = xx + mod_x[5] * rmsnorm(channel_mixer(l, i, hx), norm_g[l, 3])
        if ctx_out:
            xc = xc + mod_c[2] * rmsnorm(yc, norm_g[l, 1])
            hc = modulate(rmsnorm(xc, norm_g[l, 2]), mod_c[3], mod_c[4])
            xc = xc + mod_c[5] * rmsnorm(channel_mixer(l, i, hc), norm_g[l, 3])
    return xx
```

```python
import functools
import math

import jax
import jax.numpy as jnp
from jax import lax
from jax.experimental import pallas as pl
from jax.experimental.pallas import tpu as pltpu

F32 = jnp.float32
BF16 = jnp.bfloat16

EPS = 1e-6
ROPE_BASE = 10000.0
GRID_W = 64
LRU_C = 8.0
LRU_BLOCKS = 8
CONV_W = 4
CONV_LEFT = 2
RET_HEADS = 4
RET_D = 128
S5_GROUP = 16
S5_STATE = 64
DIFF_HEADS = 6
DIFF_DH = 64
N_EXPERTS = 8
LANES = 128
SUBLANES = 8

TB = 256
LRU_TT = 64
S5_TT = 32
ATTN_TQ = 512
ATTN_TK = 768
FFN_TM = 1024
FFN_TF = 512
VMEM_LIMIT = 56 * 1024 * 1024
NEG = -1e30


def _cp(*sem):
    return pltpu.CompilerParams(dimension_semantics=sem, vmem_limit_bytes=VMEM_LIMIT)


def _dot(a, b):
    return jnp.dot(a, b, preferred_element_type=F32)


def _split(x):
    hi = x.astype(BF16)
    lo = (x - hi.astype(F32)).astype(BF16)
    return hi, lo


def _dot3(a, w):
    ah, al = _split(a)
    wh, wl = _split(w)
    return _dot(ah, wh) + (_dot(ah, wl) + _dot(al, wh))


def _rms(x, g):
    return x * lax.rsqrt(jnp.mean(x * x, axis=-1, keepdims=True) + EPS) * g


def _sigmoid(x):
    return 1.0 / (1.0 + jnp.exp(-x))


def _silu(x):
    return x * _sigmoid(x)


def _gelu(x):
    return 0.5 * x * (1.0 + jnp.tanh(math.sqrt(2.0 / math.pi) * (x + 0.044715 * (x * x * x))))


def _softplus(x):
    return jnp.maximum(x, 0.0) + jnp.log(1.0 + jnp.exp(-jnp.abs(x)))


def _modtab_kernel(c_ref, w_ref, b_ref, o_ref):
    c = c_ref[...]
    o_ref[0] = _dot3(_silu(c), w_ref[0]) + b_ref[0]


def _modtab(cc, mod_w, mod_b):
    depth, d, n = mod_w.shape
    tn = 1536
    return pl.pallas_call(
        _modtab_kernel,
        grid=(depth, n // tn),
        in_specs=[pl.BlockSpec((cc.shape[0], d), lambda l, j: (0, 0)),
                  pl.BlockSpec((1, d, tn), lambda l, j: (l, 0, j)),
                  pl.BlockSpec((1, 1, tn), lambda l, j: (l, 0, j))],
        out_specs=pl.BlockSpec((1, cc.shape[0], tn), lambda l, j: (l, 0, j)),
        out_shape=jax.ShapeDtypeStruct((depth, cc.shape[0], n), F32),
        compiler_params=_cp("arbitrary", "arbitrary"),
        name="modtab",
    )(cc, mod_w, mod_b.reshape(depth, 1, n))


def _norm_mod_in(s_ref, mod_ref, g_ref):
    mod = mod_ref[0, 0]
    h = _rms(s_ref[0], g_ref[...]) * (1.0 + mod[1:2]) + mod[0:1]
    return h.astype(BF16)


def _inproj_even_kernel(s_ref, mod_ref, g_ref, w_ref, cos_ref, sin_ref, u_ref, zb_ref, *, nlat, kscale):
    j = pl.program_id(1)
    hb = _norm_mod_in(s_ref, mod_ref, g_ref)
    lw = RET_HEADS * RET_D

    def proj(i):
        return _dot(hb, w_ref[:, i * lw:(i + 1) * lw])

    u_ref[...] = proj(0)
    zb_ref[0, :, 0:lw] = proj(1).astype(BF16)
    lat = j < nlat
    cos = jnp.where(lat, cos_ref[...], 1.0)
    sin = jnp.where(lat, sin_ref[...], 0.0)
    for sec, scale in ((0, 1.0), (1, kscale)):
        t = proj(2 + sec)
        for hd in range(RET_HEADS):
            th = t[:, hd * RET_D:(hd + 1) * RET_D]
            th = (th * cos + pltpu.roll(th, RET_D // 2, 1) * sin) * scale
            zb_ref[0, :, (1 + sec) * lw + hd * RET_D:(1 + sec) * lw + (hd + 1) * RET_D] = th.astype(BF16)
    zb_ref[0, :, 3 * lw:4 * lw] = proj(4).astype(BF16)
    zb_ref[0, :, 4 * lw:5 * lw] = proj(5).astype(BF16)


def _inproj_odd_kernel(s_ref, mod_ref, g_ref, w_ref, cos_ref, sin_ref, u_ref, zb_ref, *, nlat, qscale, s5w):
    j = pl.program_id(1)
    hb = _norm_mod_in(s_ref, mod_ref, g_ref)
    aw = DIFF_HEADS * 2 * DIFF_DH
    u_ref[...] = _dot(hb, w_ref[:, 0:s5w])
    lat = j < nlat
    cos = jnp.where(lat, cos_ref[...], 1.0)
    sin = jnp.where(lat, sin_ref[...], 0.0)
    lane = lax.broadcasted_iota(jnp.int32, (TB, LANES), 1)
    first_half = (lane % DIFF_DH) < (DIFF_DH // 2)
    for sec, scale in ((0, qscale), (1, 1.0)):
        t = _dot(hb, w_ref[:, s5w + sec * aw:s5w + (sec + 1) * aw])
        for hd in range(DIFF_HEADS):
            th = t[:, hd * LANES:(hd + 1) * LANES]
            partner = jnp.where(first_half, pltpu.roll(th, LANES - DIFF_DH // 2, 1), pltpu.roll(th, DIFF_DH // 2, 1))
            th = (th * cos + partner * sin) * scale
            zb_ref[0, :, sec * aw + hd * LANES:sec * aw + (hd + 1) * LANES] = th.astype(BF16)
    zb_ref[0, :, 2 * aw:3 * aw] = _dot(hb, w_ref[:, s5w + 2 * aw:s5w + 3 * aw]).astype(BF16)


def _inproj(body, s, modtab, g, w, cos, sin, uw, zw, nlat, name):
    b, l, d = s.shape
    nj = l // TB
    return pl.pallas_call(
        body,
        grid=(b, nj),
        in_specs=[pl.BlockSpec((1, TB, d), lambda i, j: (i, j, 0)),
                  pl.BlockSpec((1, 1, 6, d), lambda i, j: (i, jnp.where(j < nlat, 1, 0), 0, 0)),
                  pl.BlockSpec((1, d), lambda i, j: (0, 0)),
                  pl.BlockSpec(w.shape, lambda i, j: (0, 0)),
                  pl.BlockSpec((TB, LANES), lambda i, j: (jnp.minimum(j, nlat - 1), 0)),
                  pl.BlockSpec((TB, LANES), lambda i, j: (jnp.minimum(j, nlat - 1), 0))],
        out_specs=[pl.BlockSpec((TB, uw), lambda i, j: (j, i)),
                   pl.BlockSpec((1, TB, zw), lambda i, j: (i, j, 0))],
        out_shape=[jax.ShapeDtypeStruct((l, b * uw), F32),
                   jax.ShapeDtypeStruct((b, l, zw), BF16)],
        compiler_params=_cp("arbitrary", "arbitrary"),
        name=name,
    )(s, modtab, g, w, cos, sin)


def _fwd_chunk(s, nlat_c, nctx_c):
    return jnp.where(s < nctx_c, nlat_c + s, s - nctx_c)


def _bwd_chunk(s, nlat_c, nctx_c):
    return nlat_c + nctx_c - 1 - s


def _lru_kernel(uf_ref, ufl_ref, ufr_ref, ub_ref, ubl_ref, ubr_ref, cw_ref, cb_ref, gw_ref, gb_ref, lam_ref,
                of_ref, ob_ref, pad_scr, a_scr, b_scr, h_scr, *, tt, nlat_c, nctx_c, nb):
    s = pl.program_id(0)
    nc = nlat_c + nctx_c

    @pl.when(s == 0)
    def _():
        h_scr[...] = jnp.zeros_like(h_scr)

    rows = tt * nb
    views = ((uf_ref, ufl_ref, ufr_ref, _fwd_chunk(s, nlat_c, nctx_c)),
             (ub_ref, ubl_ref, ubr_ref, _bwd_chunk(s, nlat_c, nctx_c)))
    for d, (u_ref, l_ref, r_ref, c) in enumerate(views):
        first = (c == 0) | (c == nlat_c)
        last = (c == nlat_c - 1) | (c == nc - 1)
        pad_scr[0:CONV_LEFT * nb] = jnp.where(first, 0.0, l_ref[...])
        pad_scr[CONV_LEFT * nb:CONV_LEFT * nb + rows] = u_ref[...]
        pad_scr[CONV_LEFT * nb + rows:(CONV_W - 1) * nb + rows] = jnp.where(last, 0.0, r_ref[...])
        uc = cb_ref[...] + pad_scr[0:rows] * cw_ref[0]
        for k in range(1, CONV_W):
            uc = uc + pad_scr[k * nb:k * nb + rows] * cw_ref[k]
        ucb = uc.astype(BF16)
        r = _sigmoid(_dot(ucb, gw_ref[d, 0]) + gb_ref[d, 0])
        i = _sigmoid(_dot(ucb, gw_ref[d, 1]) + gb_ref[d, 1])
        a = jnp.exp((-LRU_C * _softplus(-lam_ref[d])) * r)
        a_scr[d] = a
        b_scr[d] = jnp.sqrt(1.0 - a * a) * (i * uc)

    def body(t, carry):
        hf, hb = carry
        rf = pl.multiple_of(t * nb, nb)
        rb = pl.multiple_of((tt - 1 - t) * nb, nb)
        hf = a_scr[0, pl.ds(rf, nb), :] * hf + b_scr[0, pl.ds(rf, nb), :]
        of_ref[pl.ds(rf, nb), :] = hf
        hb = a_scr[1, pl.ds(rb, nb), :] * hb + b_scr[1, pl.ds(rb, nb), :]
        ob_ref[pl.ds(rb, nb), :] = hb
        return hf, hb

    hf, hb = lax.fori_loop(0, tt, body, (h_scr[0], h_scr[1]), unroll=8)
    h_scr[0] = hf
    h_scr[1] = hb


def _lru(u_tm, conv_w, conv_b, gate_w, gate_b, lam, nb, nlat, nctx):
    rows_total, c = u_tm.shape
    tt = LRU_TT
    nlat_c, nctx_c = nlat // tt, nctx // tt
    nc = nlat_c + nctx_c
    rows = tt * nb
    fwd = functools.partial(_fwd_chunk, nlat_c=nlat_c, nctx_c=nctx_c)
    bwd = functools.partial(_bwd_chunk, nlat_c=nlat_c, nctx_c=nctx_c)
    lrows, rrows = CONV_LEFT * nb, (CONV_W - 1 - CONV_LEFT) * nb
    nl, nr = rows // lrows, rows // rrows

    def cur(f):
        return pl.BlockSpec((rows, c), lambda s: (f(s), 0))

    def left(f):
        return pl.BlockSpec((lrows, c), lambda s: (jnp.maximum(f(s) * nl - 1, 0), 0))

    def right(f):
        return pl.BlockSpec((rrows, c), lambda s: (jnp.minimum((f(s) + 1) * nr, nc * nr - 1), 0))

    def whole(a):
        return pl.BlockSpec(a.shape, lambda s: (0,) * a.ndim)

    return pl.pallas_call(
        functools.partial(_lru_kernel, tt=tt, nlat_c=nlat_c, nctx_c=nctx_c, nb=nb),
        grid=(nc,),
        in_specs=[cur(fwd), left(fwd), right(fwd), cur(bwd), left(bwd), right(bwd),
                  whole(conv_w), whole(conv_b), whole(gate_w), whole(gate_b), whole(lam)],
        out_specs=[cur(fwd), cur(bwd)],
        out_shape=[jax.ShapeDtypeStruct((rows_total, c), F32)] * 2,
        scratch_shapes=[pltpu.VMEM((rows + (CONV_W - 1) * nb, c), F32),
                        pltpu.VMEM((2, rows, c), F32),
                        pltpu.VMEM((2, rows, c), F32),
                        pltpu.VMEM((2, nb, c), F32)],
        compiler_params=_cp("arbitrary"),
        name="rglru",
    )(u_tm, u_tm, u_tm, u_tm, u_tm, u_tm, conv_w, conv_b, gate_w, gate_b, lam)


def _ret_kernel(lg_ref, qf_ref, kf_ref, vf_ref, qb_ref, kb_ref, vb_ref, of_ref, ob_ref, s_scr, d_scr, *, c):
    s = pl.program_id(1)
    ii = lax.broadcasted_iota(jnp.int32, (c, c), 0).astype(F32)
    jj = lax.broadcasted_iota(jnp.int32, (c, c), 1).astype(F32)

    @pl.when(s == 0)
    def _():
        s_scr[...] = jnp.zeros_like(s_scr)
        for d in range(2):
            diff = (ii - jj) if d == 0 else (jj - ii)
            for h in range(RET_HEADS):
                d_scr[d, h] = jnp.where(diff >= 0, jnp.exp(jnp.maximum(diff, 0.0) * lg_ref[d, h]), 0.0)

    ri = lax.broadcasted_iota(jnp.int32, (c, 1), 0).astype(F32)
    views = ((qf_ref, kf_ref, vf_ref, of_ref), (qb_ref, kb_ref, vb_ref, ob_ref))
    for d, (q_ref, k_ref, v_ref, o_ref) in enumerate(views):
        for h in range(RET_HEADS):
            lg = lg_ref[d, h]
            cols = slice(h * RET_D, (h + 1) * RET_D)
            q, k, v = q_ref[0, :, cols], k_ref[0, :, cols], v_ref[0, :, cols]
            sc = lax.dot_general(q, k, (((1,), (1,)), ((), ())), preferred_element_type=F32) * d_scr[d, h]
            inner = _dot(sc.astype(BF16), v)
            st = s_scr[d, h]
            qdec = jnp.exp(((ri + 1.0) if d == 0 else (c - ri)) * lg)
            cross = _dot(q, st.astype(BF16)) * qdec
            o_ref[0, :, cols] = inner + cross
            kdec = jnp.exp(((c - 1.0 - ri) if d == 0 else ri) * lg)
            kd = (k.astype(F32) * kdec).T.astype(BF16)
            gc = jnp.exp(jnp.zeros((RET_D, RET_D), F32) + c * lg)
            s_scr[d, h] = gc * st + _dot(kd, v)


def _retention(log_g, zb, nlat):
    b, l, _ = zb.shape
    nj = l // TB
    w = RET_HEADS * RET_D

    def fwd(s):
        return jnp.where(s < nj - nlat, nlat + s, s - (nj - nlat))

    def bwd(s):
        return nj - 1 - s

    def col(f, i):
        return pl.BlockSpec((1, TB, w), lambda bi, s: (bi, f(s), i))

    def out(f):
        return pl.BlockSpec((1, TB, w), lambda bi, s: (bi, f(s), 0))

    return pl.pallas_call(
        functools.partial(_ret_kernel, c=TB),
        grid=(b, nj),
        in_specs=[pl.BlockSpec(memory_space=pltpu.SMEM),
                  col(fwd, 1), col(fwd, 2), col(fwd, 3), col(bwd, 1), col(bwd, 2), col(bwd, 3)],
        out_specs=[out(fwd), out(bwd)],
        out_shape=[jax.ShapeDtypeStruct((b, l, w), F32)] * 2,
        scratch_shapes=[pltpu.VMEM((2, RET_HEADS, RET_D, RET_D), F32),
                        pltpu.VMEM((2, RET_HEADS, TB, TB), F32)],
        compiler_params=_cp("arbitrary", "arbitrary"),
        name="retention",
    )(log_g, zb, zb, zb, zb, zb, zb)


def _s5_kernel(uf_ref, ub_ref, bm_ref, ar_ref, ai_ref, cm_ref, yf_ref, yb_ref, bu_scr, h_scr, *, tt, nb, ns):
    s = pl.program_id(0)

    @pl.when(s == 0)
    def _():
        h_scr[...] = jnp.zeros_like(h_scr)

    cg = 512
    for d, (u_ref, y_ref) in enumerate(((uf_ref, yf_ref), (ub_ref, yb_ref))):
        bu_scr[...] = _dot(u_ref[...].astype(BF16), bm_ref[d])
        for g in range(ns // cg):
            re = slice(g * cg, (g + 1) * cg)
            im = slice(ns + g * cg, ns + (g + 1) * cg)
            ar = ar_ref[d, :, re]
            ai = ai_ref[d, :, re]

            def body(i, carry, re=re, im=im, ar=ar, ai=ai, d=d):
                hr, hi = carry
                t = i if d == 0 else tt - 1 - i
                r0 = pl.multiple_of(t * nb, nb)
                nr = ar * hr - ai * hi + bu_scr[pl.ds(r0, nb), re]
                ni = ar * hi + ai * hr + bu_scr[pl.ds(r0, nb), im]
                bu_scr[pl.ds(r0, nb), re] = nr
                bu_scr[pl.ds(r0, nb), im] = ni
                return nr, ni

            hr, hi = lax.fori_loop(0, tt, body, (h_scr[d, :, re], h_scr[d, :, im]), unroll=4)
            h_scr[d, :, re] = hr
            h_scr[d, :, im] = hi
        y_ref[...] = _dot(bu_scr[...].astype(BF16), cm_ref[d])


def _s5(u_tm, bmat, a_re, a_im, cmat, nb, nlat, nctx):
    rows_total, c = u_tm.shape
    tt = S5_TT
    nlat_c, nctx_c = nlat // tt, nctx // tt
    rows = tt * nb
    ns = a_re.shape[-1]
    fwd = functools.partial(_fwd_chunk, nlat_c=nlat_c, nctx_c=nctx_c)
    bwd = functools.partial(_bwd_chunk, nlat_c=nlat_c, nctx_c=nctx_c)

    def cur(f):
        return pl.BlockSpec((rows, c), lambda s: (f(s), 0))

    def whole(a):
        return pl.BlockSpec(a.shape, lambda s: (0,) * a.ndim)

    return pl.pallas_call(
        functools.partial(_s5_kernel, tt=tt, nb=nb, ns=ns),
        grid=(nlat_c + nctx_c,),
        in_specs=[cur(fwd), cur(bwd), whole(bmat), whole(a_re), whole(a_im), whole(cmat)],
        out_specs=[cur(fwd), cur(bwd)],
        out_shape=[jax.ShapeDtypeStruct((rows_total, c), F32)] * 2,
        scratch_shapes=[pltpu.VMEM((rows, 2 * ns), F32), pltpu.VMEM((2, nb, 2 * ns), F32)],
        compiler_params=_cp("arbitrary"),
        name="s5",
    )(u_tm, u_tm, bmat, a_re, a_im, cmat)


def _attn_kernel(lam_ref, q_ref, k_ref, v_ref, gn_ref, o_ref, m_scr, acc_scr, *, tk, nk, out_scale):
    tq = q_ref.shape[1]
    q = q_ref[0]
    lane = lax.broadcasted_iota(jnp.int32, (tq, LANES), 1)
    zero = jnp.zeros_like(q)
    qm = (jnp.where(lane < DIFF_DH, q, zero), jnp.where(lane >= DIFF_DH, q, zero))
    m_scr[...] = jnp.full(m_scr.shape, NEG, F32)
    acc_scr[...] = jnp.zeros_like(acc_scr)
    ones = jnp.ones((tk, LANES), BF16)

    def body(ci, carry):
        r0 = pl.multiple_of(ci * tk, tk)
        kc = k_ref[0, pl.ds(r0, tk), :]
        va = jnp.concatenate([v_ref[0, pl.ds(r0, tk), :], ones], axis=1)
        for m in range(2):
            sc = lax.dot_general(qm[m], kc, (((1,), (1,)), ((), ())), preferred_element_type=F32)
            m_old = m_scr[m]
            m_new = jnp.maximum(m_old, jnp.max(sc, axis=1, keepdims=True))
            alpha = jnp.exp(m_old - m_new)
            p = jnp.exp(sc - m_new).astype(BF16)
            acc_scr[m] = alpha * acc_scr[m] + _dot(p, va)
            m_scr[m] = m_new
        return carry

    lax.fori_loop(0, nk, body, 0)
    a1, a2 = acc_scr[0], acc_scr[1]
    o1 = a1[:, :LANES] / a1[:, LANES:LANES + 1]
    o2 = a2[:, :LANES] / a2[:, LANES:LANES + 1]
    att = o1 - lam_ref[0] * o2
    att = att * lax.rsqrt(jnp.mean(att * att, axis=-1, keepdims=True) + EPS) * (gn_ref[0] * out_scale)
    o_ref[0] = att.astype(BF16)


def _attention(lam, zb, gn, nlat, out_scale):
    b, l, _ = zb.shape
    h = DIFF_HEADS
    nctx = l - nlat
    tq = ATTN_TQ if nlat % ATTN_TQ == 0 else TB
    tk = ATTN_TK if l % ATTN_TK == 0 else TB
    gn3 = gn.reshape(h, 1, LANES)

    def call(tq_, tk_, nq, kv_rows, q0, kv0, name):
        return pl.pallas_call(
            functools.partial(_attn_kernel, tk=tk_, nk=kv_rows // tk_, out_scale=out_scale),
            grid=(b, h, nq),
            in_specs=[pl.BlockSpec(memory_space=pltpu.SMEM),
                      pl.BlockSpec((1, tq_, LANES), lambda bi, hi, qi: (bi, q0 + qi, hi)),
                      pl.BlockSpec((1, kv_rows, LANES), lambda bi, hi, qi: (bi, kv0, h + hi)),
                      pl.BlockSpec((1, kv_rows, LANES), lambda bi, hi, qi: (bi, kv0, 2 * h + hi)),
                      pl.BlockSpec((1, 1, LANES), lambda bi, hi, qi: (hi, 0, 0))],
            out_specs=pl.BlockSpec((1, tq_, LANES), lambda bi, hi, qi: (bi, qi, hi)),
            out_shape=jax.ShapeDtypeStruct((b, nq * tq_, h * LANES), BF16),
            scratch_shapes=[pltpu.VMEM((2, tq_, 1), F32), pltpu.VMEM((2, tq_, 2 * LANES), F32)],
            compiler_params=_cp("arbitrary", "arbitrary", "arbitrary"),
            name=name,
        )(lam, zb, zb, zb, gn3)

    att_x = call(tq, tk, nlat // tq, l, 0, 0, "diffattn_latent")
    att_c = call(nctx, nctx, 1, nctx, nlat // nctx, nlat // nctx, "diffattn_context")
    return jnp.concatenate([att_x, att_c], axis=1)


def _out_tail(parts, w_ref, s_ref, mod, ng_ref, xo_ref):
    y, r0 = None, 0
    for p in parts:
        n = p.shape[1]
        t = _dot(p.astype(BF16), w_ref[r0:r0 + n, :])
        y = t if y is None else y + t
        r0 += n
    xn = s_ref[0] + mod[2:3] * _rms(y, ng_ref[1:2])
    xo_ref[0] = xn
    return _rms(xn, ng_ref[2:3]) * (1.0 + mod[4:5]) + mod[3:4]


def _outproj_even_kernel(lf_ref, lb_ref, zg_ref, zo_ref, rf_ref, rb_ref, gn_ref, w_ref, s_ref, mod_ref, ng_ref,
                         xo_ref, h2_ref):
    parts = [(lf_ref[...] + lb_ref[...]) * _gelu(zg_ref[0].astype(F32))]
    r = rf_ref[0] + rb_ref[0]
    og = _silu(zo_ref[0].astype(F32))
    for h in range(RET_HEADS):
        cols = slice(h * RET_D, (h + 1) * RET_D)
        y = r[:, cols]
        y = y - jnp.mean(y, axis=-1, keepdims=True)
        y = y * lax.rsqrt(jnp.mean(y * y, axis=-1, keepdims=True) + EPS) * gn_ref[:, cols]
        parts.append(y * og[:, cols])
    h2 = _out_tail(parts, w_ref, s_ref, mod_ref[0, 0], ng_ref, xo_ref)
    h2_ref[0] = h2.astype(BF16)


def _outproj_odd_kernel(yf_ref, yb_ref, u_ref, att_ref, d_ref, gw_ref, gb_ref, w_ref, s_ref, mod_ref, ng_ref,
                        rw_ref, rb_ref, xo_ref, h2_ref, gates_ref):
    z = _gelu(yf_ref[...] + yb_ref[...] + d_ref[...] * u_ref[...])
    s5o = z * _sigmoid(_dot(z.astype(BF16), gw_ref[...]) + gb_ref[...])
    h2 = _out_tail([s5o, att_ref[0]], w_ref, s_ref, mod_ref[0, 0], ng_ref, xo_ref)
    h2_ref[0] = h2.astype(BF16)
    logits = _dot3(h2, rw_ref[...]) + rb_ref[...]
    lane = lax.broadcasted_iota(jnp.int32, logits.shape, 1).astype(F32)
    m1 = jnp.max(logits, axis=1, keepdims=True)
    i1 = jnp.min(jnp.where(logits == m1, lane, float(LANES)), axis=1, keepdims=True)
    rest = jnp.where(lane == i1, 2.0 * NEG, logits)
    m2 = jnp.max(rest, axis=1, keepdims=True)
    i2 = jnp.min(jnp.where(rest == m2, lane, float(LANES)), axis=1, keepdims=True)
    e = jnp.exp(m2 - m1)
    p1 = 1.0 / (1.0 + e)
    gates_ref[0] = jnp.where(lane == i1, p1, 0.0) + jnp.where(lane == i2, e * p1, 0.0)


def _whole2(a):
    return pl.BlockSpec(a.shape, lambda i, j: (0,) * a.ndim)


def _mod_spec(d, nlat_t):
    return pl.BlockSpec((1, 1, 6, d), lambda i, j: (i, jnp.where(j < nlat_t, 1, 0), 0, 0))


def _outproj_even(lf2d, lb2d, zb, rf, rb, gn, w, s, modtab, ng, nlat_t):
    b, l, d = s.shape
    cw = RET_HEADS * RET_D
    tm = pl.BlockSpec((TB, cw), lambda i, j: (j, i))
    row = pl.BlockSpec((1, TB, d), lambda i, j: (i, j, 0))
    return pl.pallas_call(
        _outproj_even_kernel,
        grid=(b, l // TB),
        in_specs=[tm, tm,
                  pl.BlockSpec((1, TB, cw), lambda i, j: (i, j, 0)),
                  pl.BlockSpec((1, TB, cw), lambda i, j: (i, j, 4)),
                  pl.BlockSpec((1, TB, cw), lambda i, j: (i, j, 0)),
                  pl.BlockSpec((1, TB, cw), lambda i, j: (i, j, 0)),
                  _whole2(gn), _whole2(w), row, _mod_spec(d, nlat_t), _whole2(ng)],
        out_specs=[row, row],
        out_shape=[jax.ShapeDtypeStruct((b, l, d), F32), jax.ShapeDtypeStruct((b, l, d), BF16)],
        compiler_params=_cp("arbitrary", "arbitrary"),
        name="outproj_even",
    )(lf2d, lb2d, zb, zb, rf, rb, gn, w, s, modtab, ng)


def _outproj_odd(yf2d, yb2d, u2d, att, dskip, glu_w, glu_b, w, s, modtab, ng, rw, rb, nlat_t):
    b, l, d = s.shape
    sw = dskip.shape[1]
    aw = att.shape[2]
    tm = pl.BlockSpec((TB, sw), lambda i, j: (j, i))
    row = pl.BlockSpec((1, TB, d), lambda i, j: (i, j, 0))
    return pl.pallas_call(
        _outproj_odd_kernel,
        grid=(b, l // TB),
        in_specs=[tm, tm, tm,
                  pl.BlockSpec((1, TB, aw), lambda i, j: (i, j, 0)),
                  _whole2(dskip), _whole2(glu_w), _whole2(glu_b), _whole2(w), row, _mod_spec(d, nlat_t),
                  _whole2(ng), _whole2(rw), _whole2(rb)],
        out_specs=[row, row, pl.BlockSpec((1, TB, LANES), lambda i, j: (i, j, 0))],
        out_shape=[jax.ShapeDtypeStruct((b, l, d), F32), jax.ShapeDtypeStruct((b, l, d), BF16),
                   jax.ShapeDtypeStruct((b, l, LANES), F32)],
        compiler_params=_cp("arbitrary", "arbitrary"),
        name="outproj_odd",
    )(yf2d, yb2d, u2d, att, dskip, glu_w, glu_b, w, s, modtab, ng, rw, rb)


def _ffn_kernel(*refs, gated):
    if gated:
        x_ref, wg_ref, wu_ref, wd_ref, gate_ref, y_ref, acc_ref = refs
    else:
        x_ref, wg_ref, wu_ref, wd_ref, y_ref, acc_ref = refs
    e, f = pl.program_id(1), pl.program_id(2)

    @pl.when((e == 0) & (f == 0))
    def _():
        acc_ref[...] = jnp.zeros_like(acc_ref)

    x = x_ref[...]
    a = _dot(x, wg_ref[0])
    act = _silu(a) * _dot(x, wu_ref[0])
    if gated:
        g = gate_ref[...]
        lane = lax.broadcasted_iota(jnp.int32, g.shape, 1)
        act = act * jnp.sum(jnp.where(lane == e, g, 0.0), axis=1, keepdims=True)
    acc_ref[...] += _dot(act.astype(BF16), wd_ref[0])

    @pl.when((e == pl.num_programs(1) - 1) & (f == pl.num_programs(2) - 1))
    def _():
        y_ref[...] = acc_ref[...]


def _ffn(x, wg, wu, wd, gates=None):
    n, d = x.shape
    ne, _, ff = wg.shape
    tm, tf = FFN_TM, FFN_TF
    in_specs = [pl.BlockSpec((tm, d), lambda i, e, f: (i, 0)),
                pl.BlockSpec((1, d, tf), lambda i, e, f: (e, 0, f)),
                pl.BlockSpec((1, d, tf), lambda i, e, f: (e, 0, f)),
                pl.BlockSpec((1, tf, d), lambda i, e, f: (e, f, 0))]
    args = [x, wg, wu, wd]
    if gates is not None:
        in_specs.append(pl.BlockSpec((tm, LANES), lambda i, e, f: (i, 0)))
        args.append(gates)
    return pl.pallas_call(
        functools.partial(_ffn_kernel, gated=gates is not None),
        grid=(n // tm, ne, ff // tf),
        in_specs=in_specs,
        out_specs=pl.BlockSpec((tm, d), lambda i, e, f: (i, 0)),
        out_shape=jax.ShapeDtypeStruct((n, d), F32),
        scratch_shapes=[pltpu.VMEM((tm, d), F32)],
        compiler_params=_cp("arbitrary", "arbitrary", "arbitrary"),
        name="moe_ffn" if gates is not None else "dense_ffn",
    )(*args)


def _resid_kernel(s_ref, y_ref, mod_ref, g_ref, o_ref):
    o_ref[0] = s_ref[0] + mod_ref[0, 0][5:6] * _rms(y_ref[0], g_ref[...])


def _resid(s, y, modtab, g, nlat_t):
    b, l, d = s.shape
    row = pl.BlockSpec((1, TB, d), lambda i, j: (i, j, 0))
    return pl.pallas_call(
        _resid_kernel,
        grid=(b, l // TB),
        in_specs=[row, row, _mod_spec(d, nlat_t), _whole2(g)],
        out_specs=row,
        out_shape=jax.ShapeDtypeStruct((b, l, d), F32),
        compiler_params=_cp("arbitrary", "arbitrary"),
        name="ffn_residual",
    )(s, y, modtab, g)


def _rope_tables(seq, dim):
    t = jnp.arange(seq)
    rows = (t // GRID_W).astype(F32)
    cols = (t % GRID_W).astype(F32)
    quarter = dim // 4
    inv = ROPE_BASE ** (-jnp.arange(quarter, dtype=F32) / quarter)
    ang = jnp.concatenate([rows[:, None] * inv, cols[:, None] * inv], axis=-1)
    c, s = jnp.cos(ang), jnp.sin(ang)
    rep = LANES // dim
    return jnp.tile(jnp.concatenate([c, c], -1), (1, rep)), jnp.tile(jnp.concatenate([-s, s], -1), (1, rep))


def _block_diag(x):
    g, r, c = x.shape[-3:]
    y = jnp.einsum('...grc,gh->...grhc', x, jnp.eye(g, dtype=x.dtype))
    return y.reshape(x.shape[:-3] + (g * r, g * c))


def _s5_params(lam_re, lam_im, log_dt, b_re, b_im, c_re, c_im, nb):
    lam = lax.complex(lam_re, lam_im)
    a_bar = jnp.exp(lam * jnp.exp(log_dt)[..., None])
    b_bar = ((a_bar - 1.0) / lam)[..., None] * lax.complex(b_re, b_im)
    bt = jnp.swapaxes(b_bar, -1, -2)
    bmat = jnp.concatenate([_block_diag(jnp.real(bt)), _block_diag(jnp.imag(bt))], axis=-1).astype(BF16)
    cmat = jnp.concatenate([_block_diag(jnp.swapaxes(c_re, -1, -2)),
                            -_block_diag(jnp.swapaxes(c_im, -1, -2))], axis=-2).astype(BF16)
    ns = a_bar.shape[1] * a_bar.shape[2]
    a_re = jnp.broadcast_to(jnp.real(a_bar).reshape(2, 1, ns), (2, nb, ns))
    a_im = jnp.broadcast_to(jnp.imag(a_bar).reshape(2, 1, ns), (2, nb, ns))
    return bmat, a_re, a_im, cmat


def kernel(x, c, ctx, c_ctx, mod_w, mod_b, norm_g, ev_w_in, ev_conv_w, ev_conv_b, lru_gate_w, lru_gate_b, lru_lam, ret_theta, ret_gn, ev_w_out, ffd_w_gate, ffd_w_up, ffd_w_down, od_w_in, s5_lam_re, s5_lam_im, s5_log_dt, s5_b_re, s5_b_im, s5_c_re, s5_c_im, s5_d, s5_glu_w, s5_glu_b, diff_lam, diff_gn, od_w_out, moe_router_w, moe_router_b, moe_w_gate, moe_w_up, moe_w_down):
    b, seq, d = x.shape
    nctx = ctx.shape[1]
    l = seq + nctx
    n = b * l
    depth = mod_w.shape[0]
    nlat_t = seq // TB
    assert b == SUBLANES and seq % TB == 0 and nctx % TB == 0 and seq % nctx == 0 and n % FFN_TM == 0

    s = jnp.concatenate([x, ctx], axis=1)
    cc = jnp.concatenate([c, c_ctx[None], jnp.zeros((2 * SUBLANES - b - 1, d), F32)], axis=0)
    mt = _modtab(cc, mod_w, mod_b).reshape(depth, 2 * SUBLANES, 6, d)
    modtab = jnp.stack([jnp.broadcast_to(mt[:, b:b + 1], (depth, b, 6, d)), mt[:, :b]], axis=2)

    cos_r, sin_r = _rope_tables(seq, RET_D)
    cos_d, sin_d = _rope_tables(seq, DIFF_DH)

    for layer in range(depth):
        i = layer // 2
        mtab = modtab[layer]
        ng = norm_g[layer]
        if layer % 2 == 0:
            lw = ev_conv_w.shape[-1]
            u2d, zb = _inproj(functools.partial(_inproj_even_kernel, nlat=nlat_t, kscale=RET_D ** -0.5),
                              s, mtab, ng[0:1], ev_w_in[i].astype(BF16), cos_r, sin_r, lw, 5 * lw, nlat_t,
                              "inproj_even")
            gw = _block_diag(lru_gate_w[i]).astype(BF16)
            lf, lb = _lru(u2d.reshape(l * b, lw), ev_conv_w[i].reshape(CONV_W, 1, lw), ev_conv_b[i].reshape(1, lw),
                          gw, lru_gate_b[i].reshape(2, 2, 1, lw), lru_lam[i].reshape(2, 1, lw), b, seq, nctx)
            rf, rb = _retention(jax.nn.log_sigmoid(ret_theta[i].astype(F32)), zb, nlat_t)
            s, h2 = _outproj_even(lf.reshape(l, b * lw), lb.reshape(l, b * lw), zb, rf, rb, ret_gn[i].reshape(1, -1),
                                  ev_w_out[i].astype(BF16), s, mtab, ng, nlat_t)
            y = _ffn(h2.reshape(n, d), ffd_w_gate[i].astype(BF16)[None], ffd_w_up[i].astype(BF16)[None],
                     ffd_w_down[i].astype(BF16)[None])
        else:
            sw = s5_d.shape[-1]
            aw = DIFF_HEADS * 2 * DIFF_DH
            lam_init = 0.8 - 0.6 * math.exp(-0.3 * layer)
            u2d, zb = _inproj(functools.partial(_inproj_odd_kernel, nlat=nlat_t, qscale=DIFF_DH ** -0.5, s5w=sw),
                              s, mtab, ng[0:1], od_w_in[i].astype(BF16), cos_d, sin_d, sw, 3 * aw, nlat_t,
                              "inproj_odd")
            bmat, a_re, a_im, cmat = _s5_params(s5_lam_re[i], s5_lam_im[i], s5_log_dt[i], s5_b_re[i], s5_b_im[i],
                                                s5_c_re[i], s5_c_im[i], b)
            yf, yb = _s5(u2d.reshape(l * b, sw), bmat, a_re, a_im, cmat, b, seq, nctx)
            lp = diff_lam[i].astype(F32)
            lam = (jnp.exp(jnp.sum(lp[0] * lp[1])) - jnp.exp(jnp.sum(lp[2] * lp[3])) + lam_init).reshape(1)
            att = _attention(lam, zb, diff_gn[i], seq, 1.0 - lam_init)
            rw = jnp.zeros((d, LANES), F32).at[:, :N_EXPERTS].set(moe_router_w[i])
            rbias = jnp.full((1, LANES), NEG, F32).at[0, :N_EXPERTS].set(moe_router_b[i])
            s, h2, gates = _outproj_odd(yf.reshape(l, b * sw), yb.reshape(l, b * sw), u2d, att,
                                        s5_d[i].reshape(1, sw), s5_glu_w[i].astype(BF16), s5_glu_b[i].reshape(1, sw),
                                        od_w_out[i].astype(BF16), s, mtab, ng, rw, rbias, nlat_t)
            y = _ffn(h2.reshape(n, d), moe_w_gate[i].astype(BF16), moe_w_up[i].astype(BF16),
                     moe_w_down[i].astype(BF16), gates.reshape(n, LANES))
        s = _resid(s, y.reshape(b, l, d), mtab, ng[3:4], nlat_t)
    return s[:, :seq]
```

```python
import functools
import math

import jax
import jax.numpy as jnp
from jax import lax
from jax.experimental import pallas as pl
from jax.experimental.pallas import tpu as pltpu

F32 = jnp.float32
BF16 = jnp.bfloat16

EPS = 1e-6
ROPE_BASE = 10000.0
GRID_W = 64
LRU_C = 8.0
LRU_BLOCKS = 8
CONV_W = 4
CONV_LEFT = 2
RET_HEADS = 4
RET_D = 128
S5_GROUP = 16
S5_STATE = 64
DIFF_HEADS = 6
DIFF_DH = 64
N_EXPERTS = 8
LANES = 128
SUBLANES = 8

TB = 256
LRU_TT = 64
S5_TT = 32
ATTN_TQ = 512
ATTN_RB = 128
FFN_TM = 1024
FFN_TF = 512
MOE_TM = 256
MOE_ALIGN = 16
MOE_WIN = TB + LANES
VMEM_LIMIT = 56 * 1024 * 1024
NEG = -1e30


def _cp(*sem):
    return pltpu.CompilerParams(dimension_semantics=sem, vmem_limit_bytes=VMEM_LIMIT)


def _dot(a, b):
    return jnp.dot(a, b, preferred_element_type=F32)


def _split(x):
    hi = x.astype(BF16)
    lo = (x - hi.astype(F32)).astype(BF16)
    return hi, lo


def _dot3(a, w):
    ah, al = _split(a)
    wh, wl = _split(w)
    return _dot(ah, wh) + (_dot(ah, wl) + _dot(al, wh))


def _rms(x, g):
    return x * lax.rsqrt(jnp.mean(x * x, axis=-1, keepdims=True) + EPS) * g


def _sigmoid(x):
    return 1.0 / (1.0 + jnp.exp(-x))


def _silu(x):
    return x * _sigmoid(x)


def _gelu(x):
    return 0.5 * x * (1.0 + jnp.tanh(math.sqrt(2.0 / math.pi) * (x + 0.044715 * (x * x * x))))


def _softplus(x):
    return jnp.maximum(x, 0.0) + jnp.log(1.0 + jnp.exp(-jnp.abs(x)))


def _modtab_kernel(c_ref, w_ref, b_ref, o_ref):
    c = c_ref[...]
    o_ref[0] = _dot3(_silu(c), w_ref[0]) + b_ref[0]


def _modtab(cc, mod_w, mod_b):
    depth, d, n = mod_w.shape
    tn = 1536
    return pl.pallas_call(
        _modtab_kernel,
        grid=(depth, n // tn),
        in_specs=[pl.BlockSpec((cc.shape[0], d), lambda l, j: (0, 0)),
                  pl.BlockSpec((1, d, tn), lambda l, j: (l, 0, j)),
                  pl.BlockSpec((1, 1, tn), lambda l, j: (l, 0, j))],
        out_specs=pl.BlockSpec((1, cc.shape[0], tn), lambda l, j: (l, 0, j)),
        out_shape=jax.ShapeDtypeStruct((depth, cc.shape[0], n), F32),
        compiler_params=_cp("arbitrary", "arbitrary"),
        name="modtab",
    )(cc, mod_w, mod_b.reshape(depth, 1, n))


def _norm_mod_in(s_ref, mod_ref, g_ref):
    mod = mod_ref[0, 0]
    h = _rms(s_ref[0], g_ref[...]) * (1.0 + mod[1:2]) + mod[0:1]
    return h.astype(BF16)


def _inproj_even_kernel(s_ref, mod_ref, g_ref, w_ref, cos_ref, sin_ref, u_ref, zb_ref, *, nlat, kscale):
    j = pl.program_id(1)
    hb = _norm_mod_in(s_ref, mod_ref, g_ref)
    lw = RET_HEADS * RET_D

    def proj(i):
        return _dot(hb, w_ref[:, i * lw:(i + 1) * lw])

    u_ref[...] = proj(0)
    zb_ref[0, :, 0:lw] = proj(1).astype(BF16)
    lat = j < nlat
    cos = jnp.where(lat, cos_ref[...], 1.0)
    sin = jnp.where(lat, sin_ref[...], 0.0)
    for sec, scale in ((0, 1.0), (1, kscale)):
        t = proj(2 + sec)
        for hd in range(RET_HEADS):
            th = t[:, hd * RET_D:(hd + 1) * RET_D]
            th = (th * cos + pltpu.roll(th, RET_D // 2, 1) * sin) * scale
            zb_ref[0, :, (1 + sec) * lw + hd * RET_D:(1 + sec) * lw + (hd + 1) * RET_D] = th.astype(BF16)
    zb_ref[0, :, 3 * lw:4 * lw] = proj(4).astype(BF16)
    zb_ref[0, :, 4 * lw:5 * lw] = proj(5).astype(BF16)


def _inproj_odd_kernel(s_ref, mod_ref, g_ref, w_ref, cos_ref, sin_ref, u_ref, zb_ref, *, nlat, qscale, s5w):
    j = pl.program_id(1)
    hb = _norm_mod_in(s_ref, mod_ref, g_ref)
    aw = DIFF_HEADS * 2 * DIFF_DH
    u_ref[...] = _dot(hb, w_ref[:, 0:s5w])
    lat = j < nlat
    cos = jnp.where(lat, cos_ref[...], 1.0)
    sin = jnp.where(lat, sin_ref[...], 0.0)
    lane = lax.broadcasted_iota(jnp.int32, (TB, LANES), 1)
    first_half = (lane % DIFF_DH) < (DIFF_DH // 2)
    for sec, scale in ((0, qscale), (1, 1.0)):
        t = _dot(hb, w_ref[:, s5w + sec * aw:s5w + (sec + 1) * aw])
        for hd in range(DIFF_HEADS):
            th = t[:, hd * LANES:(hd + 1) * LANES]
            partner = jnp.where(first_half, pltpu.roll(th, LANES - DIFF_DH // 2, 1), pltpu.roll(th, DIFF_DH // 2, 1))
            th = (th * cos + partner * sin) * scale
            zb_ref[0, :, sec * aw + hd * LANES:sec * aw + (hd + 1) * LANES] = th.astype(BF16)
    zb_ref[0, :, 2 * aw:3 * aw] = _dot(hb, w_ref[:, s5w + 2 * aw:s5w + 3 * aw]).astype(BF16)


def _inproj(body, s, modtab, g, w, cos, sin, uw, zw, nlat, name):
    b, l, d = s.shape
    nj = l // TB
    return pl.pallas_call(
        body,
        grid=(b, nj),
        in_specs=[pl.BlockSpec((1, TB, d), lambda i, j: (i, j, 0)),
                  pl.BlockSpec((1, 1, 6, d), lambda i, j: (i, jnp.where(j < nlat, 1, 0), 0, 0)),
                  pl.BlockSpec((1, d), lambda i, j: (0, 0)),
                  pl.BlockSpec(w.shape, lambda i, j: (0, 0)),
                  pl.BlockSpec((TB, LANES), lambda i, j: (jnp.minimum(j, nlat - 1), 0)),
                  pl.BlockSpec((TB, LANES), lambda i, j: (jnp.minimum(j, nlat - 1), 0))],
        out_specs=[pl.BlockSpec((TB, uw), lambda i, j: (j, i)),
                   pl.BlockSpec((1, TB, zw), lambda i, j: (i, j, 0))],
        out_shape=[jax.ShapeDtypeStruct((l, b * uw), F32),
                   jax.ShapeDtypeStruct((b, l, zw), BF16)],
        compiler_params=_cp("arbitrary", "arbitrary"),
        name=name,
    )(s, modtab, g, w, cos, sin)


def _fwd_chunk(s, nlat_c, nctx_c):
    return jnp.where(s < nctx_c, nlat_c + s, s - nctx_c)


def _bwd_chunk(s, nlat_c, nctx_c):
    return nlat_c + nctx_c - 1 - s


def _lru_kernel(uf_ref, ufl_ref, ufr_ref, ub_ref, ubl_ref, ubr_ref, cw_ref, cb_ref, gw_ref, gb_ref, lam_ref,
                of_ref, ob_ref, pad_scr, a_scr, b_scr, h_scr, *, tt, nlat_c, nctx_c, nb):
    s = pl.program_id(0)
    nc = nlat_c + nctx_c

    @pl.when(s == 0)
    def _():
        h_scr[...] = jnp.zeros_like(h_scr)

    rows = tt * nb
    views = ((uf_ref, ufl_ref, ufr_ref, _fwd_chunk(s, nlat_c, nctx_c)),
             (ub_ref, ubl_ref, ubr_ref, _bwd_chunk(s, nlat_c, nctx_c)))
    for d, (u_ref, l_ref, r_ref, c) in enumerate(views):
        first = (c == 0) | (c == nlat_c)
        last = (c == nlat_c - 1) | (c == nc - 1)
        pad_scr[0:CONV_LEFT * nb] = jnp.where(first, 0.0, l_ref[...])
        pad_scr[CONV_LEFT * nb:CONV_LEFT * nb + rows] = u_ref[...]
        pad_scr[CONV_LEFT * nb + rows:(CONV_W - 1) * nb + rows] = jnp.where(last, 0.0, r_ref[...])
        uc = cb_ref[...] + pad_scr[0:rows] * cw_ref[0]
        for k in range(1, CONV_W):
            uc = uc + pad_scr[k * nb:k * nb + rows] * cw_ref[k]
        ucb = uc.astype(BF16)
        r = _sigmoid(_dot(ucb, gw_ref[d, 0]) + gb_ref[d, 0])
        i = _sigmoid(_dot(ucb, gw_ref[d, 1]) + gb_ref[d, 1])
        a = jnp.exp((-LRU_C * _softplus(-lam_ref[d])) * r)
        a_scr[d] = a
        b_scr[d] = jnp.sqrt(1.0 - a * a) * (i * uc)

    def body(t, carry):
        hf, hb = carry
        rf = pl.multiple_of(t * nb, nb)
        rb = pl.multiple_of((tt - 1 - t) * nb, nb)
        hf = a_scr[0, pl.ds(rf, nb), :] * hf + b_scr[0, pl.ds(rf, nb), :]
        of_ref[pl.ds(rf, nb), :] = hf
        hb = a_scr[1, pl.ds(rb, nb), :] * hb + b_scr[1, pl.ds(rb, nb), :]
        ob_ref[pl.ds(rb, nb), :] = hb
        return hf, hb

    hf, hb = lax.fori_loop(0, tt, body, (h_scr[0], h_scr[1]), unroll=8)
    h_scr[0] = hf
    h_scr[1] = hb


def _lru(u_tm, conv_w, conv_b, gate_w, gate_b, lam, nb, nlat, nctx):
    rows_total, c = u_tm.shape
    tt = LRU_TT
    nlat_c, nctx_c = nlat // tt, nctx // tt
    nc = nlat_c + nctx_c
    rows = tt * nb
    fwd = functools.partial(_fwd_chunk, nlat_c=nlat_c, nctx_c=nctx_c)
    bwd = functools.partial(_bwd_chunk, nlat_c=nlat_c, nctx_c=nctx_c)
    lrows, rrows = CONV_LEFT * nb, (CONV_W - 1 - CONV_LEFT) * nb
    nl, nr = rows // lrows, rows // rrows

    def cur(f):
        return pl.BlockSpec((rows, c), lambda s: (f(s), 0))

    def left(f):
        return pl.BlockSpec((lrows, c), lambda s: (jnp.maximum(f(s) * nl - 1, 0), 0))

    def right(f):
        return pl.BlockSpec((rrows, c), lambda s: (jnp.minimum((f(s) + 1) * nr, nc * nr - 1), 0))

    def whole(a):
        return pl.BlockSpec(a.shape, lambda s: (0,) * a.ndim)

    return pl.pallas_call(
        functools.partial(_lru_kernel, tt=tt, nlat_c=nlat_c, nctx_c=nctx_c, nb=nb),
        grid=(nc,),
        in_specs=[cur(fwd), left(fwd), right(fwd), cur(bwd), left(bwd), right(bwd),
                  whole(conv_w), whole(conv_b), whole(gate_w), whole(gate_b), whole(lam)],
        out_specs=[cur(fwd), cur(bwd)],
        out_shape=[jax.ShapeDtypeStruct((rows_total, c), F32)] * 2,
        scratch_shapes=[pltpu.VMEM((rows + (CONV_W - 1) * nb, c), F32),
                        pltpu.VMEM((2, rows, c), F32),
                        pltpu.VMEM((2, rows, c), F32),
                        pltpu.VMEM((2, nb, c), F32)],
        compiler_params=_cp("arbitrary"),
        name="rglru",
    )(u_tm, u_tm, u_tm, u_tm, u_tm, u_tm, conv_w, conv_b, gate_w, gate_b, lam)


def _ret_kernel(lg_ref, qf_ref, kf_ref, vf_ref, qb_ref, kb_ref, vb_ref, of_ref, ob_ref, s_scr, d_scr, *, c):
    s = pl.program_id(1)
    ii = lax.broadcasted_iota(jnp.int32, (c, c), 0).astype(F32)
    jj = lax.broadcasted_iota(jnp.int32, (c, c), 1).astype(F32)

    @pl.when(s == 0)
    def _():
        s_scr[...] = jnp.zeros_like(s_scr)
        for d in range(2):
            diff = (ii - jj) if d == 0 else (jj - ii)
            for h in range(RET_HEADS):
                d_scr[d, h] = jnp.where(diff >= 0, jnp.exp(jnp.maximum(diff, 0.0) * lg_ref[d, h]), 0.0)

    ri = lax.broadcasted_iota(jnp.int32, (c, 1), 0).astype(F32)
    views = ((qf_ref, kf_ref, vf_ref, of_ref), (qb_ref, kb_ref, vb_ref, ob_ref))
    for d, (q_ref, k_ref, v_ref, o_ref) in enumerate(views):
        for h in range(RET_HEADS):
            lg = lg_ref[d, h]
            cols = slice(h * RET_D, (h + 1) * RET_D)
            q, k, v = q_ref[0, :, cols], k_ref[0, :, cols], v_ref[0, :, cols]
            sc = lax.dot_general(q, k, (((1,), (1,)), ((), ())), preferred_element_type=F32) * d_scr[d, h]
            inner = _dot(sc.astype(BF16), v)
            st = s_scr[d, h]
            qdec = jnp.exp(((ri + 1.0) if d == 0 else (c - ri)) * lg)
            cross = _dot(q, st.astype(BF16)) * qdec
            o_ref[0, :, cols] = inner + cross
            kdec = jnp.exp(((c - 1.0 - ri) if d == 0 else ri) * lg)
            kd = (k.astype(F32) * kdec).T.astype(BF16)
            gc = jnp.exp(jnp.zeros((RET_D, RET_D), F32) + c * lg)
            s_scr[d, h] = gc * st + _dot(kd, v)


def _retention(log_g, zb, nlat):
    b, l, _ = zb.shape
    nj = l // TB
    w = RET_HEADS * RET_D

    def fwd(s):
        return jnp.where(s < nj - nlat, nlat + s, s - (nj - nlat))

    def bwd(s):
        return nj - 1 - s

    def col(f, i):
        return pl.BlockSpec((1, TB, w), lambda bi, s: (bi, f(s), i))

    def out(f):
        return pl.BlockSpec((1, TB, w), lambda bi, s: (bi, f(s), 0))

    return pl.pallas_call(
        functools.partial(_ret_kernel, c=TB),
        grid=(b, nj),
        in_specs=[pl.BlockSpec(memory_space=pltpu.SMEM),
                  col(fwd, 1), col(fwd, 2), col(fwd, 3), col(bwd, 1), col(bwd, 2), col(bwd, 3)],
        out_specs=[out(fwd), out(bwd)],
        out_shape=[jax.ShapeDtypeStruct((b, l, w), F32)] * 2,
        scratch_shapes=[pltpu.VMEM((2, RET_HEADS, RET_D, RET_D), F32),
                        pltpu.VMEM((2, RET_HEADS, TB, TB), F32)],
        compiler_params=_cp("arbitrary", "arbitrary"),
        name="retention",
    )(log_g, zb, zb, zb, zb, zb, zb)


def _s5_kernel(uf_ref, ub_ref, bm_ref, ar_ref, ai_ref, cm_ref, yf_ref, yb_ref, bu_scr, h_scr, *, tt, nb, ns):
    s = pl.program_id(0)

    @pl.when(s == 0)
    def _():
        h_scr[...] = jnp.zeros_like(h_scr)

    cg = 512
    for d, (u_ref, y_ref) in enumerate(((uf_ref, yf_ref), (ub_ref, yb_ref))):
        bu_scr[...] = _dot(u_ref[...].astype(BF16), bm_ref[d])
        for g in range(ns // cg):
            re = slice(g * cg, (g + 1) * cg)
            im = slice(ns + g * cg, ns + (g + 1) * cg)
            ar = ar_ref[d, :, re]
            ai = ai_ref[d, :, re]

            def body(i, carry, re=re, im=im, ar=ar, ai=ai, d=d):
                hr, hi = carry
                t = i if d == 0 else tt - 1 - i
                r0 = pl.multiple_of(t * nb, nb)
                nr = ar * hr - ai * hi + bu_scr[pl.ds(r0, nb), re]
                ni = ar * hi + ai * hr + bu_scr[pl.ds(r0, nb), im]
                bu_scr[pl.ds(r0, nb), re] = nr
                bu_scr[pl.ds(r0, nb), im] = ni
                return nr, ni

            hr, hi = lax.fori_loop(0, tt, body, (h_scr[d, :, re], h_scr[d, :, im]), unroll=4)
            h_scr[d, :, re] = hr
            h_scr[d, :, im] = hi
        y_ref[...] = _dot(bu_scr[...].astype(BF16), cm_ref[d])


def _s5(u_tm, bmat, a_re, a_im, cmat, nb, nlat, nctx):
    rows_total, c = u_tm.shape
    tt = S5_TT
    nlat_c, nctx_c = nlat // tt, nctx // tt
    rows = tt * nb
    ns = a_re.shape[-1]
    fwd = functools.partial(_fwd_chunk, nlat_c=nlat_c, nctx_c=nctx_c)
    bwd = functools.partial(_bwd_chunk, nlat_c=nlat_c, nctx_c=nctx_c)

    def cur(f):
        return pl.BlockSpec((rows, c), lambda s: (f(s), 0))

    def whole(a):
        return pl.BlockSpec(a.shape, lambda s: (0,) * a.ndim)

    return pl.pallas_call(
        functools.partial(_s5_kernel, tt=tt, nb=nb, ns=ns),
        grid=(nlat_c + nctx_c,),
        in_specs=[cur(fwd), cur(bwd), whole(bmat), whole(a_re), whole(a_im), whole(cmat)],
        out_specs=[cur(fwd), cur(bwd)],
        out_shape=[jax.ShapeDtypeStruct((rows_total, c), F32)] * 2,
        scratch_shapes=[pltpu.VMEM((rows, 2 * ns), F32), pltpu.VMEM((2, nb, 2 * ns), F32)],
        compiler_params=_cp("arbitrary"),
        name="s5",
    )(u_tm, u_tm, bmat, a_re, a_im, cmat)


def _attn_kernel(lam_ref, q_ref, k_ref, v_ref, gn_ref, o_ref, *, out_scale):
    tq, tk = q_ref.shape[1], k_ref.shape[1]
    q = q_ref[0]
    lane = lax.broadcasted_iota(jnp.int32, (tq, LANES), 1)
    zero = jnp.zeros_like(q)
    qm = (jnp.where(lane < DIFF_DH, q, zero), jnp.where(lane >= DIFF_DH, q, zero))
    kc = k_ref[0]
    va = jnp.concatenate([v_ref[0], jnp.ones((tk, LANES), BF16)], axis=1)
    rb = min(ATTN_RB, tq)
    for r in range(tq // rb):
        rows = slice(r * rb, (r + 1) * rb)
        outs = []
        for m in range(2):
            sc = lax.dot_general(qm[m][rows], kc, (((1,), (1,)), ((), ())), preferred_element_type=F32)
            p = jnp.exp2(sc - jnp.max(sc, axis=1, keepdims=True)).astype(BF16)
            a = _dot(p, va)
            outs.append(a[:, :LANES] / a[:, LANES:LANES + 1])
        att = outs[0] - lam_ref[0] * outs[1]
        att = att * lax.rsqrt(jnp.mean(att * att, axis=-1, keepdims=True) + EPS) * (gn_ref[0] * out_scale)
        o_ref[0, rows] = att.astype(BF16)


def _attention(lam, zb, gn, nlat, out_scale):
    b, l, _ = zb.shape
    h = DIFF_HEADS
    nctx = l - nlat
    tq = ATTN_TQ if nlat % ATTN_TQ == 0 else TB
    gn3 = gn.reshape(h, 1, LANES)

    def call(tq_, nq, kv_rows, q0, kv0, name):
        return pl.pallas_call(
            functools.partial(_attn_kernel, out_scale=out_scale),
            grid=(b, h, nq),
            in_specs=[pl.BlockSpec(memory_space=pltpu.SMEM),
                      pl.BlockSpec((1, tq_, LANES), lambda bi, hi, qi: (bi, q0 + qi, hi)),
                      pl.BlockSpec((1, kv_rows, LANES), lambda bi, hi, qi: (bi, kv0, h + hi)),
                      pl.BlockSpec((1, kv_rows, LANES), lambda bi, hi, qi: (bi, kv0, 2 * h + hi)),
                      pl.BlockSpec((1, 1, LANES), lambda bi, hi, qi: (hi, 0, 0))],
            out_specs=pl.BlockSpec((1, tq_, LANES), lambda bi, hi, qi: (bi, qi, hi)),
            out_shape=jax.ShapeDtypeStruct((b, nq * tq_, h * LANES), BF16),
            compiler_params=_cp("arbitrary", "arbitrary", "arbitrary"),
            name=name,
        )(lam, zb, zb, zb, gn3)

    att_x = call(tq, nlat // tq, l, 0, 0, "diffattn_latent")
    att_c = call(nctx, 1, nctx, nlat // nctx, nlat // nctx, "diffattn_context")
    return jnp.concatenate([att_x, att_c], axis=1)


def _out_tail(parts, w_ref, s_ref, mod, ng_ref, xo_ref):
    y, r0 = None, 0
    for p in parts:
        n = p.shape[1]
        t = _dot(p.astype(BF16), w_ref[r0:r0 + n, :])
        y = t if y is None else y + t
        r0 += n
    xn = s_ref[0] + mod[2:3] * _rms(y, ng_ref[1:2])
    xo_ref[0] = xn
    return _rms(xn, ng_ref[2:3]) * (1.0 + mod[4:5]) + mod[3:4]


def _outproj_even_kernel(lf_ref, lb_ref, zg_ref, zo_ref, rf_ref, rb_ref, gn_ref, w_ref, s_ref, mod_ref, ng_ref,
                         xo_ref, h2_ref):
    parts = [(lf_ref[...] + lb_ref[...]) * _gelu(zg_ref[0].astype(F32))]
    r = rf_ref[0] + rb_ref[0]
    og = _silu(zo_ref[0].astype(F32))
    for h in range(RET_HEADS):
        cols = slice(h * RET_D, (h + 1) * RET_D)
        y = r[:, cols]
        y = y - jnp.mean(y, axis=-1, keepdims=True)
        y = y * lax.rsqrt(jnp.mean(y * y, axis=-1, keepdims=True) + EPS) * gn_ref[:, cols]
        parts.append(y * og[:, cols])
    h2 = _out_tail(parts, w_ref, s_ref, mod_ref[0, 0], ng_ref, xo_ref)
    h2_ref[0] = h2.astype(BF16)


def _outproj_odd_kernel(yf_ref, yb_ref, u_ref, att_ref, d_ref, gw_ref, gb_ref, w_ref, s_ref, mod_ref, ng_ref,
                        rw_ref, rb_ref, xo_ref, h2_ref, gates_ref):
    z = _gelu(yf_ref[...] + yb_ref[...] + d_ref[...] * u_ref[...])
    s5o = z * _sigmoid(_dot(z.astype(BF16), gw_ref[...]) + gb_ref[...])
    h2 = _out_tail([s5o, att_ref[0]], w_ref, s_ref, mod_ref[0, 0], ng_ref, xo_ref)
    h2_ref[0] = h2.astype(BF16)
    logits = _dot3(h2, rw_ref[...]) + rb_ref[...]
    lane = lax.broadcasted_iota(jnp.int32, logits.shape, 1).astype(F32)
    m1 = jnp.max(logits, axis=1, keepdims=True)
    i1 = jnp.min(jnp.where(logits == m1, lane, float(LANES)), axis=1, keepdims=True)
    rest = jnp.where(lane == i1, 2.0 * NEG, logits)
    m2 = jnp.max(rest, axis=1, keepdims=True)
    i2 = jnp.min(jnp.where(rest == m2, lane, float(LANES)), axis=1, keepdims=True)
    e = jnp.exp(m2 - m1)
    p1 = 1.0 / (1.0 + e)
    gates = jnp.where(lane == i1, p1, 0.0) + jnp.where(lane == i2, e * p1, 0.0)
    ids = jnp.where(lane == float(N_EXPERTS), i1, 0.0) + jnp.where(lane == float(N_EXPERTS + 1), i2, 0.0)
    gates_ref[0] = gates + ids


def _whole2(a):
    return pl.BlockSpec(a.shape, lambda i, j: (0,) * a.ndim)


def _mod_spec(d, nlat_t):
    return pl.BlockSpec((1, 1, 6, d), lambda i, j: (i, jnp.where(j < nlat_t, 1, 0), 0, 0))


def _outproj_even(lf2d, lb2d, zb, rf, rb, gn, w, s, modtab, ng, nlat_t):
    b, l, d = s.shape
    cw = RET_HEADS * RET_D
    tm = pl.BlockSpec((TB, cw), lambda i, j: (j, i))
    row = pl.BlockSpec((1, TB, d), lambda i, j: (i, j, 0))
    return pl.pallas_call(
        _outproj_even_kernel,
        grid=(b, l // TB),
        in_specs=[tm, tm,
                  pl.BlockSpec((1, TB, cw), lambda i, j: (i, j, 0)),
                  pl.BlockSpec((1, TB, cw), lambda i, j: (i, j, 4)),
                  pl.BlockSpec((1, TB, cw), lambda i, j: (i, j, 0)),
                  pl.BlockSpec((1, TB, cw), lambda i, j: (i, j, 0)),
                  _whole2(gn), _whole2(w), row, _mod_spec(d, nlat_t), _whole2(ng)],
        out_specs=[row, row],
        out_shape=[jax.ShapeDtypeStruct((b, l, d), F32), jax.ShapeDtypeStruct((b, l, d), BF16)],
        compiler_params=_cp("arbitrary", "arbitrary"),
        name="outproj_even",
    )(lf2d, lb2d, zb, zb, rf, rb, gn, w, s, modtab, ng)


def _outproj_odd(yf2d, yb2d, u2d, att, dskip, glu_w, glu_b, w, s, modtab, ng, rw, rb, nlat_t):
    b, l, d = s.shape
    sw = dskip.shape[1]
    aw = att.shape[2]
    tm = pl.BlockSpec((TB, sw), lambda i, j: (j, i))
    row = pl.BlockSpec((1, TB, d), lambda i, j: (i, j, 0))
    return pl.pallas_call(
        _outproj_odd_kernel,
        grid=(b, l // TB),
        in_specs=[tm, tm, tm,
                  pl.BlockSpec((1, TB, aw), lambda i, j: (i, j, 0)),
                  _whole2(dskip), _whole2(glu_w), _whole2(glu_b), _whole2(w), row, _mod_spec(d, nlat_t),
                  _whole2(ng), _whole2(rw), _whole2(rb)],
        out_specs=[row, row, pl.BlockSpec((1, TB, LANES), lambda i, j: (i, j, 0))],
        out_shape=[jax.ShapeDtypeStruct((b, l, d), F32), jax.ShapeDtypeStruct((b, l, d), BF16),
                   jax.ShapeDtypeStruct((b, l, LANES), F32)],
        compiler_params=_cp("arbitrary", "arbitrary"),
        name="outproj_odd",
    )(yf2d, yb2d, u2d, att, dskip, glu_w, glu_b, w, s, modtab, ng, rw, rb)


def _ffn_kernel(x_ref, wg_ref, wu_ref, wd_ref, y_ref, acc_ref):
    f = pl.program_id(1)

    @pl.when(f == 0)
    def _():
        acc_ref[...] = jnp.zeros_like(acc_ref)

    x = x_ref[...]
    act = _silu(_dot(x, wg_ref[...])) * _dot(x, wu_ref[...])
    acc_ref[...] += _dot(act.astype(BF16), wd_ref[...])

    @pl.when(f == pl.num_programs(1) - 1)
    def _():
        y_ref[...] = acc_ref[...]


def _ffn(x, wg, wu, wd):
    n, d = x.shape
    ff = wg.shape[1]
    tm, tf = FFN_TM, FFN_TF
    return pl.pallas_call(
        _ffn_kernel,
        grid=(n // tm, ff // tf),
        in_specs=[pl.BlockSpec((tm, d), lambda i, f: (i, 0)),
                  pl.BlockSpec((d, tf), lambda i, f: (0, f)),
                  pl.BlockSpec((d, tf), lambda i, f: (0, f)),
                  pl.BlockSpec((tf, d), lambda i, f: (f, 0))],
        out_specs=pl.BlockSpec((tm, d), lambda i, f: (i, 0)),
        out_shape=jax.ShapeDtypeStruct((n, d), F32),
        scratch_shapes=[pltpu.VMEM((tm, d), F32)],
        compiler_params=_cp("arbitrary", "arbitrary"),
        name="dense_ffn",
    )(x, wg, wu, wd)


def _moe_plan(route, n_tiles):
    n = route.shape[0]
    nb = n // TB
    ne = N_EXPERTS
    e1 = route[:, ne].astype(jnp.int32)
    e2 = route[:, ne + 1].astype(jnp.int32)
    ex = jnp.arange(ne, dtype=jnp.int32)
    sel = ((e1[:, None] == ex) | (e2[:, None] == ex)).astype(jnp.int32)
    incl = jnp.cumsum(sel, axis=0)
    count = incl[-1]
    gsize = (count + MOE_TM - 1) // MOE_TM * MOE_TM
    gend = jnp.cumsum(gsize)
    goff = gend - gsize
    posm = goff[None, :] + incl - sel
    pos1 = jnp.sum(jnp.where(e1[:, None] == ex, posm, 0), axis=1)
    pos2 = jnp.sum(jnp.where(e2[:, None] == ex, posm, 0), axis=1)
    pos_cols = jnp.stack([pos1, pos2], axis=1)
    pos_rows = jnp.stack([pos1.reshape(nb, TB), pos2.reshape(nb, TB)], axis=1)
    start = posm[::TB]
    last = (goff + count)[None]
    end = jnp.concatenate([start[1:], last], axis=0)
    r0 = jnp.arange(n_tiles, dtype=jnp.int32) * MOE_TM
    tile_valid = (r0 < gend[-1]).astype(jnp.int32)
    tile_exp = jnp.minimum(jnp.sum((r0[:, None] >= gend[None, :]).astype(jnp.int32), axis=1), ne - 1)
    st, en = start.T[tile_exp], end.T[tile_exp]
    lim = jnp.minimum(r0 + MOE_TM, last[0][tile_exp])
    tb_lo = jnp.clip(jnp.sum((en <= r0[:, None]).astype(jnp.int32), axis=1), 0, nb - 1)
    tb_hi = jnp.clip(jnp.sum((st < lim[:, None]).astype(jnp.int32), axis=1) - 1, 0, nb - 1)
    a0 = start // MOE_ALIGN * MOE_ALIGN
    lo_off = start - a0
    hi_off = lo_off + (end - start)
    return (tile_exp, tile_valid, tb_lo, tb_hi, pos_rows, pos_cols,
            a0.reshape(-1), lo_off.reshape(-1), hi_off.reshape(-1))


def _moe_kernel(te_ref, tv_ref, lo_ref, hi_ref, pos_ref, h_hbm, wg_ref, wu_ref, wd_ref, y_ref, xbuf, xs_ref, sem,
                *, n_tiles, tf):
    del te_ref
    i = pl.program_id(0)
    ff = wg_ref.shape[2]

    def blk_copy(tb, slot):
        return pltpu.make_async_copy(h_hbm.at[pl.ds(pl.multiple_of(tb * TB, TB), TB)], xbuf.at[slot], sem.at[slot])

    valid = tv_ref[i] == 1

    @pl.when((i == 0) & valid)
    def _():
        blk_copy(lo_ref[0], 0).start()

    @pl.when(valid)
    def _():
        lo = lo_ref[i]
        nblk = hi_ref[i] - lo + 1
        xs_ref[...] = jnp.zeros_like(xs_ref)
        rid = i * MOE_TM + lax.broadcasted_iota(jnp.int32, (MOE_TM, TB), 0)

        def body(k, c):
            slot = lax.rem(k, 2)
            tb = lo + k
            blk_copy(tb, slot).wait()

            @pl.when(k + 1 < nblk)
            def _():
                blk_copy(tb + 1, 1 - slot).start()

            p = pos_ref[tb]
            oh = (jnp.where(rid == p[0:1], 1.0, 0.0) + jnp.where(rid == p[1:2], 1.0, 0.0)).astype(BF16)
            xs_ref[...] += _dot(oh, xbuf[slot])
            return c

        lax.fori_loop(0, nblk, body, 0)
        x = xs_ref[...].astype(BF16)
        acc = None
        for f in range(ff // tf):
            cols = slice(f * tf, (f + 1) * tf)
            act = _silu(_dot(x, wg_ref[0, :, cols])) * _dot(x, wu_ref[0, :, cols])
            t = _dot(act.astype(BF16), wd_ref[0, cols, :])
            acc = t if acc is None else acc + t
        y_ref[...] = acc.astype(BF16)
        nxt = jnp.minimum(i + 1, n_tiles - 1)

        @pl.when((i + 1 < n_tiles) & (tv_ref[nxt] == 1))
        def _():
            blk_copy(lo_ref[nxt], 0).start()

    @pl.when(jnp.logical_not(valid))
    def _():
        y_ref[...] = jnp.zeros_like(y_ref)


def _moe(h2, plan, wg, wu, wd, n_tiles):
    n, d = h2.shape
    ne, _, ff = wg.shape
    tile_exp, tile_valid, tb_lo, tb_hi, pos_rows = plan[:5]

    def wspec(shape):
        return pl.BlockSpec(shape, lambda i, te, tv, lo, hi: (te[i], 0, 0), pipeline_mode=pl.Buffered(1))

    return pl.pallas_call(
        functools.partial(_moe_kernel, n_tiles=n_tiles, tf=FFN_TF),
        grid_spec=pltpu.PrefetchScalarGridSpec(
            num_scalar_prefetch=4,
            grid=(n_tiles,),
            in_specs=[pl.BlockSpec(pos_rows.shape, lambda i, te, tv, lo, hi: (0, 0, 0)),
                      pl.BlockSpec(memory_space=pl.ANY),
                      wspec((1, d, ff)), wspec((1, d, ff)), wspec((1, ff, d))],
            out_specs=pl.BlockSpec((MOE_TM, d), lambda i, te, tv, lo, hi: (i, 0)),
            scratch_shapes=[pltpu.VMEM((2, TB, d), BF16), pltpu.VMEM((MOE_TM, d), F32),
                            pltpu.SemaphoreType.DMA((2,))]),
        out_shape=jax.ShapeDtypeStruct((n_tiles * MOE_TM, d), BF16),
        compiler_params=_cp("arbitrary"),
        name="moe_ffn",
    )(tile_exp, tile_valid, tb_lo, tb_hi, pos_rows, h2, wg, wu, wd)


def _combine_kernel(a0_ref, lo_ref, hi_ref, s_ref, pos_ref, gates_ref, mod_ref, g_ref, ys_hbm, o_ref, buf, sem,
                    *, nj, nsteps):
    step = pl.program_id(0) * nj + pl.program_id(1)
    slot = lax.rem(step, 2)
    ne = N_EXPERTS

    def win_copy(t, e, sl):
        a0 = pl.multiple_of(a0_ref[t * ne + e], MOE_ALIGN)
        return pltpu.make_async_copy(ys_hbm.at[pl.ds(a0, MOE_WIN)], buf.at[sl, e], sem.at[sl, e])

    @pl.when(step == 0)
    def _():
        for e in range(ne):
            win_copy(0, e, 0).start()

    @pl.when(step + 1 < nsteps)
    def _():
        for e in range(ne):
            win_copy(step + 1, e, 1 - slot).start()

    pos = pos_ref[...]
    gates = gates_ref[0]
    lane = lax.broadcasted_iota(jnp.int32, (TB, MOE_WIN), 1)
    acc = jnp.zeros(s_ref.shape[1:], F32)
    for e in range(ne):
        win_copy(step, e, slot).wait()
        base, lo, hi = a0_ref[step * ne + e], lo_ref[step * ne + e], hi_ref[step * ne + e]
        rel = pos - base
        rel = jnp.where((rel >= lo) & (rel < hi), rel, -1)
        oh = (jnp.where(lane == rel[:, 0:1], 1.0, 0.0) + jnp.where(lane == rel[:, 1:2], 1.0, 0.0)).astype(BF16)
        acc = acc + gates[:, e:e + 1] * _dot(oh, buf[slot, e])
    o_ref[0] = s_ref[0] + mod_ref[0, 0][5:6] * _rms(acc, g_ref[...])


def _moe_combine(s, ys, plan, gates, modtab, g, nlat_t):
    b, l, d = s.shape
    nj = l // TB
    pos_cols, a0, lo_off, hi_off = plan[5:]
    row = pl.BlockSpec((1, TB, d), lambda i, j, *_: (i, j, 0))
    return pl.pallas_call(
        functools.partial(_combine_kernel, nj=nj, nsteps=b * nj),
        grid_spec=pltpu.PrefetchScalarGridSpec(
            num_scalar_prefetch=3,
            grid=(b, nj),
            in_specs=[row,
                      pl.BlockSpec((TB, 2), lambda i, j, *_: (i * nj + j, 0)),
                      pl.BlockSpec((1, TB, LANES), lambda i, j, *_: (i, j, 0)),
                      pl.BlockSpec((1, 1, 6, d), lambda i, j, *_: (i, jnp.where(j < nlat_t, 1, 0), 0, 0)),
                      pl.BlockSpec(g.shape, lambda i, j, *_: (0, 0)),
                      pl.BlockSpec(memory_space=pl.ANY)],
            out_specs=row,
            scratch_shapes=[pltpu.VMEM((2, N_EXPERTS, MOE_WIN, d), BF16),
                            pltpu.SemaphoreType.DMA((2, N_EXPERTS))]),
        out_shape=jax.ShapeDtypeStruct((b, l, d), F32),
        compiler_params=_cp("arbitrary", "arbitrary"),
        name="moe_combine",
    )(a0, lo_off, hi_off, s, pos_cols, gates, modtab, g, ys)


def _resid_kernel(s_ref, y_ref, mod_ref, g_ref, o_ref):
    o_ref[0] = s_ref[0] + mod_ref[0, 0][5:6] * _rms(y_ref[0], g_ref[...])


def _resid(s, y, modtab, g, nlat_t):
    b, l, d = s.shape
    row = pl.BlockSpec((1, TB, d), lambda i, j: (i, j, 0))
    return pl.pallas_call(
        _resid_kernel,
        grid=(b, l // TB),
        in_specs=[row, row, _mod_spec(d, nlat_t), _whole2(g)],
        out_specs=row,
        out_shape=jax.ShapeDtypeStruct((b, l, d), F32),
        compiler_params=_cp("arbitrary", "arbitrary"),
        name="ffn_residual",
    )(s, y, modtab, g)


def _rope_tables(seq, dim):
    t = jnp.arange(seq)
    rows = (t // GRID_W).astype(F32)
    cols = (t % GRID_W).astype(F32)
    quarter = dim // 4
    inv = ROPE_BASE ** (-jnp.arange(quarter, dtype=F32) / quarter)
    ang = jnp.concatenate([rows[:, None] * inv, cols[:, None] * inv], axis=-1)
    c, s = jnp.cos(ang), jnp.sin(ang)
    rep = LANES // dim
    return jnp.tile(jnp.concatenate([c, c], -1), (1, rep)), jnp.tile(jnp.concatenate([-s, s], -1), (1, rep))


def _block_diag(x):
    g, r, c = x.shape[-3:]
    y = jnp.einsum('...grc,gh->...grhc', x, jnp.eye(g, dtype=x.dtype))
    return y.reshape(x.shape[:-3] + (g * r, g * c))


def _s5_params(lam_re, lam_im, log_dt, b_re, b_im, c_re, c_im, nb):
    lam = lax.complex(lam_re, lam_im)
    a_bar = jnp.exp(lam * jnp.exp(log_dt)[..., None])
    b_bar = ((a_bar - 1.0) / lam)[..., None] * lax.complex(b_re, b_im)
    bt = jnp.swapaxes(b_bar, -1, -2)
    bmat = jnp.concatenate([_block_diag(jnp.real(bt)), _block_diag(jnp.imag(bt))], axis=-1).astype(BF16)
    cmat = jnp.concatenate([_block_diag(jnp.swapaxes(c_re, -1, -2)),
                            -_block_diag(jnp.swapaxes(c_im, -1, -2))], axis=-2).astype(BF16)
    ns = a_bar.shape[1] * a_bar.shape[2]
    a_re = jnp.broadcast_to(jnp.real(a_bar).reshape(2, 1, ns), (2, nb, ns))
    a_im = jnp.broadcast_to(jnp.imag(a_bar).reshape(2, 1, ns), (2, nb, ns))
    return bmat, a_re, a_im, cmat


def kernel(x, c, ctx, c_ctx, mod_w, mod_b, norm_g, ev_w_in, ev_conv_w, ev_conv_b, lru_gate_w, lru_gate_b, lru_lam, ret_theta, ret_gn, ev_w_out, ffd_w_gate, ffd_w_up, ffd_w_down, od_w_in, s5_lam_re, s5_lam_im, s5_log_dt, s5_b_re, s5_b_im, s5_c_re, s5_c_im, s5_d, s5_glu_w, s5_glu_b, diff_lam, diff_gn, od_w_out, moe_router_w, moe_router_b, moe_w_gate, moe_w_up, moe_w_down):
    b, seq, d = x.shape
    nctx = ctx.shape[1]
    l = seq + nctx
    n = b * l
    depth = mod_w.shape[0]
    nlat_t = seq // TB
    assert b == SUBLANES and seq % TB == 0 and nctx % TB == 0 and seq % nctx == 0 and n % FFN_TM == 0

    s = jnp.concatenate([x, ctx], axis=1)
    cc = jnp.concatenate([c, c_ctx[None], jnp.zeros((2 * SUBLANES - b - 1, d), F32)], axis=0)
    mt = _modtab(cc, mod_w, mod_b).reshape(depth, 2 * SUBLANES, 6, d)
    modtab = jnp.stack([jnp.broadcast_to(mt[:, b:b + 1], (depth, b, 6, d)), mt[:, :b]], axis=2)

    cos_r, sin_r = _rope_tables(seq, RET_D)
    cos_d, sin_d = _rope_tables(seq, DIFF_DH)

    for layer in range(depth):
        i = layer // 2
        mtab = modtab[layer]
        ng = norm_g[layer]
        if layer % 2 == 0:
            lw = ev_conv_w.shape[-1]
            u2d, zb = _inproj(functools.partial(_inproj_even_kernel, nlat=nlat_t, kscale=RET_D ** -0.5),
                              s, mtab, ng[0:1], ev_w_in[i].astype(BF16), cos_r, sin_r, lw, 5 * lw, nlat_t,
                              "inproj_even")
            gw = _block_diag(lru_gate_w[i]).astype(BF16)
            lf, lb = _lru(u2d.reshape(l * b, lw), ev_conv_w[i].reshape(CONV_W, 1, lw), ev_conv_b[i].reshape(1, lw),
                          gw, lru_gate_b[i].reshape(2, 2, 1, lw), lru_lam[i].reshape(2, 1, lw), b, seq, nctx)
            rf, rb = _retention(jax.nn.log_sigmoid(ret_theta[i].astype(F32)), zb, nlat_t)
            s, h2 = _outproj_even(lf.reshape(l, b * lw), lb.reshape(l, b * lw), zb, rf, rb, ret_gn[i].reshape(1, -1),
                                  ev_w_out[i].astype(BF16), s, mtab, ng, nlat_t)
            y = _ffn(h2.reshape(n, d), ffd_w_gate[i].astype(BF16), ffd_w_up[i].astype(BF16),
                     ffd_w_down[i].astype(BF16))
            s = _resid(s, y.reshape(b, l, d), mtab, ng[3:4], nlat_t)
        else:
            sw = s5_d.shape[-1]
            aw = DIFF_HEADS * 2 * DIFF_DH
            lam_init = 0.8 - 0.6 * math.exp(-0.3 * layer)
            u2d, zb = _inproj(functools.partial(_inproj_odd_kernel, nlat=nlat_t, qscale=DIFF_DH ** -0.5 * math.log2(math.e), s5w=sw),
                              s, mtab, ng[0:1], od_w_in[i].astype(BF16), cos_d, sin_d, sw, 3 * aw, nlat_t,
                              "inproj_odd")
            bmat, a_re, a_im, cmat = _s5_params(s5_lam_re[i], s5_lam_im[i], s5_log_dt[i], s5_b_re[i], s5_b_im[i],
                                                s5_c_re[i], s5_c_im[i], b)
            yf, yb = _s5(u2d.reshape(l * b, sw), bmat, a_re, a_im, cmat, b, seq, nctx)
            lp = diff_lam[i].astype(F32)
            lam = (jnp.exp(jnp.sum(lp[0] * lp[1])) - jnp.exp(jnp.sum(lp[2] * lp[3])) + lam_init).reshape(1)
            att = _attention(lam, zb, diff_gn[i], seq, 1.0 - lam_init)
            rw = jnp.zeros((d, LANES), F32).at[:, :N_EXPERTS].set(moe_router_w[i])
            rbias = jnp.full((1, LANES), NEG, F32).at[0, :N_EXPERTS].set(moe_router_b[i])
            s, h2, gates = _outproj_odd(yf.reshape(l, b * sw), yb.reshape(l, b * sw), u2d, att,
                                        s5_d[i].reshape(1, sw), s5_glu_w[i].astype(BF16), s5_glu_b[i].reshape(1, sw),
                                        od_w_out[i].astype(BF16), s, mtab, ng, rw, rbias, nlat_t)
            n_tiles = (2 * n) // MOE_TM + N_EXPERTS + 2
            plan = _moe_plan(gates.reshape(n, LANES), n_tiles)
            ys = _moe(h2.reshape(n, d), plan, moe_w_gate[i].astype(BF16), moe_w_up[i].astype(BF16),
                      moe_w_down[i].astype(BF16), n_tiles)
            s = _moe_combine(s, ys, plan, gates, mtab, ng[3:4], nlat_t)
    return s[:, :seq]
```

```python
import functools
import math

import jax
import jax.numpy as jnp
from jax import lax
from jax.experimental import pallas as pl
from jax.experimental.pallas import tpu as pltpu

F32 = jnp.float32
BF16 = jnp.bfloat16

EPS = 1e-6
ROPE_BASE = 10000.0
GRID_W = 64
LRU_C = 8.0
LRU_BLOCKS = 8
CONV_W = 4
CONV_LEFT = 2
RET_HEADS = 4
RET_D = 128
S5_GROUP = 16
S5_STATE = 64
DIFF_HEADS = 6
DIFF_DH = 64
N_EXPERTS = 8
LANES = 128
SUBLANES = 8

TB = 256
LRU_TT = 64
S5_TT = 32
ATTN_TQ = 512
ATTN_RB = 128
FFN_TM = 768
FFN_TF = 512
MOE_TM = 256
MOE_ALIGN = 16
MOE_SUB = 128
MOE_SUBS_PER_WIN = (TB + MOE_ALIGN - 1 + MOE_SUB - 1) // MOE_SUB
MOE_SLOTS = (N_EXPERTS * (MOE_ALIGN - 1 + MOE_SUB - 1) + 2 * TB) // MOE_SUB
MOE_NBUF = 6
VMEM_LIMIT = 56 * 1024 * 1024
NEG = -1e30


def _cp(*sem):
    return pltpu.CompilerParams(dimension_semantics=sem, vmem_limit_bytes=VMEM_LIMIT)


def _dot(a, b):
    return jnp.dot(a, b, preferred_element_type=F32)


def _split(x):
    hi = x.astype(BF16)
    lo = (x - hi.astype(F32)).astype(BF16)
    return hi, lo


def _dot3(a, w):
    ah, al = _split(a)
    wh, wl = _split(w)
    return _dot(ah, wh) + (_dot(ah, wl) + _dot(al, wh))


def _rms(x, g):
    return x * lax.rsqrt(jnp.mean(x * x, axis=-1, keepdims=True) + EPS) * g


def _sigmoid(x):
    return 1.0 / (1.0 + jnp.exp(-x))


def _silu(x):
    return x * _sigmoid(x)


def _gelu(x):
    return 0.5 * x * (1.0 + jnp.tanh(math.sqrt(2.0 / math.pi) * (x + 0.044715 * (x * x * x))))


def _softplus(x):
    return jnp.maximum(x, 0.0) + jnp.log(1.0 + jnp.exp(-jnp.abs(x)))


def _modtab_kernel(c_ref, w_ref, b_ref, o_ref):
    c = c_ref[...]
    o_ref[0] = _dot3(_silu(c), w_ref[0]) + b_ref[0]


def _modtab(cc, mod_w, mod_b):
    depth, d, n = mod_w.shape
    tn = 1536
    return pl.pallas_call(
        _modtab_kernel,
        grid=(depth, n // tn),
        in_specs=[pl.BlockSpec((cc.shape[0], d), lambda l, j: (0, 0)),
                  pl.BlockSpec((1, d, tn), lambda l, j: (l, 0, j)),
                  pl.BlockSpec((1, 1, tn), lambda l, j: (l, 0, j))],
        out_specs=pl.BlockSpec((1, cc.shape[0], tn), lambda l, j: (l, 0, j)),
        out_shape=jax.ShapeDtypeStruct((depth, cc.shape[0], n), F32),
        compiler_params=_cp("arbitrary", "arbitrary"),
        name="modtab",
    )(cc, mod_w, mod_b.reshape(depth, 1, n))


def _norm_mod_in(s_ref, mod_ref, g_ref):
    mod = mod_ref[0, 0]
    h = _rms(s_ref[0], g_ref[...]) * (1.0 + mod[1:2]) + mod[0:1]
    return h.astype(BF16)


def _inproj_even_kernel(s_ref, mod_ref, g_ref, w_ref, cos_ref, sin_ref, u_ref, zb_ref, *, nlat, kscale):
    j = pl.program_id(0)
    hb = _norm_mod_in(s_ref, mod_ref, g_ref)
    lw = RET_HEADS * RET_D

    def proj(i):
        return _dot(hb, w_ref[:, i * lw:(i + 1) * lw])

    u_ref[:, pl.ds(pl.program_id(1), 1), :] = proj(0)[:, None, :]
    zb_ref[0, :, 0:lw] = proj(1).astype(BF16)
    lat = j < nlat
    cos = jnp.where(lat, cos_ref[...], 1.0)
    sin = jnp.where(lat, sin_ref[...], 0.0)
    for sec, scale in ((0, 1.0), (1, kscale)):
        t = proj(2 + sec)
        for hd in range(RET_HEADS):
            th = t[:, hd * RET_D:(hd + 1) * RET_D]
            th = (th * cos + pltpu.roll(th, RET_D // 2, 1) * sin) * scale
            zb_ref[0, :, (1 + sec) * lw + hd * RET_D:(1 + sec) * lw + (hd + 1) * RET_D] = th.astype(BF16)
    zb_ref[0, :, 3 * lw:4 * lw] = proj(4).astype(BF16)
    zb_ref[0, :, 4 * lw:5 * lw] = proj(5).astype(BF16)


def _inproj_odd_kernel(s_ref, mod_ref, g_ref, w_ref, cos_ref, sin_ref, u_ref, zb_ref, *, nlat, qscale, s5w):
    j = pl.program_id(0)
    hb = _norm_mod_in(s_ref, mod_ref, g_ref)
    aw = DIFF_HEADS * 2 * DIFF_DH
    u_ref[:, pl.ds(pl.program_id(1), 1), :] = _dot(hb, w_ref[:, 0:s5w])[:, None, :]
    lat = j < nlat
    cos = jnp.where(lat, cos_ref[...], 1.0)
    sin = jnp.where(lat, sin_ref[...], 0.0)
    lane = lax.broadcasted_iota(jnp.int32, (TB, LANES), 1)
    first_half = (lane % DIFF_DH) < (DIFF_DH // 2)
    for sec, scale in ((0, qscale), (1, 1.0)):
        t = _dot(hb, w_ref[:, s5w + sec * aw:s5w + (sec + 1) * aw])
        for hd in range(DIFF_HEADS):
            th = t[:, hd * LANES:(hd + 1) * LANES]
            partner = jnp.where(first_half, pltpu.roll(th, LANES - DIFF_DH // 2, 1), pltpu.roll(th, DIFF_DH // 2, 1))
            th = (th * cos + partner * sin) * scale
            zb_ref[0, :, sec * aw + hd * LANES:sec * aw + (hd + 1) * LANES] = th.astype(BF16)
    zb_ref[0, :, 2 * aw:3 * aw] = _dot(hb, w_ref[:, s5w + 2 * aw:s5w + 3 * aw]).astype(BF16)


def _inproj(body, s, modtab, g, w, cos, sin, uw, zw, nlat, name):
    b, l, d = s.shape
    nj = l // TB
    return pl.pallas_call(
        body,
        grid=(nj, b),
        in_specs=[pl.BlockSpec((1, TB, d), lambda j, i: (i, j, 0)),
                  pl.BlockSpec((1, 1, 6, d), lambda j, i: (i, jnp.where(j < nlat, 1, 0), 0, 0)),
                  pl.BlockSpec((1, d), lambda j, i: (0, 0)),
                  pl.BlockSpec(w.shape, lambda j, i: (0, 0)),
                  pl.BlockSpec((TB, LANES), lambda j, i: (jnp.minimum(j, nlat - 1), 0)),
                  pl.BlockSpec((TB, LANES), lambda j, i: (jnp.minimum(j, nlat - 1), 0))],
        out_specs=[pl.BlockSpec((TB, b, uw), lambda j, i: (j, 0, 0)),
                   pl.BlockSpec((1, TB, zw), lambda j, i: (i, j, 0))],
        out_shape=[jax.ShapeDtypeStruct((l, b, uw), F32),
                   jax.ShapeDtypeStruct((b, l, zw), BF16)],
        compiler_params=_cp("arbitrary", "arbitrary"),
        name=name,
    )(s, modtab, g, w, cos, sin)


def _fwd_chunk(s, nlat_c, nctx_c):
    return jnp.where(s < nctx_c, nlat_c + s, s - nctx_c)


def _bwd_chunk(s, nlat_c, nctx_c):
    return nlat_c + nctx_c - 1 - s


def _lru_kernel(uf_ref, ufl_ref, ufr_ref, ub_ref, ubl_ref, ubr_ref, cw_ref, cb_ref, gw_ref, gb_ref, lam_ref,
                of_ref, ob_ref, pad_scr, a_scr, b_scr, h_scr, *, tt, nlat_c, nctx_c, nb):
    s = pl.program_id(0)
    nc = nlat_c + nctx_c

    @pl.when(s == 0)
    def _():
        h_scr[...] = jnp.zeros_like(h_scr)

    rows = tt * nb
    views = ((uf_ref, ufl_ref, ufr_ref, _fwd_chunk(s, nlat_c, nctx_c)),
             (ub_ref, ubl_ref, ubr_ref, _bwd_chunk(s, nlat_c, nctx_c)))
    for d, (u_ref, l_ref, r_ref, c) in enumerate(views):
        first = (c == 0) | (c == nlat_c)
        last = (c == nlat_c - 1) | (c == nc - 1)
        pad_scr[0:CONV_LEFT * nb] = jnp.where(first, 0.0, l_ref[...])
        pad_scr[CONV_LEFT * nb:CONV_LEFT * nb + rows] = u_ref[...]
        pad_scr[CONV_LEFT * nb + rows:(CONV_W - 1) * nb + rows] = jnp.where(last, 0.0, r_ref[...])
        uc = cb_ref[...] + pad_scr[0:rows] * cw_ref[0]
        for k in range(1, CONV_W):
            uc = uc + pad_scr[k * nb:k * nb + rows] * cw_ref[k]
        ucb = uc.astype(BF16)
        r = _sigmoid(_dot(ucb, gw_ref[d, 0]) + gb_ref[d, 0])
        i = _sigmoid(_dot(ucb, gw_ref[d, 1]) + gb_ref[d, 1])
        a = jnp.exp((-LRU_C * _softplus(-lam_ref[d])) * r)
        a_scr[d] = a
        b_scr[d] = jnp.sqrt(1.0 - a * a) * (i * uc)

    def body(t, carry):
        hf, hb = carry
        rf = pl.multiple_of(t * nb, nb)
        rb = pl.multiple_of((tt - 1 - t) * nb, nb)
        hf = a_scr[0, pl.ds(rf, nb), :] * hf + b_scr[0, pl.ds(rf, nb), :]
        of_ref[pl.ds(rf, nb), :] = hf
        hb = a_scr[1, pl.ds(rb, nb), :] * hb + b_scr[1, pl.ds(rb, nb), :]
        ob_ref[pl.ds(rb, nb), :] = hb
        return hf, hb

    hf, hb = lax.fori_loop(0, tt, body, (h_scr[0], h_scr[1]), unroll=8)
    h_scr[0] = hf
    h_scr[1] = hb


def _lru(u_tm, conv_w, conv_b, gate_w, gate_b, lam, nb, nlat, nctx):
    rows_total, c = u_tm.shape
    tt = LRU_TT
    nlat_c, nctx_c = nlat // tt, nctx // tt
    nc = nlat_c + nctx_c
    rows = tt * nb
    fwd = functools.partial(_fwd_chunk, nlat_c=nlat_c, nctx_c=nctx_c)
    bwd = functools.partial(_bwd_chunk, nlat_c=nlat_c, nctx_c=nctx_c)
    lrows, rrows = CONV_LEFT * nb, (CONV_W - 1 - CONV_LEFT) * nb
    nl, nr = rows // lrows, rows // rrows

    def cur(f):
        return pl.BlockSpec((rows, c), lambda s: (f(s), 0))

    def left(f):
        return pl.BlockSpec((lrows, c), lambda s: (jnp.maximum(f(s) * nl - 1, 0), 0))

    def right(f):
        return pl.BlockSpec((rrows, c), lambda s: (jnp.minimum((f(s) + 1) * nr, nc * nr - 1), 0))

    def whole(a):
        return pl.BlockSpec(a.shape, lambda s: (0,) * a.ndim)

    return pl.pallas_call(
        functools.partial(_lru_kernel, tt=tt, nlat_c=nlat_c, nctx_c=nctx_c, nb=nb),
        grid=(nc,),
        in_specs=[cur(fwd), left(fwd), right(fwd), cur(bwd), left(bwd), right(bwd),
                  whole(conv_w), whole(conv_b), whole(gate_w), whole(gate_b), whole(lam)],
        out_specs=[cur(fwd), cur(bwd)],
        out_shape=[jax.ShapeDtypeStruct((rows_total, c), F32)] * 2,
        scratch_shapes=[pltpu.VMEM((rows + (CONV_W - 1) * nb, c), F32),
                        pltpu.VMEM((2, rows, c), F32),
                        pltpu.VMEM((2, rows, c), F32),
                        pltpu.VMEM((2, nb, c), F32)],
        compiler_params=_cp("arbitrary"),
        name="rglru",
    )(u_tm, u_tm, u_tm, u_tm, u_tm, u_tm, conv_w, conv_b, gate_w, gate_b, lam)


def _ret_kernel(lg_ref, qf_ref, kf_ref, vf_ref, qb_ref, kb_ref, vb_ref, of_ref, ob_ref, s_scr, d_scr, *, c):
    s = pl.program_id(1)
    ii = lax.broadcasted_iota(jnp.int32, (c, c), 0).astype(F32)
    jj = lax.broadcasted_iota(jnp.int32, (c, c), 1).astype(F32)

    @pl.when(s == 0)
    def _():
        s_scr[...] = jnp.zeros_like(s_scr)
        for d in range(2):
            diff = (ii - jj) if d == 0 else (jj - ii)
            for h in range(RET_HEADS):
                d_scr[d, h] = jnp.where(diff >= 0, jnp.exp(jnp.maximum(diff, 0.0) * lg_ref[d, h]), 0.0)

    ri = lax.broadcasted_iota(jnp.int32, (c, 1), 0).astype(F32)
    views = ((qf_ref, kf_ref, vf_ref, of_ref), (qb_ref, kb_ref, vb_ref, ob_ref))
    for d, (q_ref, k_ref, v_ref, o_ref) in enumerate(views):
        for h in range(RET_HEADS):
            lg = lg_ref[d, h]
            cols = slice(h * RET_D, (h + 1) * RET_D)
            q, k, v = q_ref[0, :, cols], k_ref[0, :, cols], v_ref[0, :, cols]
            sc = lax.dot_general(q, k, (((1,), (1,)), ((), ())), preferred_element_type=F32) * d_scr[d, h]
            inner = _dot(sc.astype(BF16), v)
            st = s_scr[d, h]
            qdec = jnp.exp(((ri + 1.0) if d == 0 else (c - ri)) * lg)
            cross = _dot(q, st.astype(BF16)) * qdec
            o_ref[0, :, cols] = inner + cross
            kdec = jnp.exp(((c - 1.0 - ri) if d == 0 else ri) * lg)
            kd = (k.astype(F32) * kdec).T.astype(BF16)
            gc = jnp.exp(jnp.zeros((RET_D, RET_D), F32) + c * lg)
            s_scr[d, h] = gc * st + _dot(kd, v)


def _retention(log_g, zb, nlat):
    b, l, _ = zb.shape
    nj = l // TB
    w = RET_HEADS * RET_D

    def fwd(s):
        return jnp.where(s < nj - nlat, nlat + s, s - (nj - nlat))

    def bwd(s):
        return nj - 1 - s

    def col(f, i):
        return pl.BlockSpec((1, TB, w), lambda bi, s: (bi, f(s), i))

    def out(f):
        return pl.BlockSpec((1, TB, w), lambda bi, s: (bi, f(s), 0))

    return pl.pallas_call(
        functools.partial(_ret_kernel, c=TB),
        grid=(b, nj),
        in_specs=[pl.BlockSpec(memory_space=pltpu.SMEM),
                  col(fwd, 1), col(fwd, 2), col(fwd, 3), col(bwd, 1), col(bwd, 2), col(bwd, 3)],
        out_specs=[out(fwd), out(bwd)],
        out_shape=[jax.ShapeDtypeStruct((b, l, w), F32)] * 2,
        scratch_shapes=[pltpu.VMEM((2, RET_HEADS, RET_D, RET_D), F32),
                        pltpu.VMEM((2, RET_HEADS, TB, TB), F32)],
        compiler_params=_cp("arbitrary", "arbitrary"),
        name="retention",
    )(log_g, zb, zb, zb, zb, zb, zb)


def _s5_kernel(uf_ref, ub_ref, bm_ref, ar_ref, ai_ref, cm_ref, yf_ref, yb_ref, bu_scr, h_scr, *, tt, nb, ns):
    s = pl.program_id(0)

    @pl.when(s == 0)
    def _():
        h_scr[...] = jnp.zeros_like(h_scr)

    cg = 512
    for d, (u_ref, y_ref) in enumerate(((uf_ref, yf_ref), (ub_ref, yb_ref))):
        bu_scr[...] = _dot(u_ref[...].astype(BF16), bm_ref[d])
        for g in range(ns // cg):
            re = slice(g * cg, (g + 1) * cg)
            im = slice(ns + g * cg, ns + (g + 1) * cg)
            ar = ar_ref[d, :, re]
            ai = ai_ref[d, :, re]

            def body(i, carry, re=re, im=im, ar=ar, ai=ai, d=d):
                hr, hi = carry
                t = i if d == 0 else tt - 1 - i
                r0 = pl.multiple_of(t * nb, nb)
                nr = ar * hr - ai * hi + bu_scr[pl.ds(r0, nb), re]
                ni = ar * hi + ai * hr + bu_scr[pl.ds(r0, nb), im]
                bu_scr[pl.ds(r0, nb), re] = nr
                bu_scr[pl.ds(r0, nb), im] = ni
                return nr, ni

            hr, hi = lax.fori_loop(0, tt, body, (h_scr[d, :, re], h_scr[d, :, im]), unroll=4)
            h_scr[d, :, re] = hr
            h_scr[d, :, im] = hi
        y_ref[...] = _dot(bu_scr[...].astype(BF16), cm_ref[d])


def _s5(u_tm, bmat, a_re, a_im, cmat, nb, nlat, nctx):
    rows_total, c = u_tm.shape
    tt = S5_TT
    nlat_c, nctx_c = nlat // tt, nctx // tt
    rows = tt * nb
    ns = a_re.shape[-1]
    fwd = functools.partial(_fwd_chunk, nlat_c=nlat_c, nctx_c=nctx_c)
    bwd = functools.partial(_bwd_chunk, nlat_c=nlat_c, nctx_c=nctx_c)

    def cur(f):
        return pl.BlockSpec((rows, c), lambda s: (f(s), 0))

    def whole(a):
        return pl.BlockSpec(a.shape, lambda s: (0,) * a.ndim)

    return pl.pallas_call(
        functools.partial(_s5_kernel, tt=tt, nb=nb, ns=ns),
        grid=(nlat_c + nctx_c,),
        in_specs=[cur(fwd), cur(bwd), whole(bmat), whole(a_re), whole(a_im), whole(cmat)],
        out_specs=[cur(fwd), cur(bwd)],
        out_shape=[jax.ShapeDtypeStruct((rows_total, c), F32)] * 2,
        scratch_shapes=[pltpu.VMEM((rows, 2 * ns), F32), pltpu.VMEM((2, nb, 2 * ns), F32)],
        compiler_params=_cp("arbitrary"),
        name="s5",
    )(u_tm, u_tm, bmat, a_re, a_im, cmat)


def _attn_kernel(lam_ref, q_ref, k_ref, v_ref, gn_ref, o_ref, *, out_scale):
    tq, tk = q_ref.shape[1], k_ref.shape[1]
    q = q_ref[0]
    lane = lax.broadcasted_iota(jnp.int32, (tq, LANES), 1)
    zero = jnp.zeros_like(q)
    qm = (jnp.where(lane < DIFF_DH, q, zero), jnp.where(lane >= DIFF_DH, q, zero))
    kc = k_ref[0]
    va = jnp.concatenate([v_ref[0], jnp.ones((tk, LANES), BF16)], axis=1)
    rb = min(ATTN_RB, tq)
    for r in range(tq // rb):
        rows = slice(r * rb, (r + 1) * rb)
        outs = []
        for m in range(2):
            sc = lax.dot_general(qm[m][rows], kc, (((1,), (1,)), ((), ())), preferred_element_type=F32)
            p = jnp.exp2(sc - jnp.max(sc, axis=1, keepdims=True)).astype(BF16)
            a = _dot(p, va)
            outs.append(a[:, :LANES] / a[:, LANES:LANES + 1])
        att = outs[0] - lam_ref[0] * outs[1]
        att = att * lax.rsqrt(jnp.mean(att * att, axis=-1, keepdims=True) + EPS) * (gn_ref[0] * out_scale)
        o_ref[0, rows] = att.astype(BF16)


def _attention(lam, zb, gn, nlat, out_scale):
    b, l, _ = zb.shape
    h = DIFF_HEADS
    nctx = l - nlat
    tq = ATTN_TQ if nlat % ATTN_TQ == 0 else TB
    gn3 = gn.reshape(h, 1, LANES)

    def call(tq_, nq, kv_rows, q0, kv0, name):
        return pl.pallas_call(
            functools.partial(_attn_kernel, out_scale=out_scale),
            grid=(b, h, nq),
            in_specs=[pl.BlockSpec(memory_space=pltpu.SMEM),
                      pl.BlockSpec((1, tq_, LANES), lambda bi, hi, qi: (bi, q0 + qi, hi)),
                      pl.BlockSpec((1, kv_rows, LANES), lambda bi, hi, qi: (bi, kv0, h + hi)),
                      pl.BlockSpec((1, kv_rows, LANES), lambda bi, hi, qi: (bi, kv0, 2 * h + hi)),
                      pl.BlockSpec((1, 1, LANES), lambda bi, hi, qi: (hi, 0, 0))],
            out_specs=pl.BlockSpec((1, tq_, LANES), lambda bi, hi, qi: (bi, qi, hi)),
            out_shape=jax.ShapeDtypeStruct((b, nq * tq_, h * LANES), BF16),
            compiler_params=_cp("arbitrary", "arbitrary", "arbitrary"),
            name=name,
        )(lam, zb, zb, zb, gn3)

    att_x = call(tq, nlat // tq, l, 0, 0, "diffattn_latent")
    att_c = call(nctx, 1, nctx, nlat // nctx, nlat // nctx, "diffattn_context")
    return jnp.concatenate([att_x, att_c], axis=1)


def _out_tail(parts, w_ref, s_ref, mod, ng_ref, xo_ref):
    y, r0 = None, 0
    for p in parts:
        n = p.shape[1]
        t = _dot(p.astype(BF16), w_ref[r0:r0 + n, :])
        y = t if y is None else y + t
        r0 += n
    xn = s_ref[0] + mod[2:3] * _rms(y, ng_ref[1:2])
    xo_ref[0] = xn
    return _rms(xn, ng_ref[2:3]) * (1.0 + mod[4:5]) + mod[3:4]


def _outproj_even_kernel(lf_ref, lb_ref, zg_ref, zo_ref, rf_ref, rb_ref, gn_ref, w_ref, s_ref, mod_ref, ng_ref,
                         xo_ref, h2_ref):
    parts = [(lf_ref[...] + lb_ref[...]) * _gelu(zg_ref[0].astype(F32))]
    r = rf_ref[0] + rb_ref[0]
    og = _silu(zo_ref[0].astype(F32))
    for h in range(RET_HEADS):
        cols = slice(h * RET_D, (h + 1) * RET_D)
        y = r[:, cols]
        y = y - jnp.mean(y, axis=-1, keepdims=True)
        y = y * lax.rsqrt(jnp.mean(y * y, axis=-1, keepdims=True) + EPS) * gn_ref[:, cols]
        parts.append(y * og[:, cols])
    h2 = _out_tail(parts, w_ref, s_ref, mod_ref[0, 0], ng_ref, xo_ref)
    h2_ref[0] = h2.astype(BF16)


def _outproj_odd_kernel(yf_ref, yb_ref, u_ref, att_ref, d_ref, gw_ref, gb_ref, w_ref, s_ref, mod_ref, ng_ref,
                        rw_ref, rb_ref, xo_ref, h2_ref, gates_ref):
    z = _gelu(yf_ref[...] + yb_ref[...] + d_ref[...] * u_ref[...])
    s5o = z * _sigmoid(_dot(z.astype(BF16), gw_ref[...]) + gb_ref[...])
    h2 = _out_tail([s5o, att_ref[0]], w_ref, s_ref, mod_ref[0, 0], ng_ref, xo_ref)
    d = h2.shape[1]
    h2_ref[0, :, :d] = h2.astype(BF16)
    logits = _dot3(h2, rw_ref[...]) + rb_ref[...]
    lane = lax.broadcasted_iota(jnp.int32, logits.shape, 1).astype(F32)
    m1 = jnp.max(logits, axis=1, keepdims=True)
    i1 = jnp.min(jnp.where(logits == m1, lane, float(LANES)), axis=1, keepdims=True)
    rest = jnp.where(lane == i1, 2.0 * NEG, logits)
    m2 = jnp.max(rest, axis=1, keepdims=True)
    i2 = jnp.min(jnp.where(rest == m2, lane, float(LANES)), axis=1, keepdims=True)
    e = jnp.exp(m2 - m1)
    p1 = 1.0 / (1.0 + e)
    gates = jnp.where(lane == i1, p1, 0.0) + jnp.where(lane == i2, e * p1, 0.0)
    ids = jnp.where(lane == float(N_EXPERTS), i1, 0.0) + jnp.where(lane == float(N_EXPERTS + 1), i2, 0.0)
    gates_ref[0] = gates + ids
    ghi = gates.astype(BF16).astype(F32)
    glo = (gates - ghi).astype(BF16).astype(F32)
    h2_ref[0, :, d:] = (ghi + pltpu.roll(glo, N_EXPERTS, 1)).astype(BF16)


def _whole2(a):
    return pl.BlockSpec(a.shape, lambda i, j: (0,) * a.ndim)


def _mod_spec(d, nlat_t):
    return pl.BlockSpec((1, 1, 6, d), lambda i, j: (i, jnp.where(j < nlat_t, 1, 0), 0, 0))


def _outproj_even(lf2d, lb2d, zb, rf, rb, gn, w, s, modtab, ng, nlat_t):
    b, l, d = s.shape
    cw = RET_HEADS * RET_D
    tm = pl.BlockSpec((TB, cw), lambda i, j: (j, i))
    row = pl.BlockSpec((1, TB, d), lambda i, j: (i, j, 0))
    return pl.pallas_call(
        _outproj_even_kernel,
        grid=(b, l // TB),
        in_specs=[tm, tm,
                  pl.BlockSpec((1, TB, cw), lambda i, j: (i, j, 0)),
                  pl.BlockSpec((1, TB, cw), lambda i, j: (i, j, 4)),
                  pl.BlockSpec((1, TB, cw), lambda i, j: (i, j, 0)),
                  pl.BlockSpec((1, TB, cw), lambda i, j: (i, j, 0)),
                  _whole2(gn), _whole2(w), row, _mod_spec(d, nlat_t), _whole2(ng)],
        out_specs=[row, row],
        out_shape=[jax.ShapeDtypeStruct((b, l, d), F32), jax.ShapeDtypeStruct((b, l, d), BF16)],
        compiler_params=_cp("arbitrary", "arbitrary"),
        name="outproj_even",
    )(lf2d, lb2d, zb, zb, rf, rb, gn, w, s, modtab, ng)


def _outproj_odd(yf2d, yb2d, u2d, att, dskip, glu_w, glu_b, w, s, modtab, ng, rw, rb, nlat_t):
    b, l, d = s.shape
    sw = dskip.shape[1]
    aw = att.shape[2]
    tm = pl.BlockSpec((TB, sw), lambda i, j: (j, i))
    row = pl.BlockSpec((1, TB, d), lambda i, j: (i, j, 0))
    return pl.pallas_call(
        _outproj_odd_kernel,
        grid=(b, l // TB),
        in_specs=[tm, tm, tm,
                  pl.BlockSpec((1, TB, aw), lambda i, j: (i, j, 0)),
                  _whole2(dskip), _whole2(glu_w), _whole2(glu_b), _whole2(w), row, _mod_spec(d, nlat_t),
                  _whole2(ng), _whole2(rw), _whole2(rb)],
        out_specs=[row, pl.BlockSpec((1, TB, d + LANES), lambda i, j: (i, j, 0)),
                   pl.BlockSpec((1, TB, LANES), lambda i, j: (i, j, 0))],
        out_shape=[jax.ShapeDtypeStruct((b, l, d), F32), jax.ShapeDtypeStruct((b, l, d + LANES), BF16),
                   jax.ShapeDtypeStruct((b, l, LANES), F32)],
        compiler_params=_cp("arbitrary", "arbitrary"),
        name="outproj_odd",
    )(yf2d, yb2d, u2d, att, dskip, glu_w, glu_b, w, s, modtab, ng, rw, rb)


def _ffn_kernel(x_ref, wg_ref, wu_ref, wd_ref, s_ref, mod_ref, g_ref, o_ref, acc_ref, *, tpb, seq):
    i, f = pl.program_id(0), pl.program_id(1)

    @pl.when(f == 0)
    def _():
        acc_ref[...] = jnp.zeros_like(acc_ref)

    x = x_ref[...]
    act = _silu(_dot(x, wg_ref[...])) * _dot(x, wu_ref[...])
    acc_ref[...] += _dot(act.astype(BF16), wd_ref[...])

    @pl.when(f == pl.num_programs(1) - 1)
    def _():
        tm = acc_ref.shape[0]
        row = lax.rem(i, tpb) * tm + lax.broadcasted_iota(jnp.int32, (tm, 1), 0)
        mod = mod_ref[0]
        gate = jnp.where(row < seq, mod[1, 5:6], mod[0, 5:6])
        o_ref[...] = s_ref[...] + gate * _rms(acc_ref[...], g_ref[...])


def _ffn(x, wg, wu, wd, s, modtab, g, seq):
    b, l, d = s.shape
    n = b * l
    ff = wg.shape[1]
    tm, tf = FFN_TM, FFN_TF
    tpb = l // tm
    rows = pl.BlockSpec((tm, d), lambda i, f: (i, 0))
    out = pl.pallas_call(
        functools.partial(_ffn_kernel, tpb=tpb, seq=seq),
        grid=(n // tm, ff // tf),
        in_specs=[rows,
                  pl.BlockSpec((d, tf), lambda i, f: (0, f)),
                  pl.BlockSpec((d, tf), lambda i, f: (0, f)),
                  pl.BlockSpec((tf, d), lambda i, f: (f, 0)),
                  rows,
                  pl.BlockSpec((1, 2, 6, d), lambda i, f: (i // tpb, 0, 0, 0)),
                  pl.BlockSpec(g.shape, lambda i, f: (0, 0))],
        out_specs=rows,
        out_shape=jax.ShapeDtypeStruct((n, d), F32),
        scratch_shapes=[pltpu.VMEM((tm, d), F32)],
        compiler_params=_cp("arbitrary", "arbitrary"),
        name="dense_ffn",
    )(x, wg, wu, wd, s.reshape(n, d), modtab, g)
    return out.reshape(b, l, d)


def _moe_plan(route, n_tiles):
    n = route.shape[0]
    nb = n // TB
    ne = N_EXPERTS
    e1 = route[:, ne].astype(jnp.int32)
    e2 = route[:, ne + 1].astype(jnp.int32)
    ex = jnp.arange(ne, dtype=jnp.int32)
    sel = ((e1[:, None] == ex) | (e2[:, None] == ex)).astype(jnp.int32)
    incl = jnp.cumsum(sel, axis=0)
    count = incl[-1]
    gsize = (count + MOE_TM - 1) // MOE_TM * MOE_TM
    gend = jnp.cumsum(gsize)
    goff = gend - gsize
    posm = goff[None, :] + incl - sel
    pos1 = jnp.sum(jnp.where(e1[:, None] == ex, posm, 0), axis=1)
    pos2 = jnp.sum(jnp.where(e2[:, None] == ex, posm, 0), axis=1)
    pos_cols = jnp.stack([pos1, pos2], axis=1)
    pos_rows = jnp.stack([pos1.reshape(nb, TB), pos2.reshape(nb, TB)], axis=1)
    start = posm[::TB]
    last = (goff + count)[None]
    end = jnp.concatenate([start[1:], last], axis=0)
    r0 = jnp.arange(n_tiles, dtype=jnp.int32) * MOE_TM
    tile_valid = (r0 < gend[-1]).astype(jnp.int32)
    tile_exp = jnp.minimum(jnp.sum((r0[:, None] >= gend[None, :]).astype(jnp.int32), axis=1), ne - 1)
    st, en = start.T[tile_exp], end.T[tile_exp]
    lim = jnp.minimum(r0 + MOE_TM, last[0][tile_exp])
    tb_lo = jnp.clip(jnp.sum((en <= r0[:, None]).astype(jnp.int32), axis=1), 0, nb - 1)
    tb_hi = jnp.clip(jnp.sum((st < lim[:, None]).astype(jnp.int32), axis=1) - 1, 0, nb - 1)
    a0 = start // MOE_ALIGN * MOE_ALIGN
    lo_off = start - a0
    hi_off = lo_off + (end - start)
    k = jnp.arange(MOE_SUBS_PER_WIN, dtype=jnp.int32) * MOE_SUB
    c_valid = ((end > start)[..., None] & (hi_off[..., None] > k)).reshape(nb, -1)
    c_base = (a0[..., None] + k).reshape(nb, -1)
    c_lo = jnp.clip(lo_off[..., None] - k, 0, MOE_SUB).reshape(nb, -1)
    c_hi = jnp.clip(hi_off[..., None] - k, 0, MOE_SUB).reshape(nb, -1)
    slot = jnp.cumsum(c_valid.astype(jnp.int32), axis=1) - 1
    put = (slot[..., None] == jnp.arange(MOE_SLOTS, dtype=jnp.int32)) & c_valid[..., None]

    def place(v):
        return jnp.sum(jnp.where(put, v[..., None], 0), axis=1).reshape(-1)

    return (tile_exp, tile_valid, tb_lo, tb_hi, pos_rows, pos_cols, place(c_base), place(c_lo), place(c_hi))


def _moe_kernel(te_ref, tv_ref, lo_ref, hi_ref, pos_ref, h_hbm, wg_ref, wu_ref, wd_ref, y_ref, xbuf, xs_ref, sem,
                *, n_tiles, tf):
    i = pl.program_id(0)
    d, ff = wg_ref.shape[1], wg_ref.shape[2]

    def blk_copy(tb, slot):
        return pltpu.make_async_copy(h_hbm.at[pl.ds(pl.multiple_of(tb * TB, TB), TB)], xbuf.at[slot], sem.at[slot])

    def start_head(t):
        lo = lo_ref[t]
        cnt = hi_ref[t] - lo + 1
        for k in range(MOE_NBUF):
            @pl.when(k < cnt)
            def _(k=k):
                blk_copy(lo + k, k).start()

    valid = tv_ref[i] == 1

    @pl.when((i == 0) & valid)
    def _():
        start_head(0)

    @pl.when(valid)
    def _():
        lo = lo_ref[i]
        nblk = hi_ref[i] - lo + 1
        xs_ref[...] = jnp.zeros_like(xs_ref)
        rid = i * MOE_TM + lax.broadcasted_iota(jnp.int32, (MOE_TM, TB), 0)

        def body(k, c):
            slot = lax.rem(k, MOE_NBUF)
            tb = lo + k
            blk_copy(tb, slot).wait()
            p = pos_ref[tb]
            oh = (jnp.where(rid == p[0:1], 1.0, 0.0) + jnp.where(rid == p[1:2], 1.0, 0.0)).astype(BF16)
            xs_ref[...] += _dot(oh, xbuf[slot])

            @pl.when(k + MOE_NBUF < nblk)
            def _():
                blk_copy(tb + MOE_NBUF, slot).start()

            return c

        lax.fori_loop(0, nblk, body, 0)
        nxt = jnp.minimum(i + 1, n_tiles - 1)

        @pl.when((i + 1 < n_tiles) & (tv_ref[nxt] == 1))
        def _():
            start_head(nxt)

        xs = xs_ref[...]
        x = xs[:, :d].astype(BF16)
        ext = xs[:, d:]
        lane = lax.broadcasted_iota(jnp.int32, ext.shape, 1)
        e_t = te_ref[i]
        gate = jnp.sum(jnp.where((lane == e_t) | (lane == e_t + N_EXPERTS), ext, 0.0), axis=1, keepdims=True)
        acc = None
        for f in range(ff // tf):
            cols = slice(f * tf, (f + 1) * tf)
            act = _silu(_dot(x, wg_ref[0, :, cols])) * _dot(x, wu_ref[0, :, cols])
            t = _dot(act.astype(BF16), wd_ref[0, cols, :])
            acc = t if acc is None else acc + t
        y_ref[...] = (acc * gate).astype(BF16)

    @pl.when(jnp.logical_not(valid))
    def _():
        y_ref[...] = jnp.zeros_like(y_ref)


def _moe(h2x, plan, wg, wu, wd, n_tiles):
    n, dx = h2x.shape
    ne, d, ff = wg.shape
    tile_exp, tile_valid, tb_lo, tb_hi, pos_rows = plan[:5]

    def wspec(shape):
        return pl.BlockSpec(shape, lambda i, te, tv, lo, hi: (te[i], 0, 0), pipeline_mode=pl.Buffered(1))

    return pl.pallas_call(
        functools.partial(_moe_kernel, n_tiles=n_tiles, tf=FFN_TF),
        grid_spec=pltpu.PrefetchScalarGridSpec(
            num_scalar_prefetch=4,
            grid=(n_tiles,),
            in_specs=[pl.BlockSpec(pos_rows.shape, lambda i, te, tv, lo, hi: (0, 0, 0)),
                      pl.BlockSpec(memory_space=pl.ANY),
                      wspec((1, d, ff)), wspec((1, d, ff)), wspec((1, ff, d))],
            out_specs=pl.BlockSpec((MOE_TM, d), lambda i, te, tv, lo, hi: (i, 0)),
            scratch_shapes=[pltpu.VMEM((MOE_NBUF, TB, dx), BF16), pltpu.VMEM((MOE_TM, dx), F32),
                            pltpu.SemaphoreType.DMA((MOE_NBUF,))]),
        out_shape=jax.ShapeDtypeStruct((n_tiles * MOE_TM, d), BF16),
        compiler_params=_cp("arbitrary"),
        name="moe_ffn",
    )(tile_exp, tile_valid, tb_lo, tb_hi, pos_rows, h2x, wg, wu, wd)


def _combine_kernel(base_ref, lo_ref, hi_ref, s_ref, pos_ref, mod_ref, g_ref, ys_hbm, o_ref, buf, sem, *, nj, nsteps):
    step = pl.program_id(0) * nj + pl.program_id(1)
    slot = lax.rem(step, 2)

    def win_copy(t, k, sl):
        a = pl.multiple_of(base_ref[t * MOE_SLOTS + k], MOE_ALIGN)
        return pltpu.make_async_copy(ys_hbm.at[pl.ds(a, MOE_SUB)], buf.at[sl, pl.ds(k * MOE_SUB, MOE_SUB)],
                                     sem.at[sl, k])

    @pl.when(step == 0)
    def _():
        for k in range(MOE_SLOTS):
            win_copy(0, k, 0).start()

    @pl.when(step + 1 < nsteps)
    def _():
        for k in range(MOE_SLOTS):
            win_copy(step + 1, k, 1 - slot).start()

    pos = pos_ref[...]
    pos1 = jnp.broadcast_to(pos[:, 0:1], (TB, MOE_SUB))
    pos2 = jnp.broadcast_to(pos[:, 1:2], (TB, MOE_SUB))
    lane = lax.broadcasted_iota(jnp.int32, (1, MOE_SUB), 1)
    acc = jnp.zeros(s_ref.shape[1:], F32)
    per_dot = 2 * LANES // MOE_SUB
    for g in range(MOE_SLOTS // per_dot):
        pieces = []
        for k in range(g * per_dot, (g + 1) * per_dot):
            win_copy(step, k, slot).wait()
            j = step * MOE_SLOTS + k
            held = jnp.where((lane >= lo_ref[j]) & (lane < hi_ref[j]), lane + base_ref[j], -1)
            pieces.append(jnp.where(pos1 == held, 1.0, 0.0) + jnp.where(pos2 == held, 1.0, 0.0))
        oh = jnp.concatenate(pieces, axis=1).astype(BF16)
        rows = slice(g * per_dot * MOE_SUB, (g + 1) * per_dot * MOE_SUB)
        acc = acc + _dot(oh, buf[slot, rows])
    o_ref[0] = s_ref[0] + mod_ref[0, 0][5:6] * _rms(acc, g_ref[...])


def _moe_combine(s, ys, plan, modtab, g, nlat_t):
    b, l, d = s.shape
    nj = l // TB
    pos_cols, base, lo_off, hi_off = plan[5:]
    row = pl.BlockSpec((1, TB, d), lambda i, j, *_: (i, j, 0))
    return pl.pallas_call(
        functools.partial(_combine_kernel, nj=nj, nsteps=b * nj),
        grid_spec=pltpu.PrefetchScalarGridSpec(
            num_scalar_prefetch=3,
            grid=(b, nj),
            in_specs=[row,
                      pl.BlockSpec((TB, 2), lambda i, j, *_: (i * nj + j, 0)),
                      pl.BlockSpec((1, 1, 6, d), lambda i, j, *_: (i, jnp.where(j < nlat_t, 1, 0), 0, 0)),
                      pl.BlockSpec(g.shape, lambda i, j, *_: (0, 0)),
                      pl.BlockSpec(memory_space=pl.ANY)],
            out_specs=row,
            scratch_shapes=[pltpu.VMEM((2, MOE_SLOTS * MOE_SUB, d), BF16),
                            pltpu.SemaphoreType.DMA((2, MOE_SLOTS))]),
        out_shape=jax.ShapeDtypeStruct((b, l, d), F32),
        compiler_params=_cp("arbitrary", "arbitrary"),
        name="moe_combine",
    )(base, lo_off, hi_off, s, pos_cols, modtab, g, ys)


def _rope_tables(seq, dim):
    t = jnp.arange(seq)
    rows = (t // GRID_W).astype(F32)
    cols = (t % GRID_W).astype(F32)
    quarter = dim // 4
    inv = ROPE_BASE ** (-jnp.arange(quarter, dtype=F32) / quarter)
    ang = jnp.concatenate([rows[:, None] * inv, cols[:, None] * inv], axis=-1)
    c, s = jnp.cos(ang), jnp.sin(ang)
    rep = LANES // dim
    return jnp.tile(jnp.concatenate([c, c], -1), (1, rep)), jnp.tile(jnp.concatenate([-s, s], -1), (1, rep))


def _block_diag(x):
    g, r, c = x.shape[-3:]
    y = jnp.einsum('...grc,gh->...grhc', x, jnp.eye(g, dtype=x.dtype))
    return y.reshape(x.shape[:-3] + (g * r, g * c))


def _s5_params(lam_re, lam_im, log_dt, b_re, b_im, c_re, c_im, nb):
    lam = lax.complex(lam_re, lam_im)
    a_bar = jnp.exp(lam * jnp.exp(log_dt)[..., None])
    b_bar = ((a_bar - 1.0) / lam)[..., None] * lax.complex(b_re, b_im)
    bt = jnp.swapaxes(b_bar, -1, -2)
    bmat = jnp.concatenate([_block_diag(jnp.real(bt)), _block_diag(jnp.imag(bt))], axis=-1).astype(BF16)
    cmat = jnp.concatenate([_block_diag(jnp.swapaxes(c_re, -1, -2)),
                            -_block_diag(jnp.swapaxes(c_im, -1, -2))], axis=-2).astype(BF16)
    ns = a_bar.shape[1] * a_bar.shape[2]
    a_re = jnp.broadcast_to(jnp.real(a_bar).reshape(2, 1, ns), (2, nb, ns))
    a_im = jnp.broadcast_to(jnp.imag(a_bar).reshape(2, 1, ns), (2, nb, ns))
    return bmat, a_re, a_im, cmat


def kernel(x, c, ctx, c_ctx, mod_w, mod_b, norm_g, ev_w_in, ev_conv_w, ev_conv_b, lru_gate_w, lru_gate_b, lru_lam, ret_theta, ret_gn, ev_w_out, ffd_w_gate, ffd_w_up, ffd_w_down, od_w_in, s5_lam_re, s5_lam_im, s5_log_dt, s5_b_re, s5_b_im, s5_c_re, s5_c_im, s5_d, s5_glu_w, s5_glu_b, diff_lam, diff_gn, od_w_out, moe_router_w, moe_router_b, moe_w_gate, moe_w_up, moe_w_down):
    b, seq, d = x.shape
    nctx = ctx.shape[1]
    l = seq + nctx
    n = b * l
    depth = mod_w.shape[0]
    nlat_t = seq // TB
    assert b == SUBLANES and seq % TB == 0 and nctx % TB == 0 and seq % nctx == 0 and l % FFN_TM == 0

    s = jnp.concatenate([x, ctx], axis=1)
    cc = jnp.concatenate([c, c_ctx[None], jnp.zeros((2 * SUBLANES - b - 1, d), F32)], axis=0)
    mt = _modtab(cc, mod_w, mod_b).reshape(depth, 2 * SUBLANES, 6, d)
    modtab = jnp.stack([jnp.broadcast_to(mt[:, b:b + 1], (depth, b, 6, d)), mt[:, :b]], axis=2)

    cos_r, sin_r = _rope_tables(seq, RET_D)
    cos_d, sin_d = _rope_tables(seq, DIFF_DH)

    for layer in range(depth):
        i = layer // 2
        mtab = modtab[layer]
        ng = norm_g[layer]
        if layer % 2 == 0:
            lw = ev_conv_w.shape[-1]
            u3d, zb = _inproj(functools.partial(_inproj_even_kernel, nlat=nlat_t, kscale=RET_D ** -0.5),
                              s, mtab, ng[0:1], ev_w_in[i].astype(BF16), cos_r, sin_r, lw, 5 * lw, nlat_t,
                              "inproj_even")
            gw = _block_diag(lru_gate_w[i]).astype(BF16)
            lf, lb = _lru(u3d.reshape(l * b, lw), ev_conv_w[i].reshape(CONV_W, 1, lw), ev_conv_b[i].reshape(1, lw),
                          gw, lru_gate_b[i].reshape(2, 2, 1, lw), lru_lam[i].reshape(2, 1, lw), b, seq, nctx)
            rf, rb = _retention(jax.nn.log_sigmoid(ret_theta[i].astype(F32)), zb, nlat_t)
            s, h2 = _outproj_even(lf.reshape(l, b * lw), lb.reshape(l, b * lw), zb, rf, rb, ret_gn[i].reshape(1, -1),
                                  ev_w_out[i].astype(BF16), s, mtab, ng, nlat_t)
            s = _ffn(h2.reshape(n, d), ffd_w_gate[i].astype(BF16), ffd_w_up[i].astype(BF16),
                     ffd_w_down[i].astype(BF16), s, mtab, ng[3:4], seq)
        else:
            sw = s5_d.shape[-1]
            aw = DIFF_HEADS * 2 * DIFF_DH
            lam_init = 0.8 - 0.6 * math.exp(-0.3 * layer)
            u3d, zb = _inproj(functools.partial(_inproj_odd_kernel, nlat=nlat_t, qscale=DIFF_DH ** -0.5 * math.log2(math.e), s5w=sw),
                              s, mtab, ng[0:1], od_w_in[i].astype(BF16), cos_d, sin_d, sw, 3 * aw, nlat_t,
                              "inproj_odd")
            bmat, a_re, a_im, cmat = _s5_params(s5_lam_re[i], s5_lam_im[i], s5_log_dt[i], s5_b_re[i], s5_b_im[i],
                                                s5_c_re[i], s5_c_im[i], b)
            yf, yb = _s5(u3d.reshape(l * b, sw), bmat, a_re, a_im, cmat, b, seq, nctx)
            lp = diff_lam[i].astype(F32)
            lam = (jnp.exp(jnp.sum(lp[0] * lp[1])) - jnp.exp(jnp.sum(lp[2] * lp[3])) + lam_init).reshape(1)
            att = _attention(lam, zb, diff_gn[i], seq, 1.0 - lam_init)
            rw = jnp.zeros((d, LANES), F32).at[:, :N_EXPERTS].set(moe_router_w[i])
            rbias = jnp.full((1, LANES), NEG, F32).at[0, :N_EXPERTS].set(moe_router_b[i])
            s, h2x, gates = _outproj_odd(yf.reshape(l, b * sw), yb.reshape(l, b * sw), u3d.reshape(l, b * sw), att,
                                        s5_d[i].reshape(1, sw), s5_glu_w[i].astype(BF16), s5_glu_b[i].reshape(1, sw),
                                        od_w_out[i].astype(BF16), s, mtab, ng, rw, rbias, nlat_t)
            n_tiles = (2 * n) // MOE_TM + N_EXPERTS + 2
            plan = _moe_plan(gates.reshape(n, LANES), n_tiles)
            ys = _moe(h2x.reshape(n, d + LANES), plan, moe_w_gate[i].astype(BF16), moe_w_up[i].astype(BF16),
                      moe_w_down[i].astype(BF16), n_tiles)
            s = _moe_combine(s, ys, plan, mtab, ng[3:4], nlat_t)
    return s[:, :seq]
```

```python
import functools
import math

import jax
import jax.numpy as jnp
from jax import lax
from jax.experimental import pallas as pl
from jax.experimental.pallas import tpu as pltpu

F32 = jnp.float32
BF16 = jnp.bfloat16

EPS = 1e-6
ROPE_BASE = 10000.0
GRID_W = 64
LRU_C = 8.0
LRU_BLOCKS = 8
CONV_W = 4
CONV_LEFT = 2
RET_HEADS = 4
RET_D = 128
S5_GROUP = 16
S5_STATE = 64
DIFF_HEADS = 6
DIFF_DH = 64
N_EXPERTS = 8
LANES = 128
SUBLANES = 8

TB = 256
LRU_TT = 64
S5_TT = 32
ATTN_TQ = 1024
ATTN_RB = 128
FFN_TM = 768
FFN_TF = 512
MOE_TM = 256
MOE_ALIGN = 16
MOE_SUB = 128
MOE_SUBS_PER_WIN = (TB + MOE_ALIGN - 1 + MOE_SUB - 1) // MOE_SUB
MOE_SLOTS = (N_EXPERTS * (MOE_ALIGN - 1 + MOE_SUB - 1) + 2 * TB) // MOE_SUB
MOE_NBUF = 6
VMEM_LIMIT = 56 * 1024 * 1024
NEG = -1e30


def _cp(*sem):
    return pltpu.CompilerParams(dimension_semantics=sem, vmem_limit_bytes=VMEM_LIMIT)


def _dot(a, b):
    return jnp.dot(a, b, preferred_element_type=F32)


def _split(x):
    hi = x.astype(BF16)
    lo = (x - hi.astype(F32)).astype(BF16)
    return hi, lo


def _dot3(a, w):
    ah, al = _split(a)
    wh, wl = _split(w)
    return _dot(ah, wh) + (_dot(ah, wl) + _dot(al, wh))


def _rms(x, g):
    return x * lax.rsqrt(jnp.mean(x * x, axis=-1, keepdims=True) + EPS) * g


def _sigmoid(x):
    return 1.0 / (1.0 + jnp.exp(-x))


def _silu(x):
    return x * _sigmoid(x)


def _gelu(x):
    return 0.5 * x * (1.0 + jnp.tanh(math.sqrt(2.0 / math.pi) * (x + 0.044715 * (x * x * x))))


def _softplus(x):
    return jnp.maximum(x, 0.0) + jnp.log(1.0 + jnp.exp(-jnp.abs(x)))


def _modtab_kernel(c_ref, w_ref, b_ref, o_ref):
    c = c_ref[...]
    o_ref[0] = _dot3(_silu(c), w_ref[0]) + b_ref[0]


def _modtab(cc, mod_w, mod_b):
    depth, d, n = mod_w.shape
    tn = 1536
    return pl.pallas_call(
        _modtab_kernel,
        grid=(depth, n // tn),
        in_specs=[pl.BlockSpec((cc.shape[0], d), lambda l, j: (0, 0)),
                  pl.BlockSpec((1, d, tn), lambda l, j: (l, 0, j)),
                  pl.BlockSpec((1, 1, tn), lambda l, j: (l, 0, j))],
        out_specs=pl.BlockSpec((1, cc.shape[0], tn), lambda l, j: (l, 0, j)),
        out_shape=jax.ShapeDtypeStruct((depth, cc.shape[0], n), F32),
        compiler_params=_cp("arbitrary", "arbitrary"),
        name="modtab",
    )(cc, mod_w, mod_b.reshape(depth, 1, n))


def _norm_mod_in(s_ref, mod_ref, g_ref):
    mod = mod_ref[0, 0]
    h = _rms(s_ref[0], g_ref[...]) * (1.0 + mod[1:2]) + mod[0:1]
    return h.astype(BF16)


def _inproj_even_kernel(s_ref, mod_ref, g_ref, w_ref, cos_ref, sin_ref, u_ref, zb_ref, *, nlat, kscale):
    j = pl.program_id(0)
    hb = _norm_mod_in(s_ref, mod_ref, g_ref)
    lw = RET_HEADS * RET_D

    def proj(i):
        return _dot(hb, w_ref[:, i * lw:(i + 1) * lw])

    u_ref[:, pl.ds(pl.program_id(1), 1), :] = proj(0)[:, None, :]
    zb_ref[0, :, 0:lw] = proj(1).astype(BF16)
    lat = j < nlat
    cos = jnp.where(lat, cos_ref[...], 1.0)
    sin = jnp.where(lat, sin_ref[...], 0.0)
    for sec, scale in ((0, 1.0), (1, kscale)):
        t = proj(2 + sec)
        for hd in range(RET_HEADS):
            th = t[:, hd * RET_D:(hd + 1) * RET_D]
            th = (th * cos + pltpu.roll(th, RET_D // 2, 1) * sin) * scale
            zb_ref[0, :, (1 + sec) * lw + hd * RET_D:(1 + sec) * lw + (hd + 1) * RET_D] = th.astype(BF16)
    zb_ref[0, :, 3 * lw:4 * lw] = proj(4).astype(BF16)
    zb_ref[0, :, 4 * lw:5 * lw] = proj(5).astype(BF16)


def _inproj_odd_kernel(s_ref, mod_ref, g_ref, w_ref, cos_ref, sin_ref, u_ref, zb_ref, ub_ref, *, nlat, qscale, s5w):
    j = pl.program_id(0)
    hb = _norm_mod_in(s_ref, mod_ref, g_ref)
    aw = DIFF_HEADS * 2 * DIFF_DH
    u = _dot(hb, w_ref[:, 0:s5w])
    u_ref[:, pl.ds(pl.program_id(1), 1), :] = u[:, None, :]
    ub_ref[0] = u
    lat = j < nlat
    cos = jnp.where(lat, cos_ref[...], 1.0)
    sin = jnp.where(lat, sin_ref[...], 0.0)
    lane = lax.broadcasted_iota(jnp.int32, (TB, LANES), 1)
    first_half = (lane % DIFF_DH) < (DIFF_DH // 2)
    for sec, scale in ((0, qscale), (1, 1.0)):
        t = _dot(hb, w_ref[:, s5w + sec * aw:s5w + (sec + 1) * aw])
        for hd in range(DIFF_HEADS):
            th = t[:, hd * LANES:(hd + 1) * LANES]
            partner = jnp.where(first_half, pltpu.roll(th, LANES - DIFF_DH // 2, 1), pltpu.roll(th, DIFF_DH // 2, 1))
            th = (th * cos + partner * sin) * scale
            zb_ref[0, :, sec * aw + hd * LANES:sec * aw + (hd + 1) * LANES] = th.astype(BF16)
    zb_ref[0, :, 2 * aw:3 * aw] = _dot(hb, w_ref[:, s5w + 2 * aw:s5w + 3 * aw]).astype(BF16)


def _inproj(body, s, modtab, g, w, cos, sin, uw, zw, nlat, name, u_rows=False):
    b, l, d = s.shape
    nj = l // TB
    return pl.pallas_call(
        body,
        grid=(nj, b),
        in_specs=[pl.BlockSpec((1, TB, d), lambda j, i: (i, j, 0)),
                  pl.BlockSpec((1, 1, 6, d), lambda j, i: (i, jnp.where(j < nlat, 1, 0), 0, 0)),
                  pl.BlockSpec((1, d), lambda j, i: (0, 0)),
                  pl.BlockSpec(w.shape, lambda j, i: (0, 0)),
                  pl.BlockSpec((TB, LANES), lambda j, i: (jnp.minimum(j, nlat - 1), 0)),
                  pl.BlockSpec((TB, LANES), lambda j, i: (jnp.minimum(j, nlat - 1), 0))],
        out_specs=[pl.BlockSpec((TB, b, uw), lambda j, i: (j, 0, 0)),
                   pl.BlockSpec((1, TB, zw), lambda j, i: (i, j, 0))]
        + ([pl.BlockSpec((1, TB, uw), lambda j, i: (i, j, 0))] if u_rows else []),
        out_shape=[jax.ShapeDtypeStruct((l, b, uw), F32),
                   jax.ShapeDtypeStruct((b, l, zw), BF16)]
        + ([jax.ShapeDtypeStruct((b, l, uw), F32)] if u_rows else []),
        compiler_params=_cp("arbitrary", "arbitrary"),
        name=name,
    )(s, modtab, g, w, cos, sin)


def _fwd_chunk(s, nlat_c, nctx_c):
    return jnp.where(s < nctx_c, nlat_c + s, s - nctx_c)


def _bwd_chunk(s, nlat_c, nctx_c):
    return nlat_c + nctx_c - 1 - s


def _lru_kernel(uf_ref, ufl_ref, ufr_ref, ub_ref, ubl_ref, ubr_ref, cw_ref, cb_ref, gw_ref, gb_ref, lam_ref,
                of_ref, ob_ref, pad_scr, a_scr, b_scr, h_scr, o_scr, *, tt, nlat_c, nctx_c, nb):
    s = pl.program_id(0)
    nc = nlat_c + nctx_c

    @pl.when(s == 0)
    def _():
        h_scr[...] = jnp.zeros_like(h_scr)

    rows = tt * nb
    views = ((uf_ref, ufl_ref, ufr_ref, _fwd_chunk(s, nlat_c, nctx_c)),
             (ub_ref, ubl_ref, ubr_ref, _bwd_chunk(s, nlat_c, nctx_c)))
    for d, (u_ref, l_ref, r_ref, c) in enumerate(views):
        first = (c == 0) | (c == nlat_c)
        last = (c == nlat_c - 1) | (c == nc - 1)
        pad_scr[0:CONV_LEFT * nb] = jnp.where(first, 0.0, l_ref[...])
        pad_scr[CONV_LEFT * nb:CONV_LEFT * nb + rows] = u_ref[...]
        pad_scr[CONV_LEFT * nb + rows:(CONV_W - 1) * nb + rows] = jnp.where(last, 0.0, r_ref[...])
        uc = cb_ref[...] + pad_scr[0:rows] * cw_ref[0]
        for k in range(1, CONV_W):
            uc = uc + pad_scr[k * nb:k * nb + rows] * cw_ref[k]
        ucb = uc.astype(BF16)
        r = _sigmoid(_dot(ucb, gw_ref[d, 0]) + gb_ref[d, 0])
        i = _sigmoid(_dot(ucb, gw_ref[d, 1]) + gb_ref[d, 1])
        a = jnp.exp((-LRU_C * _softplus(-lam_ref[d])) * r)
        a_scr[d] = a
        b_scr[d] = jnp.sqrt(1.0 - a * a) * (i * uc)

    def body(t, carry):
        hf, hb = carry
        rf = pl.multiple_of(t * nb, nb)
        rb = pl.multiple_of((tt - 1 - t) * nb, nb)
        hf = a_scr[0, pl.ds(rf, nb), :] * hf + b_scr[0, pl.ds(rf, nb), :]
        hb = a_scr[1, pl.ds(rb, nb), :] * hb + b_scr[1, pl.ds(rb, nb), :]
        for g in range(ng):
            o_scr[0, g, pl.ds(rf, nb), :] = hf[:, g * LANES:(g + 1) * LANES]
            o_scr[1, g, pl.ds(rb, nb), :] = hb[:, g * LANES:(g + 1) * LANES]
        return hf, hb

    ng = o_scr.shape[1]
    hf, hb = lax.fori_loop(0, tt, body, (h_scr[0], h_scr[1]), unroll=8)
    h_scr[0] = hf
    h_scr[1] = hb
    for bi in range(nb):
        for g in range(ng):
            of_ref[bi, :, g * LANES:(g + 1) * LANES] = o_scr[0, g, pl.ds(bi, tt, stride=nb), :]
            ob_ref[bi, :, g * LANES:(g + 1) * LANES] = o_scr[1, g, pl.ds(bi, tt, stride=nb), :]


def _lru(u_tm, conv_w, conv_b, gate_w, gate_b, lam, nb, nlat, nctx):
    rows_total, c = u_tm.shape
    tt = LRU_TT
    nlat_c, nctx_c = nlat // tt, nctx // tt
    nc = nlat_c + nctx_c
    rows = tt * nb
    fwd = functools.partial(_fwd_chunk, nlat_c=nlat_c, nctx_c=nctx_c)
    bwd = functools.partial(_bwd_chunk, nlat_c=nlat_c, nctx_c=nctx_c)
    lrows, rrows = CONV_LEFT * nb, (CONV_W - 1 - CONV_LEFT) * nb
    nl, nr = rows // lrows, rows // rrows

    def cur(f):
        return pl.BlockSpec((rows, c), lambda s: (f(s), 0))

    def left(f):
        return pl.BlockSpec((lrows, c), lambda s: (jnp.maximum(f(s) * nl - 1, 0), 0))

    def right(f):
        return pl.BlockSpec((rrows, c), lambda s: (jnp.minimum((f(s) + 1) * nr, nc * nr - 1), 0))

    def whole(a):
        return pl.BlockSpec(a.shape, lambda s: (0,) * a.ndim)

    return pl.pallas_call(
        functools.partial(_lru_kernel, tt=tt, nlat_c=nlat_c, nctx_c=nctx_c, nb=nb),
        grid=(nc,),
        in_specs=[cur(fwd), left(fwd), right(fwd), cur(bwd), left(bwd), right(bwd),
                  whole(conv_w), whole(conv_b), whole(gate_w), whole(gate_b), whole(lam)],
        out_specs=[pl.BlockSpec((nb, tt, c), lambda s: (0, fwd(s), 0)),
                   pl.BlockSpec((nb, tt, c), lambda s: (0, bwd(s), 0))],
        out_shape=[jax.ShapeDtypeStruct((nb, rows_total // nb, c), F32)] * 2,
        scratch_shapes=[pltpu.VMEM((rows + (CONV_W - 1) * nb, c), F32),
                        pltpu.VMEM((2, rows, c), F32),
                        pltpu.VMEM((2, rows, c), F32),
                        pltpu.VMEM((2, nb, c), F32),
                        pltpu.VMEM((2, c // LANES, rows, LANES), F32)],
        compiler_params=_cp("arbitrary"),
        name="rglru",
    )(u_tm, u_tm, u_tm, u_tm, u_tm, u_tm, conv_w, conv_b, gate_w, gate_b, lam)


def _ret_kernel(lg_ref, qf_ref, kf_ref, vf_ref, qb_ref, kb_ref, vb_ref, of_ref, ob_ref, s_scr, d_scr, *, c):
    s = pl.program_id(1)
    ii = lax.broadcasted_iota(jnp.int32, (c, c), 0).astype(F32)
    jj = lax.broadcasted_iota(jnp.int32, (c, c), 1).astype(F32)

    @pl.when(s == 0)
    def _():
        s_scr[...] = jnp.zeros_like(s_scr)
        for d in range(2):
            diff = (ii - jj) if d == 0 else (jj - ii)
            for h in range(RET_HEADS):
                d_scr[d, h] = jnp.where(diff >= 0, jnp.exp(jnp.maximum(diff, 0.0) * lg_ref[d, h]), 0.0)

    ri = lax.broadcasted_iota(jnp.int32, (c, 1), 0).astype(F32)
    views = ((qf_ref, kf_ref, vf_ref, of_ref), (qb_ref, kb_ref, vb_ref, ob_ref))
    for d, (q_ref, k_ref, v_ref, o_ref) in enumerate(views):
        for h in range(RET_HEADS):
            lg = lg_ref[d, h]
            cols = slice(h * RET_D, (h + 1) * RET_D)
            q, k, v = q_ref[0, :, cols], k_ref[0, :, cols], v_ref[0, :, cols]
            sc = lax.dot_general(q, k, (((1,), (1,)), ((), ())), preferred_element_type=F32) * d_scr[d, h]
            inner = _dot(sc.astype(BF16), v)
            st = s_scr[d, h]
            qdec = jnp.exp(((ri + 1.0) if d == 0 else (c - ri)) * lg)
            cross = _dot(q, st.astype(BF16)) * qdec
            o_ref[0, :, cols] = inner + cross
            kdec = jnp.exp(((c - 1.0 - ri) if d == 0 else ri) * lg)
            kd = (k.astype(F32) * kdec).T.astype(BF16)
            gc = jnp.exp(jnp.zeros((RET_D, RET_D), F32) + c * lg)
            s_scr[d, h] = gc * st + _dot(kd, v)


def _retention(log_g, zb, nlat):
    b, l, _ = zb.shape
    nj = l // TB
    w = RET_HEADS * RET_D

    def fwd(s):
        return jnp.where(s < nj - nlat, nlat + s, s - (nj - nlat))

    def bwd(s):
        return nj - 1 - s

    def col(f, i):
        return pl.BlockSpec((1, TB, w), lambda bi, s: (bi, f(s), i))

    def out(f):
        return pl.BlockSpec((1, TB, w), lambda bi, s: (bi, f(s), 0))

    return pl.pallas_call(
        functools.partial(_ret_kernel, c=TB),
        grid=(b, nj),
        in_specs=[pl.BlockSpec(memory_space=pltpu.SMEM),
                  col(fwd, 1), col(fwd, 2), col(fwd, 3), col(bwd, 1), col(bwd, 2), col(bwd, 3)],
        out_specs=[out(fwd), out(bwd)],
        out_shape=[jax.ShapeDtypeStruct((b, l, w), F32)] * 2,
        scratch_shapes=[pltpu.VMEM((2, RET_HEADS, RET_D, RET_D), F32),
                        pltpu.VMEM((2, RET_HEADS, TB, TB), F32)],
        compiler_params=_cp("arbitrary", "arbitrary"),
        name="retention",
    )(log_g, zb, zb, zb, zb, zb, zb)


def _s5_kernel(uf_ref, ub_ref, bm_ref, ar_ref, ai_ref, cm_ref, yf_ref, yb_ref, bu_scr, h_scr, y_scr, *, tt, nb, ns):
    s = pl.program_id(0)

    @pl.when(s == 0)
    def _():
        h_scr[...] = jnp.zeros_like(h_scr)

    cg = 512
    for d, (u_ref, y_ref) in enumerate(((uf_ref, yf_ref), (ub_ref, yb_ref))):
        bu_scr[...] = _dot(u_ref[...].astype(BF16), bm_ref[d])
        for g in range(ns // cg):
            re = slice(g * cg, (g + 1) * cg)
            im = slice(ns + g * cg, ns + (g + 1) * cg)
            ar = ar_ref[d, :, re]
            ai = ai_ref[d, :, re]

            def body(i, carry, re=re, im=im, ar=ar, ai=ai, d=d):
                hr, hi = carry
                t = i if d == 0 else tt - 1 - i
                r0 = pl.multiple_of(t * nb, nb)
                nr = ar * hr - ai * hi + bu_scr[pl.ds(r0, nb), re]
                ni = ar * hi + ai * hr + bu_scr[pl.ds(r0, nb), im]
                bu_scr[pl.ds(r0, nb), re] = nr
                bu_scr[pl.ds(r0, nb), im] = ni
                return nr, ni

            hr, hi = lax.fori_loop(0, tt, body, (h_scr[d, :, re], h_scr[d, :, im]), unroll=4)
            h_scr[d, :, re] = hr
            h_scr[d, :, im] = hi
        y = _dot(bu_scr[...].astype(BF16), cm_ref[d])
        for g in range(y_scr.shape[0]):
            y_scr[g] = y[:, g * LANES:(g + 1) * LANES]
        for bi in range(nb):
            for g in range(y_scr.shape[0]):
                y_ref[bi, :, g * LANES:(g + 1) * LANES] = y_scr[g, pl.ds(bi, tt, stride=nb), :]


def _s5(u_tm, bmat, a_re, a_im, cmat, nb, nlat, nctx):
    rows_total, c = u_tm.shape
    tt = S5_TT
    nlat_c, nctx_c = nlat // tt, nctx // tt
    rows = tt * nb
    ns = a_re.shape[-1]
    fwd = functools.partial(_fwd_chunk, nlat_c=nlat_c, nctx_c=nctx_c)
    bwd = functools.partial(_bwd_chunk, nlat_c=nlat_c, nctx_c=nctx_c)

    def cur(f):
        return pl.BlockSpec((rows, c), lambda s: (f(s), 0))

    def whole(a):
        return pl.BlockSpec(a.shape, lambda s: (0,) * a.ndim)

    return pl.pallas_call(
        functools.partial(_s5_kernel, tt=tt, nb=nb, ns=ns),
        grid=(nlat_c + nctx_c,),
        in_specs=[cur(fwd), cur(bwd), whole(bmat), whole(a_re), whole(a_im), whole(cmat)],
        out_specs=[pl.BlockSpec((nb, tt, c), lambda s: (0, fwd(s), 0)),
                   pl.BlockSpec((nb, tt, c), lambda s: (0, bwd(s), 0))],
        out_shape=[jax.ShapeDtypeStruct((nb, rows_total // nb, c), F32)] * 2,
        scratch_shapes=[pltpu.VMEM((rows, 2 * ns), F32), pltpu.VMEM((2, nb, 2 * ns), F32),
                        pltpu.VMEM((c // LANES, rows, LANES), F32)],
        compiler_params=_cp("arbitrary"),
        name="s5",
    )(u_tm, u_tm, bmat, a_re, a_im, cmat)


def _attn_kernel(lam_ref, q_ref, k_ref, v_ref, gn_ref, o_ref, *, out_scale):
    tq, tk = q_ref.shape[1], k_ref.shape[1]
    q = q_ref[0]
    lane = lax.broadcasted_iota(jnp.int32, (tq, LANES), 1)
    zero = jnp.zeros_like(q)
    qm = (jnp.where(lane < DIFF_DH, q, zero), jnp.where(lane >= DIFF_DH, q, zero))
    kc = k_ref[0]
    va = jnp.concatenate([v_ref[0], jnp.ones((tk, LANES), BF16)], axis=1)
    rb = min(ATTN_RB, tq)
    for r in range(tq // rb):
        rows = slice(r * rb, (r + 1) * rb)
        outs = []
        for m in range(2):
            sc = lax.dot_general(qm[m][rows], kc, (((1,), (1,)), ((), ())), preferred_element_type=F32)
            p = jnp.exp2(sc - jnp.max(sc, axis=1, keepdims=True)).astype(BF16)
            a = _dot(p, va)
            outs.append(a[:, :LANES] / a[:, LANES:LANES + 1])
        att = outs[0] - lam_ref[0] * outs[1]
        att = att * lax.rsqrt(jnp.mean(att * att, axis=-1, keepdims=True) + EPS) * (gn_ref[0] * out_scale)
        o_ref[0, rows] = att.astype(BF16)


def _attention(lam, zb, gn, nlat, out_scale):
    b, l, _ = zb.shape
    h = DIFF_HEADS
    nctx = l - nlat
    tq = ATTN_TQ if nlat % ATTN_TQ == 0 else TB
    gn3 = gn.reshape(h, 1, LANES)

    def call(tq_, nq, kv_rows, q0, kv0, name):
        return pl.pallas_call(
            functools.partial(_attn_kernel, out_scale=out_scale),
            grid=(b, h, nq),
            in_specs=[pl.BlockSpec(memory_space=pltpu.SMEM),
                      pl.BlockSpec((1, tq_, LANES), lambda bi, hi, qi: (bi, q0 + qi, hi)),
                      pl.BlockSpec((1, kv_rows, LANES), lambda bi, hi, qi: (bi, kv0, h + hi)),
                      pl.BlockSpec((1, kv_rows, LANES), lambda bi, hi, qi: (bi, kv0, 2 * h + hi)),
                      pl.BlockSpec((1, 1, LANES), lambda bi, hi, qi: (hi, 0, 0))],
            out_specs=pl.BlockSpec((1, tq_, LANES), lambda bi, hi, qi: (bi, qi, hi)),
            out_shape=jax.ShapeDtypeStruct((b, nq * tq_, h * LANES), BF16),
            compiler_params=_cp("arbitrary", "arbitrary", "arbitrary"),
            name=name,
        )(lam, zb, zb, zb, gn3)

    att_x = call(tq, nlat // tq, l, 0, 0, "diffattn_latent")
    att_c = call(nctx, 1, nctx, nlat // nctx, nlat // nctx, "diffattn_context")
    return jnp.concatenate([att_x, att_c], axis=1)


def _out_tail(parts, w_ref, s_ref, mod, ng_ref, xo_ref):
    y, r0 = None, 0
    for p in parts:
        n = p.shape[1]
        t = _dot(p.astype(BF16), w_ref[r0:r0 + n, :])
        y = t if y is None else y + t
        r0 += n
    xn = s_ref[0] + mod[2:3] * _rms(y, ng_ref[1:2])
    xo_ref[0] = xn
    return _rms(xn, ng_ref[2:3]) * (1.0 + mod[4:5]) + mod[3:4]


def _outproj_even_kernel(lf_ref, lb_ref, zg_ref, zo_ref, rf_ref, rb_ref, gn_ref, w_ref, s_ref, mod_ref, ng_ref,
                         xo_ref, h2_ref):
    parts = [(lf_ref[0] + lb_ref[0]) * _gelu(zg_ref[0].astype(F32))]
    r = rf_ref[0] + rb_ref[0]
    og = _silu(zo_ref[0].astype(F32))
    for h in range(RET_HEADS):
        cols = slice(h * RET_D, (h + 1) * RET_D)
        y = r[:, cols]
        y = y - jnp.mean(y, axis=-1, keepdims=True)
        y = y * lax.rsqrt(jnp.mean(y * y, axis=-1, keepdims=True) + EPS) * gn_ref[:, cols]
        parts.append(y * og[:, cols])
    h2 = _out_tail(parts, w_ref, s_ref, mod_ref[0, 0], ng_ref, xo_ref)
    h2_ref[0] = h2.astype(BF16)


def _outproj_odd_kernel(yf_ref, yb_ref, u_ref, att_ref, d_ref, gw_ref, gb_ref, w_ref, s_ref, mod_ref, ng_ref,
                        rw_ref, rb_ref, xo_ref, h2_ref, gates_ref):
    z = _gelu(yf_ref[0] + yb_ref[0] + d_ref[...] * u_ref[0])
    s5o = z * _sigmoid(_dot(z.astype(BF16), gw_ref[...]) + gb_ref[...])
    h2 = _out_tail([s5o, att_ref[0]], w_ref, s_ref, mod_ref[0, 0], ng_ref, xo_ref)
    d = h2.shape[1]
    h2_ref[0, :, :d] = h2.astype(BF16)
    logits = _dot3(h2, rw_ref[...]) + rb_ref[...]
    lane = lax.broadcasted_iota(jnp.int32, logits.shape, 1).astype(F32)
    m1 = jnp.max(logits, axis=1, keepdims=True)
    i1 = jnp.min(jnp.where(logits == m1, lane, float(LANES)), axis=1, keepdims=True)
    rest = jnp.where(lane == i1, 2.0 * NEG, logits)
    m2 = jnp.max(rest, axis=1, keepdims=True)
    i2 = jnp.min(jnp.where(rest == m2, lane, float(LANES)), axis=1, keepdims=True)
    e = jnp.exp(m2 - m1)
    p1 = 1.0 / (1.0 + e)
    gates = jnp.where(lane == i1, p1, 0.0) + jnp.where(lane == i2, e * p1, 0.0)
    ids = jnp.where(lane == float(N_EXPERTS), i1, 0.0) + jnp.where(lane == float(N_EXPERTS + 1), i2, 0.0)
    gates_ref[0] = gates + ids
    ghi = gates.astype(BF16).astype(F32)
    glo = (gates - ghi).astype(BF16).astype(F32)
    h2_ref[0, :, d:] = (ghi + pltpu.roll(glo, N_EXPERTS, 1)).astype(BF16)


def _whole2(a):
    return pl.BlockSpec(a.shape, lambda i, j: (0,) * a.ndim)


def _mod_spec(d, nlat_t):
    return pl.BlockSpec((1, 1, 6, d), lambda i, j: (i, jnp.where(j < nlat_t, 1, 0), 0, 0))


def _outproj_even(lf, lb, zb, rf, rb, gn, w, s, modtab, ng, nlat_t):
    b, l, d = s.shape
    cw = RET_HEADS * RET_D
    tm = pl.BlockSpec((1, TB, cw), lambda i, j: (i, j, 0))
    row = pl.BlockSpec((1, TB, d), lambda i, j: (i, j, 0))
    return pl.pallas_call(
        _outproj_even_kernel,
        grid=(b, l // TB),
        in_specs=[tm, tm,
                  pl.BlockSpec((1, TB, cw), lambda i, j: (i, j, 0)),
                  pl.BlockSpec((1, TB, cw), lambda i, j: (i, j, 4)),
                  pl.BlockSpec((1, TB, cw), lambda i, j: (i, j, 0)),
                  pl.BlockSpec((1, TB, cw), lambda i, j: (i, j, 0)),
                  _whole2(gn), _whole2(w), row, _mod_spec(d, nlat_t), _whole2(ng)],
        out_specs=[row, row],
        out_shape=[jax.ShapeDtypeStruct((b, l, d), F32), jax.ShapeDtypeStruct((b, l, d), BF16)],
        compiler_params=_cp("arbitrary", "arbitrary"),
        name="outproj_even",
    )(lf, lb, zb, zb, rf, rb, gn, w, s, modtab, ng)


def _outproj_odd(yf, yb, u, att, dskip, glu_w, glu_b, w, s, modtab, ng, rw, rb, nlat_t):
    b, l, d = s.shape
    sw = dskip.shape[1]
    aw = att.shape[2]
    tm = pl.BlockSpec((1, TB, sw), lambda i, j: (i, j, 0))
    row = pl.BlockSpec((1, TB, d), lambda i, j: (i, j, 0))
    return pl.pallas_call(
        _outproj_odd_kernel,
        grid=(b, l // TB),
        in_specs=[tm, tm, tm,
                  pl.BlockSpec((1, TB, aw), lambda i, j: (i, j, 0)),
                  _whole2(dskip), _whole2(glu_w), _whole2(glu_b), _whole2(w), row, _mod_spec(d, nlat_t),
                  _whole2(ng), _whole2(rw), _whole2(rb)],
        out_specs=[row, pl.BlockSpec((1, TB, d + LANES), lambda i, j: (i, j, 0)),
                   pl.BlockSpec((1, TB, LANES), lambda i, j: (i, j, 0))],
        out_shape=[jax.ShapeDtypeStruct((b, l, d), F32), jax.ShapeDtypeStruct((b, l, d + LANES), BF16),
                   jax.ShapeDtypeStruct((b, l, LANES), F32)],
        compiler_params=_cp("arbitrary", "arbitrary"),
        name="outproj_odd",
    )(yf, yb, u, att, dskip, glu_w, glu_b, w, s, modtab, ng, rw, rb)


def _ffn_kernel(x_ref, wg_ref, wu_ref, wd_ref, s_ref, mod_ref, g_ref, o_ref, acc_ref, *, tpb, seq):
    i, f = pl.program_id(0), pl.program_id(1)

    @pl.when(f == 0)
    def _():
        acc_ref[...] = jnp.zeros_like(acc_ref)

    x = x_ref[...]
    act = _silu(_dot(x, wg_ref[...])) * _dot(x, wu_ref[...])
    acc_ref[...] += _dot(act.astype(BF16), wd_ref[...])

    @pl.when(f == pl.num_programs(1) - 1)
    def _():
        tm = acc_ref.shape[0]
        row = lax.rem(i, tpb) * tm + lax.broadcasted_iota(jnp.int32, (tm, 1), 0)
        mod = mod_ref[0]
        gate = jnp.where(row < seq, mod[1, 5:6], mod[0, 5:6])
        o_ref[...] = s_ref[...] + gate * _rms(acc_ref[...], g_ref[...])


def _ffn(x, wg, wu, wd, s, modtab, g, seq):
    b, l, d = s.shape
    n = b * l
    ff = wg.shape[1]
    tm, tf = FFN_TM, FFN_TF
    tpb = l // tm
    rows = pl.BlockSpec((tm, d), lambda i, f: (i, 0))
    out = pl.pallas_call(
        functools.partial(_ffn_kernel, tpb=tpb, seq=seq),
        grid=(n // tm, ff // tf),
        in_specs=[rows,
                  pl.BlockSpec((d, tf), lambda i, f: (0, f)),
                  pl.BlockSpec((d, tf), lambda i, f: (0, f)),
                  pl.BlockSpec((tf, d), lambda i, f: (f, 0)),
                  rows,
                  pl.BlockSpec((1, 2, 6, d), lambda i, f: (i // tpb, 0, 0, 0)),
                  pl.BlockSpec(g.shape, lambda i, f: (0, 0))],
        out_specs=rows,
        out_shape=jax.ShapeDtypeStruct((n, d), F32),
        scratch_shapes=[pltpu.VMEM((tm, d), F32)],
        compiler_params=_cp("arbitrary", "arbitrary"),
        name="dense_ffn",
    )(x, wg, wu, wd, s.reshape(n, d), modtab, g)
    return out.reshape(b, l, d)


def _moe_plan(route, n_tiles):
    n = route.shape[0]
    nb = n // TB
    ne = N_EXPERTS
    e1 = route[:, ne].astype(jnp.int32)
    e2 = route[:, ne + 1].astype(jnp.int32)
    ex = jnp.arange(ne, dtype=jnp.int32)
    sel = ((e1[:, None] == ex) | (e2[:, None] == ex)).astype(jnp.int32)
    incl = jnp.cumsum(sel, axis=0)
    count = incl[-1]
    gsize = (count + MOE_TM - 1) // MOE_TM * MOE_TM
    gend = jnp.cumsum(gsize)
    goff = gend - gsize
    posm = goff[None, :] + incl - sel
    pos1 = jnp.sum(jnp.where(e1[:, None] == ex, posm, 0), axis=1)
    pos2 = jnp.sum(jnp.where(e2[:, None] == ex, posm, 0), axis=1)
    pos_cols = jnp.stack([pos1, pos2], axis=1)
    pos_rows = jnp.stack([pos1.reshape(nb, TB), pos2.reshape(nb, TB)], axis=1)
    start = posm[::TB]
    last = (goff + count)[None]
    end = jnp.concatenate([start[1:], last], axis=0)
    r0 = jnp.arange(n_tiles, dtype=jnp.int32) * MOE_TM
    tile_valid = (r0 < gend[-1]).astype(jnp.int32)
    tile_exp = jnp.minimum(jnp.sum((r0[:, None] >= gend[None, :]).astype(jnp.int32), axis=1), ne - 1)
    st, en = start.T[tile_exp], end.T[tile_exp]
    lim = jnp.minimum(r0 + MOE_TM, last[0][tile_exp])
    tb_lo = jnp.clip(jnp.sum((en <= r0[:, None]).astype(jnp.int32), axis=1), 0, nb - 1)
    tb_hi = jnp.clip(jnp.sum((st < lim[:, None]).astype(jnp.int32), axis=1) - 1, 0, nb - 1)
    a0 = start // MOE_ALIGN * MOE_ALIGN
    lo_off = start - a0
    hi_off = lo_off + (end - start)
    k = jnp.arange(MOE_SUBS_PER_WIN, dtype=jnp.int32) * MOE_SUB
    c_valid = ((end > start)[..., None] & (hi_off[..., None] > k)).reshape(nb, -1)
    c_base = (a0[..., None] + k).reshape(nb, -1)
    c_lo = jnp.clip(lo_off[..., None] - k, 0, MOE_SUB).reshape(nb, -1)
    c_hi = jnp.clip(hi_off[..., None] - k, 0, MOE_SUB).reshape(nb, -1)
    slot = jnp.cumsum(c_valid.astype(jnp.int32), axis=1) - 1
    put = (slot[..., None] == jnp.arange(MOE_SLOTS, dtype=jnp.int32)) & c_valid[..., None]

    def place(v):
        return jnp.sum(jnp.where(put, v[..., None], 0), axis=1).reshape(-1)

    return (tile_exp, tile_valid, tb_lo, tb_hi, pos_rows, pos_cols, place(c_base), place(c_lo), place(c_hi))


def _moe_kernel(te_ref, tv_ref, lo_ref, hi_ref, pos_ref, h_hbm, wg_ref, wu_ref, wd_ref, y_ref, xbuf, xs_ref, sem,
                *, n_tiles, tf):
    i = pl.program_id(0)
    d, ff = wg_ref.shape[1], wg_ref.shape[2]

    def blk_copy(tb, slot):
        return pltpu.make_async_copy(h_hbm.at[pl.ds(pl.multiple_of(tb * TB, TB), TB)], xbuf.at[slot], sem.at[slot])

    def start_head(t):
        lo = lo_ref[t]
        cnt = hi_ref[t] - lo + 1
        for k in range(MOE_NBUF):
            @pl.when(k < cnt)
            def _(k=k):
                blk_copy(lo + k, k).start()

    valid = tv_ref[i] == 1

    @pl.when((i == 0) & valid)
    def _():
        start_head(0)

    @pl.when(valid)
    def _():
        lo = lo_ref[i]
        nblk = hi_ref[i] - lo + 1
        xs_ref[...] = jnp.zeros_like(xs_ref)
        rid = i * MOE_TM + lax.broadcasted_iota(jnp.int32, (MOE_TM, TB), 0)

        def body(k, c):
            slot = lax.rem(k, MOE_NBUF)
            tb = lo + k
            blk_copy(tb, slot).wait()
            p = pos_ref[tb]
            oh = (jnp.where(rid == p[0:1], 1.0, 0.0) + jnp.where(rid == p[1:2], 1.0, 0.0)).astype(BF16)
            xs_ref[...] += _dot(oh, xbuf[slot])

            @pl.when(k + MOE_NBUF < nblk)
            def _():
                blk_copy(tb + MOE_NBUF, slot).start()

            return c

        lax.fori_loop(0, nblk, body, 0)
        nxt = jnp.minimum(i + 1, n_tiles - 1)

        @pl.when((i + 1 < n_tiles) & (tv_ref[nxt] == 1))
        def _():
            start_head(nxt)

        xs = xs_ref[...]
        x = xs[:, :d].astype(BF16)
        ext = xs[:, d:]
        lane = lax.broadcasted_iota(jnp.int32, ext.shape, 1)
        e_t = te_ref[i]
        gate = jnp.sum(jnp.where((lane == e_t) | (lane == e_t + N_EXPERTS), ext, 0.0), axis=1, keepdims=True)
        acc = None
        for f in range(ff // tf):
            cols = slice(f * tf, (f + 1) * tf)
            act = _silu(_dot(x, wg_ref[0, :, cols])) * _dot(x, wu_ref[0, :, cols])
            t = _dot(act.astype(BF16), wd_ref[0, cols, :])
            acc = t if acc is None else acc + t
        y_ref[...] = (acc * gate).astype(BF16)

    @pl.when(jnp.logical_not(valid))
    def _():
        y_ref[...] = jnp.zeros_like(y_ref)


def _moe(h2x, plan, wg, wu, wd, n_tiles):
    n, dx = h2x.shape
    ne, d, ff = wg.shape
    tile_exp, tile_valid, tb_lo, tb_hi, pos_rows = plan[:5]

    def wspec(shape):
        return pl.BlockSpec(shape, lambda i, te, tv, lo, hi: (te[i], 0, 0), pipeline_mode=pl.Buffered(1))

    return pl.pallas_call(
        functools.partial(_moe_kernel, n_tiles=n_tiles, tf=FFN_TF),
        grid_spec=pltpu.PrefetchScalarGridSpec(
            num_scalar_prefetch=4,
            grid=(n_tiles,),
            in_specs=[pl.BlockSpec(pos_rows.shape, lambda i, te, tv, lo, hi: (0, 0, 0)),
                      pl.BlockSpec(memory_space=pl.ANY),
                      wspec((1, d, ff)), wspec((1, d, ff)), wspec((1, ff, d))],
            out_specs=pl.BlockSpec((MOE_TM, d), lambda i, te, tv, lo, hi: (i, 0)),
            scratch_shapes=[pltpu.VMEM((MOE_NBUF, TB, dx), BF16), pltpu.VMEM((MOE_TM, dx), F32),
                            pltpu.SemaphoreType.DMA((MOE_NBUF,))]),
        out_shape=jax.ShapeDtypeStruct((n_tiles * MOE_TM, d), BF16),
        compiler_params=_cp("arbitrary"),
        name="moe_ffn",
    )(tile_exp, tile_valid, tb_lo, tb_hi, pos_rows, h2x, wg, wu, wd)


def _combine_kernel(base_ref, lo_ref, hi_ref, s_ref, pos_ref, mod_ref, g_ref, ys_hbm, o_ref, buf, sem, *, nj, nsteps):
    step = pl.program_id(0) * nj + pl.program_id(1)
    slot = lax.rem(step, 2)

    def win_copy(t, k, sl):
        a = pl.multiple_of(base_ref[t * MOE_SLOTS + k], MOE_ALIGN)
        return pltpu.make_async_copy(ys_hbm.at[pl.ds(a, MOE_SUB)], buf.at[sl, pl.ds(k * MOE_SUB, MOE_SUB)],
                                     sem.at[sl, k])

    @pl.when(step == 0)
    def _():
        for k in range(MOE_SLOTS):
            win_copy(0, k, 0).start()

    @pl.when(step + 1 < nsteps)
    def _():
        for k in range(MOE_SLOTS):
            win_copy(step + 1, k, 1 - slot).start()

    pos = pos_ref[...]
    pos1 = jnp.broadcast_to(pos[:, 0:1], (TB, MOE_SUB))
    pos2 = jnp.broadcast_to(pos[:, 1:2], (TB, MOE_SUB))
    lane = lax.broadcasted_iota(jnp.int32, (1, MOE_SUB), 1)
    acc = jnp.zeros(s_ref.shape[1:], F32)
    per_dot = 2 * LANES // MOE_SUB
    for g in range(MOE_SLOTS // per_dot):
        pieces = []
        for k in range(g * per_dot, (g + 1) * per_dot):
            win_copy(step, k, slot).wait()
            j = step * MOE_SLOTS + k
            held = jnp.where((lane >= lo_ref[j]) & (lane < hi_ref[j]), lane + base_ref[j], -1)
            pieces.append(jnp.where(pos1 == held, 1.0, 0.0) + jnp.where(pos2 == held, 1.0, 0.0))
        oh = jnp.concatenate(pieces, axis=1).astype(BF16)
        rows = slice(g * per_dot * MOE_SUB, (g + 1) * per_dot * MOE_SUB)
        acc = acc + _dot(oh, buf[slot, rows])
    o_ref[0] = s_ref[0] + mod_ref[0, 0][5:6] * _rms(acc, g_ref[...])


def _moe_combine(s, ys, plan, modtab, g, nlat_t):
    b, l, d = s.shape
    nj = l // TB
    pos_cols, base, lo_off, hi_off = plan[5:]
    row = pl.BlockSpec((1, TB, d), lambda i, j, *_: (i, j, 0))
    return pl.pallas_call(
        functools.partial(_combine_kernel, nj=nj, nsteps=b * nj),
        grid_spec=pltpu.PrefetchScalarGridSpec(
            num_scalar_prefetch=3,
            grid=(b, nj),
            in_specs=[row,
                      pl.BlockSpec((TB, 2), lambda i, j, *_: (i * nj + j, 0)),
                      pl.BlockSpec((1, 1, 6, d), lambda i, j, *_: (i, jnp.where(j < nlat_t, 1, 0), 0, 0)),
                      pl.BlockSpec(g.shape, lambda i, j, *_: (0, 0)),
                      pl.BlockSpec(memory_space=pl.ANY)],
            out_specs=row,
            scratch_shapes=[pltpu.VMEM((2, MOE_SLOTS * MOE_SUB, d), BF16),
                            pltpu.SemaphoreType.DMA((2, MOE_SLOTS))]),
        out_shape=jax.ShapeDtypeStruct((b, l, d), F32),
        compiler_params=_cp("arbitrary", "arbitrary"),
        name="moe_combine",
    )(base, lo_off, hi_off, s, pos_cols, modtab, g, ys)


def _rope_tables(seq, dim):
    t = jnp.arange(seq)
    rows = (t // GRID_W).astype(F32)
    cols = (t % GRID_W).astype(F32)
    quarter = dim // 4
    inv = ROPE_BASE ** (-jnp.arange(quarter, dtype=F32) / quarter)
    ang = jnp.concatenate([rows[:, None] * inv, cols[:, None] * inv], axis=-1)
    c, s = jnp.cos(ang), jnp.sin(ang)
    rep = LANES // dim
    return jnp.tile(jnp.concatenate([c, c], -1), (1, rep)), jnp.tile(jnp.concatenate([-s, s], -1), (1, rep))


def _block_diag(x):
    g, r, c = x.shape[-3:]
    y = jnp.einsum('...grc,gh->...grhc', x, jnp.eye(g, dtype=x.dtype))
    return y.reshape(x.shape[:-3] + (g * r, g * c))


def _s5_params(lam_re, lam_im, log_dt, b_re, b_im, c_re, c_im, nb):
    lam = lax.complex(lam_re, lam_im)
    a_bar = jnp.exp(lam * jnp.exp(log_dt)[..., None])
    b_bar = ((a_bar - 1.0) / lam)[..., None] * lax.complex(b_re, b_im)
    bt = jnp.swapaxes(b_bar, -1, -2)
    bmat = jnp.concatenate([_block_diag(jnp.real(bt)), _block_diag(jnp.imag(bt))], axis=-1).astype(BF16)
    cmat = jnp.concatenate([_block_diag(jnp.swapaxes(c_re, -1, -2)),
                            -_block_diag(jnp.swapaxes(c_im, -1, -2))], axis=-2).astype(BF16)
    ns = a_bar.shape[1] * a_bar.shape[2]
    a_re = jnp.broadcast_to(jnp.real(a_bar).reshape(2, 1, ns), (2, nb, ns))
    a_im = jnp.broadcast_to(jnp.imag(a_bar).reshape(2, 1, ns), (2, nb, ns))
    return bmat, a_re, a_im, cmat


def kernel(x, c, ctx, c_ctx, mod_w, mod_b, norm_g, ev_w_in, ev_conv_w, ev_conv_b, lru_gate_w, lru_gate_b, lru_lam, ret_theta, ret_gn, ev_w_out, ffd_w_gate, ffd_w_up, ffd_w_down, od_w_in, s5_lam_re, s5_lam_im, s5_log_dt, s5_b_re, s5_b_im, s5_c_re, s5_c_im, s5_d, s5_glu_w, s5_glu_b, diff_lam, diff_gn, od_w_out, moe_router_w, moe_router_b, moe_w_gate, moe_w_up, moe_w_down):
    b, seq, d = x.shape
    nctx = ctx.shape[1]
    l = seq + nctx
    n = b * l
    depth = mod_w.shape[0]
    nlat_t = seq // TB
    assert b == SUBLANES and seq % TB == 0 and nctx % TB == 0 and seq % nctx == 0 and l % FFN_TM == 0

    s = jnp.concatenate([x, ctx], axis=1)
    cc = jnp.concatenate([c, c_ctx[None], jnp.zeros((2 * SUBLANES - b - 1, d), F32)], axis=0)
    mt = _modtab(cc, mod_w, mod_b).reshape(depth, 2 * SUBLANES, 6, d)
    modtab = jnp.stack([jnp.broadcast_to(mt[:, b:b + 1], (depth, b, 6, d)), mt[:, :b]], axis=2)

    cos_r, sin_r = _rope_tables(seq, RET_D)
    cos_d, sin_d = _rope_tables(seq, DIFF_DH)

    for layer in range(depth):
        i = layer // 2
        mtab = modtab[layer]
        ng = norm_g[layer]
        if layer % 2 == 0:
            lw = ev_conv_w.shape[-1]
            u3d, zb = _inproj(functools.partial(_inproj_even_kernel, nlat=nlat_t, kscale=RET_D ** -0.5),
                              s, mtab, ng[0:1], ev_w_in[i].astype(BF16), cos_r, sin_r, lw, 5 * lw, nlat_t,
                              "inproj_even")
            gw = _block_diag(lru_gate_w[i]).astype(BF16)
            lf, lb = _lru(u3d.reshape(l * b, lw), ev_conv_w[i].reshape(CONV_W, 1, lw), ev_conv_b[i].reshape(1, lw),
                          gw, lru_gate_b[i].reshape(2, 2, 1, lw), lru_lam[i].reshape(2, 1, lw), b, seq, nctx)
            rf, rb = _retention(jax.nn.log_sigmoid(ret_theta[i].astype(F32)), zb, nlat_t)
            s, h2 = _outproj_even(lf, lb, zb, rf, rb, ret_gn[i].reshape(1, -1),
                                  ev_w_out[i].astype(BF16), s, mtab, ng, nlat_t)
            s = _ffn(h2.reshape(n, d), ffd_w_gate[i].astype(BF16), ffd_w_up[i].astype(BF16),
                     ffd_w_down[i].astype(BF16), s, mtab, ng[3:4], seq)
        else:
            sw = s5_d.shape[-1]
            aw = DIFF_HEADS * 2 * DIFF_DH
            lam_init = 0.8 - 0.6 * math.exp(-0.3 * layer)
            u3d, zb, ub = _inproj(functools.partial(_inproj_odd_kernel, nlat=nlat_t, qscale=DIFF_DH ** -0.5 * math.log2(math.e), s5w=sw),
                                  s, mtab, ng[0:1], od_w_in[i].astype(BF16), cos_d, sin_d, sw, 3 * aw, nlat_t,
                                  "inproj_odd", u_rows=True)
            bmat, a_re, a_im, cmat = _s5_params(s5_lam_re[i], s5_lam_im[i], s5_log_dt[i], s5_b_re[i], s5_b_im[i],
                                                s5_c_re[i], s5_c_im[i], b)
            yf, yb = _s5(u3d.reshape(l * b, sw), bmat, a_re, a_im, cmat, b, seq, nctx)
            lp = diff_lam[i].astype(F32)
            lam = (jnp.exp(jnp.sum(lp[0] * lp[1])) - jnp.exp(jnp.sum(lp[2] * lp[3])) + lam_init).reshape(1)
            att = _attention(lam, zb, diff_gn[i], seq, 1.0 - lam_init)
            rw = jnp.zeros((d, LANES), F32).at[:, :N_EXPERTS].set(moe_router_w[i])
            rbias = jnp.full((1, LANES), NEG, F32).at[0, :N_EXPERTS].set(moe_router_b[i])
            s, h2x, gates = _outproj_odd(yf, yb, ub, att,
                                        s5_d[i].reshape(1, sw), s5_glu_w[i].astype(BF16), s5_glu_b[i].reshape(1, sw),
                                        od_w_out[i].astype(BF16), s, mtab, ng, rw, rbias, nlat_t)
            n_tiles = (2 * n) // MOE_TM + N_EXPERTS + 2
            plan = _moe_plan(gates.reshape(n, LANES), n_tiles)
            ys = _moe(h2x.reshape(n, d + LANES), plan, moe_w_gate[i].astype(BF16), moe_w_up[i].astype(BF16),
                      moe_w_down[i].astype(BF16), n_tiles)
            s = _moe_combine(s, ys, plan, mtab, ng[3:4], nlat_t)
    return s[:, :seq]
```

```python
import functools
import math

import jax
import jax.numpy as jnp
from jax import lax
from jax.experimental import pallas as pl
from jax.experimental.pallas import tpu as pltpu

F32 = jnp.float32
BF16 = jnp.bfloat16

EPS = 1e-6
ROPE_BASE = 10000.0
GRID_W = 64
LRU_C = 8.0
LRU_BLOCKS = 8
CONV_W = 4
CONV_LEFT = 2
RET_HEADS = 4
RET_D = 128
S5_GROUP = 16
S5_STATE = 64
DIFF_HEADS = 6
DIFF_DH = 64
N_EXPERTS = 8
LANES = 128
SUBLANES = 8

TB = 256
LRU_TT = 64
S5_TT = 32
ATTN_TQ = 1024
ATTN_RB = 128
FFN_TM = 768
FFN_TF = 512
MOE_TM = 256
MOE_ALIGN = 16
MOE_SUB = 128
MOE_SUBS_PER_WIN = (TB + MOE_ALIGN - 1 + MOE_SUB - 1) // MOE_SUB
MOE_SLOTS = (N_EXPERTS * (MOE_ALIGN - 1 + MOE_SUB - 1) + 2 * TB) // MOE_SUB
MOE_GROUP = 6
VMEM_LIMIT = 56 * 1024 * 1024
NEG = -1e30


def _cp(*sem):
    return pltpu.CompilerParams(dimension_semantics=sem, vmem_limit_bytes=VMEM_LIMIT)


def _dot(a, b):
    return jnp.dot(a, b, preferred_element_type=F32)


def _split(x):
    hi = x.astype(BF16)
    lo = (x - hi.astype(F32)).astype(BF16)
    return hi, lo


def _dot3(a, w):
    ah, al = _split(a)
    wh, wl = _split(w)
    return _dot(ah, wh) + (_dot(ah, wl) + _dot(al, wh))


def _rms(x, g):
    return x * lax.rsqrt(jnp.mean(x * x, axis=-1, keepdims=True) + EPS) * g


def _sigmoid(x):
    return 1.0 / (1.0 + jnp.exp(-x))


def _silu(x):
    return x * _sigmoid(x)


def _gelu(x):
    return 0.5 * x * (1.0 + jnp.tanh(math.sqrt(2.0 / math.pi) * (x + 0.044715 * (x * x * x))))


def _softplus(x):
    return jnp.maximum(x, 0.0) + jnp.log(1.0 + jnp.exp(-jnp.abs(x)))


def _modtab_kernel(c_ref, w_ref, b_ref, o_ref):
    c = c_ref[...]
    o_ref[0] = _dot3(_silu(c), w_ref[0]) + b_ref[0]


def _modtab(cc, mod_w, mod_b):
    depth, d, n = mod_w.shape
    tn = 1536
    return pl.pallas_call(
        _modtab_kernel,
        grid=(depth, n // tn),
        in_specs=[pl.BlockSpec((cc.shape[0], d), lambda l, j: (0, 0)),
                  pl.BlockSpec((1, d, tn), lambda l, j: (l, 0, j)),
                  pl.BlockSpec((1, 1, tn), lambda l, j: (l, 0, j))],
        out_specs=pl.BlockSpec((1, cc.shape[0], tn), lambda l, j: (l, 0, j)),
        out_shape=jax.ShapeDtypeStruct((depth, cc.shape[0], n), F32),
        compiler_params=_cp("arbitrary", "arbitrary"),
        name="modtab",
    )(cc, mod_w, mod_b.reshape(depth, 1, n))


def _norm_mod_in(s_ref, mod_ref, g_ref):
    mod = mod_ref[0, 0]
    h = _rms(s_ref[0], g_ref[...]) * (1.0 + mod[1:2]) + mod[0:1]
    return h.astype(BF16)


def _inproj_even_kernel(s_ref, mod_ref, g_ref, w_ref, cos_ref, sin_ref, u_ref, zb_ref, *, nlat, kscale):
    j = pl.program_id(0)
    hb = _norm_mod_in(s_ref, mod_ref, g_ref)
    lw = RET_HEADS * RET_D

    def proj(i):
        return _dot(hb, w_ref[:, i * lw:(i + 1) * lw])

    u_ref[:, pl.ds(pl.program_id(1), 1), :] = proj(0)[:, None, :]
    zb_ref[0, :, 0:lw] = proj(1).astype(BF16)
    lat = j < nlat
    cos = jnp.where(lat, cos_ref[...], 1.0)
    sin = jnp.where(lat, sin_ref[...], 0.0)
    for sec, scale in ((0, 1.0), (1, kscale)):
        t = proj(2 + sec)
        for hd in range(RET_HEADS):
            th = t[:, hd * RET_D:(hd + 1) * RET_D]
            th = (th * cos + pltpu.roll(th, RET_D // 2, 1) * sin) * scale
            zb_ref[0, :, (1 + sec) * lw + hd * RET_D:(1 + sec) * lw + (hd + 1) * RET_D] = th.astype(BF16)
    zb_ref[0, :, 3 * lw:4 * lw] = proj(4).astype(BF16)
    zb_ref[0, :, 4 * lw:5 * lw] = proj(5).astype(BF16)


def _inproj_odd_kernel(s_ref, mod_ref, g_ref, w_ref, cos_ref, sin_ref, u_ref, zb_ref, ub_ref, *, nlat, qscale, s5w):
    j = pl.program_id(0)
    hb = _norm_mod_in(s_ref, mod_ref, g_ref)
    aw = DIFF_HEADS * 2 * DIFF_DH
    u = _dot(hb, w_ref[:, 0:s5w])
    u_ref[:, pl.ds(pl.program_id(1), 1), :] = u[:, None, :]
    ub_ref[0] = u
    lat = j < nlat
    cos = jnp.where(lat, cos_ref[...], 1.0)
    sin = jnp.where(lat, sin_ref[...], 0.0)
    lane = lax.broadcasted_iota(jnp.int32, (TB, LANES), 1)
    first_half = (lane % DIFF_DH) < (DIFF_DH // 2)
    for sec, scale in ((0, qscale), (1, 1.0)):
        t = _dot(hb, w_ref[:, s5w + sec * aw:s5w + (sec + 1) * aw])
        for hd in range(DIFF_HEADS):
            th = t[:, hd * LANES:(hd + 1) * LANES]
            partner = jnp.where(first_half, pltpu.roll(th, LANES - DIFF_DH // 2, 1), pltpu.roll(th, DIFF_DH // 2, 1))
            th = (th * cos + partner * sin) * scale
            zb_ref[0, :, sec * aw + hd * LANES:sec * aw + (hd + 1) * LANES] = th.astype(BF16)
    zb_ref[0, :, 2 * aw:3 * aw] = _dot(hb, w_ref[:, s5w + 2 * aw:s5w + 3 * aw]).astype(BF16)


def _inproj(body, s, modtab, g, w, cos, sin, uw, zw, nlat, name, u_rows=False):
    b, l, d = s.shape
    nj = l // TB
    return pl.pallas_call(
        body,
        grid=(nj, b),
        in_specs=[pl.BlockSpec((1, TB, d), lambda j, i: (i, j, 0)),
                  pl.BlockSpec((1, 1, 6, d), lambda j, i: (i, jnp.where(j < nlat, 1, 0), 0, 0)),
                  pl.BlockSpec((1, d), lambda j, i: (0, 0)),
                  pl.BlockSpec(w.shape, lambda j, i: (0, 0)),
                  pl.BlockSpec((TB, LANES), lambda j, i: (jnp.minimum(j, nlat - 1), 0)),
                  pl.BlockSpec((TB, LANES), lambda j, i: (jnp.minimum(j, nlat - 1), 0))],
        out_specs=[pl.BlockSpec((TB, b, uw), lambda j, i: (j, 0, 0)),
                   pl.BlockSpec((1, TB, zw), lambda j, i: (i, j, 0))]
        + ([pl.BlockSpec((1, TB, uw), lambda j, i: (i, j, 0))] if u_rows else []),
        out_shape=[jax.ShapeDtypeStruct((l, b, uw), F32),
                   jax.ShapeDtypeStruct((b, l, zw), BF16)]
        + ([jax.ShapeDtypeStruct((b, l, uw), F32)] if u_rows else []),
        compiler_params=_cp("arbitrary", "arbitrary"),
        name=name,
    )(s, modtab, g, w, cos, sin)


def _fwd_chunk(s, nlat_c, nctx_c):
    return jnp.where(s < nctx_c, nlat_c + s, s - nctx_c)


def _bwd_chunk(s, nlat_c, nctx_c):
    return nlat_c + nctx_c - 1 - s


def _lru_kernel(uf_ref, ufl_ref, ufr_ref, ub_ref, ubl_ref, ubr_ref, cw_ref, cb_ref, gw_ref, gb_ref, lam_ref,
                of_ref, ob_ref, pad_scr, a_scr, b_scr, h_scr, o_scr, *, tt, nlat_c, nctx_c, nb):
    s = pl.program_id(0)
    nc = nlat_c + nctx_c

    @pl.when(s == 0)
    def _():
        h_scr[...] = jnp.zeros_like(h_scr)

    rows = tt * nb
    views = ((uf_ref, ufl_ref, ufr_ref, _fwd_chunk(s, nlat_c, nctx_c)),
             (ub_ref, ubl_ref, ubr_ref, _bwd_chunk(s, nlat_c, nctx_c)))
    for d, (u_ref, l_ref, r_ref, c) in enumerate(views):
        first = (c == 0) | (c == nlat_c)
        last = (c == nlat_c - 1) | (c == nc - 1)
        pad_scr[0:CONV_LEFT * nb] = jnp.where(first, 0.0, l_ref[...])
        pad_scr[CONV_LEFT * nb:CONV_LEFT * nb + rows] = u_ref[...]
        pad_scr[CONV_LEFT * nb + rows:(CONV_W - 1) * nb + rows] = jnp.where(last, 0.0, r_ref[...])
        uc = cb_ref[...] + pad_scr[0:rows] * cw_ref[0]
        for k in range(1, CONV_W):
            uc = uc + pad_scr[k * nb:k * nb + rows] * cw_ref[k]
        ucb = uc.astype(BF16)
        r = _sigmoid(_dot(ucb, gw_ref[d, 0]) + gb_ref[d, 0])
        i = _sigmoid(_dot(ucb, gw_ref[d, 1]) + gb_ref[d, 1])
        a = jnp.exp((-LRU_C * _softplus(-lam_ref[d])) * r)
        a_scr[d] = a
        b_scr[d] = jnp.sqrt(1.0 - a * a) * (i * uc)

    def body(t, carry):
        hf, hb = carry
        rf = pl.multiple_of(t * nb, nb)
        rb = pl.multiple_of((tt - 1 - t) * nb, nb)
        hf = a_scr[0, pl.ds(rf, nb), :] * hf + b_scr[0, pl.ds(rf, nb), :]
        hb = a_scr[1, pl.ds(rb, nb), :] * hb + b_scr[1, pl.ds(rb, nb), :]
        for g in range(ng):
            o_scr[0, g, pl.ds(rf, nb), :] = hf[:, g * LANES:(g + 1) * LANES]
            o_scr[1, g, pl.ds(rb, nb), :] = hb[:, g * LANES:(g + 1) * LANES]
        return hf, hb

    ng = o_scr.shape[1]
    hf, hb = lax.fori_loop(0, tt, body, (h_scr[0], h_scr[1]), unroll=8)
    h_scr[0] = hf
    h_scr[1] = hb
    for bi in range(nb):
        for g in range(ng):
            of_ref[bi, :, g * LANES:(g + 1) * LANES] = o_scr[0, g, pl.ds(bi, tt, stride=nb), :]
            ob_ref[bi, :, g * LANES:(g + 1) * LANES] = o_scr[1, g, pl.ds(bi, tt, stride=nb), :]


def _lru(u_tm, conv_w, conv_b, gate_w, gate_b, lam, nb, nlat, nctx):
    rows_total, c = u_tm.shape
    tt = LRU_TT
    nlat_c, nctx_c = nlat // tt, nctx // tt
    nc = nlat_c + nctx_c
    rows = tt * nb
    fwd = functools.partial(_fwd_chunk, nlat_c=nlat_c, nctx_c=nctx_c)
    bwd = functools.partial(_bwd_chunk, nlat_c=nlat_c, nctx_c=nctx_c)
    lrows, rrows = CONV_LEFT * nb, (CONV_W - 1 - CONV_LEFT) * nb
    nl, nr = rows // lrows, rows // rrows

    def cur(f):
        return pl.BlockSpec((rows, c), lambda s: (f(s), 0))

    def left(f):
        return pl.BlockSpec((lrows, c), lambda s: (jnp.maximum(f(s) * nl - 1, 0), 0))

    def right(f):
        return pl.BlockSpec((rrows, c), lambda s: (jnp.minimum((f(s) + 1) * nr, nc * nr - 1), 0))

    def whole(a):
        return pl.BlockSpec(a.shape, lambda s: (0,) * a.ndim)

    return pl.pallas_call(
        functools.partial(_lru_kernel, tt=tt, nlat_c=nlat_c, nctx_c=nctx_c, nb=nb),
        grid=(nc,),
        in_specs=[cur(fwd), left(fwd), right(fwd), cur(bwd), left(bwd), right(bwd),
                  whole(conv_w), whole(conv_b), whole(gate_w), whole(gate_b), whole(lam)],
        out_specs=[pl.BlockSpec((nb, tt, c), lambda s: (0, fwd(s), 0)),
                   pl.BlockSpec((nb, tt, c), lambda s: (0, bwd(s), 0))],
        out_shape=[jax.ShapeDtypeStruct((nb, rows_total // nb, c), F32)] * 2,
        scratch_shapes=[pltpu.VMEM((rows + (CONV_W - 1) * nb, c), F32),
                        pltpu.VMEM((2, rows, c), F32),
                        pltpu.VMEM((2, rows, c), F32),
                        pltpu.VMEM((2, nb, c), F32),
                        pltpu.VMEM((2, c // LANES, rows, LANES), F32)],
        compiler_params=_cp("arbitrary"),
        name="rglru",
    )(u_tm, u_tm, u_tm, u_tm, u_tm, u_tm, conv_w, conv_b, gate_w, gate_b, lam)


def _ret_kernel(lg_ref, qf_ref, kf_ref, vf_ref, qb_ref, kb_ref, vb_ref, of_ref, ob_ref, s_scr, d_scr, *, c):
    s = pl.program_id(1)
    ii = lax.broadcasted_iota(jnp.int32, (c, c), 0).astype(F32)
    jj = lax.broadcasted_iota(jnp.int32, (c, c), 1).astype(F32)

    @pl.when(s == 0)
    def _():
        s_scr[...] = jnp.zeros_like(s_scr)
        for d in range(2):
            diff = (ii - jj) if d == 0 else (jj - ii)
            for h in range(RET_HEADS):
                d_scr[d, h] = jnp.where(diff >= 0, jnp.exp(jnp.maximum(diff, 0.0) * lg_ref[d, h]), 0.0)

    ri = lax.broadcasted_iota(jnp.int32, (c, 1), 0).astype(F32)
    views = ((qf_ref, kf_ref, vf_ref, of_ref), (qb_ref, kb_ref, vb_ref, ob_ref))
    for d, (q_ref, k_ref, v_ref, o_ref) in enumerate(views):
        for h in range(RET_HEADS):
            lg = lg_ref[d, h]
            cols = slice(h * RET_D, (h + 1) * RET_D)
            q, k, v = q_ref[0, :, cols], k_ref[0, :, cols], v_ref[0, :, cols]
            sc = lax.dot_general(q, k, (((1,), (1,)), ((), ())), preferred_element_type=F32) * d_scr[d, h]
            inner = _dot(sc.astype(BF16), v)
            st = s_scr[d, h]
            qdec = jnp.exp(((ri + 1.0) if d == 0 else (c - ri)) * lg)
            cross = _dot(q, st.astype(BF16)) * qdec
            o_ref[0, :, cols] = inner + cross
            kdec = jnp.exp(((c - 1.0 - ri) if d == 0 else ri) * lg)
            kd = (k.astype(F32) * kdec).T.astype(BF16)
            gc = jnp.exp(jnp.zeros((RET_D, RET_D), F32) + c * lg)
            s_scr[d, h] = gc * st + _dot(kd, v)


def _retention(log_g, zb, nlat):
    b, l, _ = zb.shape
    nj = l // TB
    w = RET_HEADS * RET_D

    def fwd(s):
        return jnp.where(s < nj - nlat, nlat + s, s - (nj - nlat))

    def bwd(s):
        return nj - 1 - s

    def col(f, i):
        return pl.BlockSpec((1, TB, w), lambda bi, s: (bi, f(s), i))

    def out(f):
        return pl.BlockSpec((1, TB, w), lambda bi, s: (bi, f(s), 0))

    return pl.pallas_call(
        functools.partial(_ret_kernel, c=TB),
        grid=(b, nj),
        in_specs=[pl.BlockSpec(memory_space=pltpu.SMEM),
                  col(fwd, 1), col(fwd, 2), col(fwd, 3), col(bwd, 1), col(bwd, 2), col(bwd, 3)],
        out_specs=[out(fwd), out(bwd)],
        out_shape=[jax.ShapeDtypeStruct((b, l, w), F32)] * 2,
        scratch_shapes=[pltpu.VMEM((2, RET_HEADS, RET_D, RET_D), F32),
                        pltpu.VMEM((2, RET_HEADS, TB, TB), F32)],
        compiler_params=_cp("arbitrary", "arbitrary"),
        name="retention",
    )(log_g, zb, zb, zb, zb, zb, zb)


def _s5_kernel(uf_ref, ub_ref, bm_ref, ar_ref, ai_ref, cm_ref, yf_ref, yb_ref, bu_scr, h_scr, y_scr, *, tt, nb, ns):
    s = pl.program_id(0)

    @pl.when(s == 0)
    def _():
        h_scr[...] = jnp.zeros_like(h_scr)

    cg = 512
    for d, (u_ref, y_ref) in enumerate(((uf_ref, yf_ref), (ub_ref, yb_ref))):
        bu_scr[...] = _dot(u_ref[...].astype(BF16), bm_ref[d])
        for g in range(ns // cg):
            re = slice(g * cg, (g + 1) * cg)
            im = slice(ns + g * cg, ns + (g + 1) * cg)
            ar = ar_ref[d, :, re]
            ai = ai_ref[d, :, re]

            def body(i, carry, re=re, im=im, ar=ar, ai=ai, d=d):
                hr, hi = carry
                t = i if d == 0 else tt - 1 - i
                r0 = pl.multiple_of(t * nb, nb)
                nr = ar * hr - ai * hi + bu_scr[pl.ds(r0, nb), re]
                ni = ar * hi + ai * hr + bu_scr[pl.ds(r0, nb), im]
                bu_scr[pl.ds(r0, nb), re] = nr
                bu_scr[pl.ds(r0, nb), im] = ni
                return nr, ni

            hr, hi = lax.fori_loop(0, tt, body, (h_scr[d, :, re], h_scr[d, :, im]), unroll=4)
            h_scr[d, :, re] = hr
            h_scr[d, :, im] = hi
        y = _dot(bu_scr[...].astype(BF16), cm_ref[d])
        for g in range(y_scr.shape[0]):
            y_scr[g] = y[:, g * LANES:(g + 1) * LANES]
        for bi in range(nb):
            for g in range(y_scr.shape[0]):
                y_ref[bi, :, g * LANES:(g + 1) * LANES] = y_scr[g, pl.ds(bi, tt, stride=nb), :]


def _s5(u_tm, bmat, a_re, a_im, cmat, nb, nlat, nctx):
    rows_total, c = u_tm.shape
    tt = S5_TT
    nlat_c, nctx_c = nlat // tt, nctx // tt
    rows = tt * nb
    ns = a_re.shape[-1]
    fwd = functools.partial(_fwd_chunk, nlat_c=nlat_c, nctx_c=nctx_c)
    bwd = functools.partial(_bwd_chunk, nlat_c=nlat_c, nctx_c=nctx_c)

    def cur(f):
        return pl.BlockSpec((rows, c), lambda s: (f(s), 0))

    def whole(a):
        return pl.BlockSpec(a.shape, lambda s: (0,) * a.ndim)

    return pl.pallas_call(
        functools.partial(_s5_kernel, tt=tt, nb=nb, ns=ns),
        grid=(nlat_c + nctx_c,),
        in_specs=[cur(fwd), cur(bwd), whole(bmat), whole(a_re), whole(a_im), whole(cmat)],
        out_specs=[pl.BlockSpec((nb, tt, c), lambda s: (0, fwd(s), 0)),
                   pl.BlockSpec((nb, tt, c), lambda s: (0, bwd(s), 0))],
        out_shape=[jax.ShapeDtypeStruct((nb, rows_total // nb, c), F32)] * 2,
        scratch_shapes=[pltpu.VMEM((rows, 2 * ns), F32), pltpu.VMEM((2, nb, 2 * ns), F32),
                        pltpu.VMEM((c // LANES, rows, LANES), F32)],
        compiler_params=_cp("arbitrary"),
        name="s5",
    )(u_tm, u_tm, bmat, a_re, a_im, cmat)


def _attn_kernel(lam_ref, q_ref, k_ref, v_ref, gn_ref, o_ref, *, out_scale):
    tq, tk = q_ref.shape[1], k_ref.shape[1]
    q = q_ref[0]
    lane = lax.broadcasted_iota(jnp.int32, (tq, LANES), 1)
    zero = jnp.zeros_like(q)
    qm = (jnp.where(lane < DIFF_DH, q, zero), jnp.where(lane >= DIFF_DH, q, zero))
    kc = k_ref[0]
    va = jnp.concatenate([v_ref[0], jnp.ones((tk, LANES), BF16)], axis=1)
    rb = min(ATTN_RB, tq)
    for r in range(tq // rb):
        rows = slice(r * rb, (r + 1) * rb)
        outs = []
        for m in range(2):
            sc = lax.dot_general(qm[m][rows], kc, (((1,), (1,)), ((), ())), preferred_element_type=F32)
            p = jnp.exp2(sc - jnp.max(sc, axis=1, keepdims=True)).astype(BF16)
            a = _dot(p, va)
            outs.append(a[:, :LANES] / a[:, LANES:LANES + 1])
        att = outs[0] - lam_ref[0] * outs[1]
        att = att * lax.rsqrt(jnp.mean(att * att, axis=-1, keepdims=True) + EPS) * (gn_ref[0] * out_scale)
        o_ref[0, rows] = att.astype(BF16)


def _attention(lam, zb, gn, nlat, out_scale):
    b, l, _ = zb.shape
    h = DIFF_HEADS
    nctx = l - nlat
    tq = ATTN_TQ if nlat % ATTN_TQ == 0 else TB
    gn3 = gn.reshape(h, 1, LANES)

    def call(tq_, nq, kv_rows, q0, kv0, name):
        return pl.pallas_call(
            functools.partial(_attn_kernel, out_scale=out_scale),
            grid=(b, h, nq),
            in_specs=[pl.BlockSpec(memory_space=pltpu.SMEM),
                      pl.BlockSpec((1, tq_, LANES), lambda bi, hi, qi: (bi, q0 + qi, hi)),
                      pl.BlockSpec((1, kv_rows, LANES), lambda bi, hi, qi: (bi, kv0, h + hi)),
                      pl.BlockSpec((1, kv_rows, LANES), lambda bi, hi, qi: (bi, kv0, 2 * h + hi)),
                      pl.BlockSpec((1, 1, LANES), lambda bi, hi, qi: (hi, 0, 0))],
            out_specs=pl.BlockSpec((1, tq_, LANES), lambda bi, hi, qi: (bi, qi, hi)),
            out_shape=jax.ShapeDtypeStruct((b, nq * tq_, h * LANES), BF16),
            compiler_params=_cp("arbitrary", "arbitrary", "arbitrary"),
            name=name,
        )(lam, zb, zb, zb, gn3)

    att_x = call(tq, nlat // tq, l, 0, 0, "diffattn_latent")
    att_c = call(nctx, 1, nctx, nlat // nctx, nlat // nctx, "diffattn_context")
    return jnp.concatenate([att_x, att_c], axis=1)


def _out_tail(parts, w_ref, s_ref, mod, ng_ref, xo_ref):
    y, r0 = None, 0
    for p in parts:
        n = p.shape[1]
        t = _dot(p.astype(BF16), w_ref[r0:r0 + n, :])
        y = t if y is None else y + t
        r0 += n
    xn = s_ref[0] + mod[2:3] * _rms(y, ng_ref[1:2])
    xo_ref[0] = xn
    return _rms(xn, ng_ref[2:3]) * (1.0 + mod[4:5]) + mod[3:4]


def _outproj_even_kernel(lf_ref, lb_ref, zg_ref, zo_ref, rf_ref, rb_ref, gn_ref, w_ref, s_ref, mod_ref, ng_ref,
                         xo_ref, h2_ref):
    parts = [(lf_ref[0] + lb_ref[0]) * _gelu(zg_ref[0].astype(F32))]
    r = rf_ref[0] + rb_ref[0]
    og = _silu(zo_ref[0].astype(F32))
    for h in range(RET_HEADS):
        cols = slice(h * RET_D, (h + 1) * RET_D)
        y = r[:, cols]
        y = y - jnp.mean(y, axis=-1, keepdims=True)
        y = y * lax.rsqrt(jnp.mean(y * y, axis=-1, keepdims=True) + EPS) * gn_ref[:, cols]
        parts.append(y * og[:, cols])
    h2 = _out_tail(parts, w_ref, s_ref, mod_ref[0, 0], ng_ref, xo_ref)
    h2_ref[0] = h2.astype(BF16)


def _outproj_odd_kernel(yf_ref, yb_ref, u_ref, att_ref, d_ref, gw_ref, gb_ref, w_ref, s_ref, mod_ref, ng_ref,
                        rw_ref, rb_ref, xo_ref, h2_ref, gates_ref):
    z = _gelu(yf_ref[0] + yb_ref[0] + d_ref[...] * u_ref[0])
    s5o = z * _sigmoid(_dot(z.astype(BF16), gw_ref[...]) + gb_ref[...])
    h2 = _out_tail([s5o, att_ref[0]], w_ref, s_ref, mod_ref[0, 0], ng_ref, xo_ref)
    d = h2.shape[1]
    h2_ref[0, :, :d] = h2.astype(BF16)
    logits = _dot3(h2, rw_ref[...]) + rb_ref[...]
    lane = lax.broadcasted_iota(jnp.int32, logits.shape, 1).astype(F32)
    m1 = jnp.max(logits, axis=1, keepdims=True)
    i1 = jnp.min(jnp.where(logits == m1, lane, float(LANES)), axis=1, keepdims=True)
    rest = jnp.where(lane == i1, 2.0 * NEG, logits)
    m2 = jnp.max(rest, axis=1, keepdims=True)
    i2 = jnp.min(jnp.where(rest == m2, lane, float(LANES)), axis=1, keepdims=True)
    e = jnp.exp(m2 - m1)
    p1 = 1.0 / (1.0 + e)
    gates = jnp.where(lane == i1, p1, 0.0) + jnp.where(lane == i2, e * p1, 0.0)
    ids = jnp.where(lane == float(N_EXPERTS), i1, 0.0) + jnp.where(lane == float(N_EXPERTS + 1), i2, 0.0)
    gates_ref[0] = gates + ids
    ghi = gates.astype(BF16).astype(F32)
    glo = (gates - ghi).astype(BF16).astype(F32)
    h2_ref[0, :, d:] = (ghi + pltpu.roll(glo, N_EXPERTS, 1)).astype(BF16)


def _whole2(a):
    return pl.BlockSpec(a.shape, lambda i, j: (0,) * a.ndim)


def _mod_spec(d, nlat_t):
    return pl.BlockSpec((1, 1, 6, d), lambda i, j: (i, jnp.where(j < nlat_t, 1, 0), 0, 0))


def _outproj_even(lf, lb, zb, rf, rb, gn, w, s, modtab, ng, nlat_t):
    b, l, d = s.shape
    cw = RET_HEADS * RET_D
    tm = pl.BlockSpec((1, TB, cw), lambda i, j: (i, j, 0))
    row = pl.BlockSpec((1, TB, d), lambda i, j: (i, j, 0))
    return pl.pallas_call(
        _outproj_even_kernel,
        grid=(b, l // TB),
        in_specs=[tm, tm,
                  pl.BlockSpec((1, TB, cw), lambda i, j: (i, j, 0)),
                  pl.BlockSpec((1, TB, cw), lambda i, j: (i, j, 4)),
                  pl.BlockSpec((1, TB, cw), lambda i, j: (i, j, 0)),
                  pl.BlockSpec((1, TB, cw), lambda i, j: (i, j, 0)),
                  _whole2(gn), _whole2(w), row, _mod_spec(d, nlat_t), _whole2(ng)],
        out_specs=[row, row],
        out_shape=[jax.ShapeDtypeStruct((b, l, d), F32), jax.ShapeDtypeStruct((b, l, d), BF16)],
        compiler_params=_cp("arbitrary", "arbitrary"),
        name="outproj_even",
    )(lf, lb, zb, zb, rf, rb, gn, w, s, modtab, ng)


def _outproj_odd(yf, yb, u, att, dskip, glu_w, glu_b, w, s, modtab, ng, rw, rb, nlat_t):
    b, l, d = s.shape
    sw = dskip.shape[1]
    aw = att.shape[2]
    tm = pl.BlockSpec((1, TB, sw), lambda i, j: (i, j, 0))
    row = pl.BlockSpec((1, TB, d), lambda i, j: (i, j, 0))
    return pl.pallas_call(
        _outproj_odd_kernel,
        grid=(b, l // TB),
        in_specs=[tm, tm, tm,
                  pl.BlockSpec((1, TB, aw), lambda i, j: (i, j, 0)),
                  _whole2(dskip), _whole2(glu_w), _whole2(glu_b), _whole2(w), row, _mod_spec(d, nlat_t),
                  _whole2(ng), _whole2(rw), _whole2(rb)],
        out_specs=[row, pl.BlockSpec((1, TB, d + LANES), lambda i, j: (i, j, 0)),
                   pl.BlockSpec((1, TB, LANES), lambda i, j: (i, j, 0))],
        out_shape=[jax.ShapeDtypeStruct((b, l, d), F32), jax.ShapeDtypeStruct((b, l, d + LANES), BF16),
                   jax.ShapeDtypeStruct((b, l, LANES), F32)],
        compiler_params=_cp("arbitrary", "arbitrary"),
        name="outproj_odd",
    )(yf, yb, u, att, dskip, glu_w, glu_b, w, s, modtab, ng, rw, rb)


def _ffn_kernel(x_ref, wg_ref, wu_ref, wd_ref, s_ref, mod_ref, g_ref, o_ref, acc_ref, *, tpb, seq):
    i, f = pl.program_id(0), pl.program_id(1)

    @pl.when(f == 0)
    def _():
        acc_ref[...] = jnp.zeros_like(acc_ref)

    x = x_ref[...]
    act = _silu(_dot(x, wg_ref[...])) * _dot(x, wu_ref[...])
    acc_ref[...] += _dot(act.astype(BF16), wd_ref[...])

    @pl.when(f == pl.num_programs(1) - 1)
    def _():
        tm = acc_ref.shape[0]
        row = lax.rem(i, tpb) * tm + lax.broadcasted_iota(jnp.int32, (tm, 1), 0)
        mod = mod_ref[0]
        gate = jnp.where(row < seq, mod[1, 5:6], mod[0, 5:6])
        o_ref[...] = s_ref[...] + gate * _rms(acc_ref[...], g_ref[...])


def _ffn(x, wg, wu, wd, s, modtab, g, seq):
    b, l, d = s.shape
    n = b * l
    ff = wg.shape[1]
    tm, tf = FFN_TM, FFN_TF
    tpb = l // tm
    rows = pl.BlockSpec((tm, d), lambda i, f: (i, 0))
    out = pl.pallas_call(
        functools.partial(_ffn_kernel, tpb=tpb, seq=seq),
        grid=(n // tm, ff // tf),
        in_specs=[rows,
                  pl.BlockSpec((d, tf), lambda i, f: (0, f)),
                  pl.BlockSpec((d, tf), lambda i, f: (0, f)),
                  pl.BlockSpec((tf, d), lambda i, f: (f, 0)),
                  rows,
                  pl.BlockSpec((1, 2, 6, d), lambda i, f: (i // tpb, 0, 0, 0)),
                  pl.BlockSpec(g.shape, lambda i, f: (0, 0))],
        out_specs=rows,
        out_shape=jax.ShapeDtypeStruct((n, d), F32),
        scratch_shapes=[pltpu.VMEM((tm, d), F32)],
        compiler_params=_cp("arbitrary", "arbitrary"),
        name="dense_ffn",
    )(x, wg, wu, wd, s.reshape(n, d), modtab, g)
    return out.reshape(b, l, d)


def _moe_plan(route, n_tiles):
    n = route.shape[0]
    nb = n // TB
    ne = N_EXPERTS
    e1 = route[:, ne].astype(jnp.int32)
    e2 = route[:, ne + 1].astype(jnp.int32)
    ex = jnp.arange(ne, dtype=jnp.int32)
    sel = ((e1[:, None] == ex) | (e2[:, None] == ex)).astype(jnp.int32)
    incl = jnp.cumsum(sel, axis=0)
    count = incl[-1]
    gsize = (count + MOE_TM - 1) // MOE_TM * MOE_TM
    gend = jnp.cumsum(gsize)
    goff = gend - gsize
    posm = goff[None, :] + incl - sel
    pos1 = jnp.sum(jnp.where(e1[:, None] == ex, posm, 0), axis=1)
    pos2 = jnp.sum(jnp.where(e2[:, None] == ex, posm, 0), axis=1)
    pos_cols = jnp.stack([pos1, pos2], axis=1)
    pos_rows = jnp.stack([pos1.reshape(nb, TB), pos2.reshape(nb, TB)], axis=1)
    start = posm[::TB]
    last = (goff + count)[None]
    end = jnp.concatenate([start[1:], last], axis=0)
    r0 = jnp.arange(n_tiles, dtype=jnp.int32) * MOE_TM
    tile_valid = (r0 < gend[-1]).astype(jnp.int32)
    tile_exp = jnp.minimum(jnp.sum((r0[:, None] >= gend[None, :]).astype(jnp.int32), axis=1), ne - 1)
    st, en = start.T[tile_exp], end.T[tile_exp]
    lim = jnp.minimum(r0 + MOE_TM, last[0][tile_exp])
    tb_lo = jnp.clip(jnp.sum((en <= r0[:, None]).astype(jnp.int32), axis=1), 0, nb - 1)
    tb_hi = jnp.clip(jnp.sum((st < lim[:, None]).astype(jnp.int32), axis=1) - 1, 0, nb - 1)
    a0 = start // MOE_ALIGN * MOE_ALIGN
    lo_off = start - a0
    hi_off = lo_off + (end - start)
    k = jnp.arange(MOE_SUBS_PER_WIN, dtype=jnp.int32) * MOE_SUB
    c_valid = ((end > start)[..., None] & (hi_off[..., None] > k)).reshape(nb, -1)
    c_base = (a0[..., None] + k).reshape(nb, -1)
    c_lo = jnp.clip(lo_off[..., None] - k, 0, MOE_SUB).reshape(nb, -1)
    c_hi = jnp.clip(hi_off[..., None] - k, 0, MOE_SUB).reshape(nb, -1)
    slot = jnp.cumsum(c_valid.astype(jnp.int32), axis=1) - 1
    put = (slot[..., None] == jnp.arange(MOE_SLOTS, dtype=jnp.int32)) & c_valid[..., None]

    def place(v):
        return jnp.sum(jnp.where(put, v[..., None], 0), axis=1).reshape(-1)

    return (tile_exp, tile_valid, tb_lo, tb_hi, pos_rows, pos_cols, place(c_base), place(c_lo), place(c_hi))


def _moe_kernel(te_ref, tv_ref, lo_ref, hi_ref, pos_ref, h_hbm, wg_ref, wu_ref, wd_ref, y_ref, xbuf, xs_ref, sem,
                *, n_tiles, tf):
    i = pl.program_id(0)
    d, ff = wg_ref.shape[1], wg_ref.shape[2]
    grp = MOE_GROUP
    nb = pos_ref.shape[0]
    slot = lax.rem(i, 2)

    def first_blk(t, g):
        return jnp.minimum(lo_ref[t] + g * grp, nb - grp)

    def grp_copy(fb, sl):
        return pltpu.make_async_copy(h_hbm.at[pl.ds(pl.multiple_of(fb * TB, TB), grp * TB)], xbuf.at[sl], sem.at[sl])

    valid = tv_ref[i] == 1

    @pl.when((i == 0) & valid)
    def _():
        grp_copy(first_blk(0, 0), 0).start()

    @pl.when(valid)
    def _():
        lo, hi = lo_ref[i], hi_ref[i]
        rid = i * MOE_TM + lax.broadcasted_iota(jnp.int32, (MOE_TM, TB), 0)

        def gather(g):
            fb = first_blk(i, g)
            pieces = []
            for k in range(grp):
                tb = fb + k
                p = jnp.where((tb >= lo + g * grp) & (tb <= hi), pos_ref[tb], -1)
                pieces.append(jnp.where(rid == p[0:1], 1.0, 0.0) + jnp.where(rid == p[1:2], 1.0, 0.0))
            return _dot(jnp.concatenate(pieces, axis=1).astype(BF16), xbuf[slot])

        grp_copy(first_blk(i, 0), slot).wait()
        xs_ref[...] = gather(0)

        def more(g, c):
            cp = grp_copy(first_blk(i, g), slot)
            cp.start()
            cp.wait()
            xs_ref[...] += gather(g)
            return c

        lax.fori_loop(1, (hi - lo) // grp + 1, more, 0)
        nxt = jnp.minimum(i + 1, n_tiles - 1)

        @pl.when((i + 1 < n_tiles) & (tv_ref[nxt] == 1))
        def _():
            grp_copy(first_blk(nxt, 0), 1 - slot).start()

        xs = xs_ref[...]
        x = xs[:, :d].astype(BF16)
        ext = xs[:, d:]
        lane = lax.broadcasted_iota(jnp.int32, ext.shape, 1)
        e_t = te_ref[i]
        gate = jnp.sum(jnp.where((lane == e_t) | (lane == e_t + N_EXPERTS), ext, 0.0), axis=1, keepdims=True)
        acc = None
        for f in range(ff // tf):
            cols = slice(f * tf, (f + 1) * tf)
            act = _silu(_dot(x, wg_ref[0, :, cols])) * _dot(x, wu_ref[0, :, cols])
            t = _dot(act.astype(BF16), wd_ref[0, cols, :])
            acc = t if acc is None else acc + t
        y_ref[...] = (acc * gate).astype(BF16)

    @pl.when(jnp.logical_not(valid))
    def _():
        y_ref[...] = jnp.zeros_like(y_ref)


def _moe(h2x, plan, wg, wu, wd, n_tiles):
    n, dx = h2x.shape
    ne, d, ff = wg.shape
    tile_exp, tile_valid, tb_lo, tb_hi, pos_rows = plan[:5]

    def wspec(shape):
        return pl.BlockSpec(shape, lambda i, te, tv, lo, hi: (te[i], 0, 0), pipeline_mode=pl.Buffered(1))

    return pl.pallas_call(
        functools.partial(_moe_kernel, n_tiles=n_tiles, tf=FFN_TF),
        grid_spec=pltpu.PrefetchScalarGridSpec(
            num_scalar_prefetch=4,
            grid=(n_tiles,),
            in_specs=[pl.BlockSpec(pos_rows.shape, lambda i, te, tv, lo, hi: (0, 0, 0)),
                      pl.BlockSpec(memory_space=pl.ANY),
                      wspec((1, d, ff)), wspec((1, d, ff)), wspec((1, ff, d))],
            out_specs=pl.BlockSpec((MOE_TM, d), lambda i, te, tv, lo, hi: (i, 0)),
            scratch_shapes=[pltpu.VMEM((2, MOE_GROUP * TB, dx), BF16), pltpu.VMEM((MOE_TM, dx), F32),
                            pltpu.SemaphoreType.DMA((2,))]),
        out_shape=jax.ShapeDtypeStruct((n_tiles * MOE_TM, d), BF16),
        compiler_params=_cp("arbitrary"),
        name="moe_ffn",
    )(tile_exp, tile_valid, tb_lo, tb_hi, pos_rows, h2x, wg, wu, wd)


def _combine_kernel(base_ref, lo_ref, hi_ref, s_ref, pos_ref, mod_ref, g_ref, ys_hbm, o_ref, buf, sem, *, nj, nsteps):
    step = pl.program_id(0) * nj + pl.program_id(1)
    slot = lax.rem(step, 2)

    def win_copy(t, k, sl):
        a = pl.multiple_of(base_ref[t * MOE_SLOTS + k], MOE_ALIGN)
        return pltpu.make_async_copy(ys_hbm.at[pl.ds(a, MOE_SUB)], buf.at[sl, pl.ds(k * MOE_SUB, MOE_SUB)],
                                     sem.at[sl, k])

    @pl.when(step == 0)
    def _():
        for k in range(MOE_SLOTS):
            win_copy(0, k, 0).start()

    @pl.when(step + 1 < nsteps)
    def _():
        for k in range(MOE_SLOTS):
            win_copy(step + 1, k, 1 - slot).start()

    pos = pos_ref[...]
    pos1 = jnp.broadcast_to(pos[:, 0:1], (TB, MOE_SUB))
    pos2 = jnp.broadcast_to(pos[:, 1:2], (TB, MOE_SUB))
    lane = lax.broadcasted_iota(jnp.int32, (1, MOE_SUB), 1)
    acc = jnp.zeros(s_ref.shape[1:], F32)
    per_dot = 2 * LANES // MOE_SUB
    for g in range(MOE_SLOTS // per_dot):
        pieces = []
        for k in range(g * per_dot, (g + 1) * per_dot):
            win_copy(step, k, slot).wait()
            j = step * MOE_SLOTS + k
            held = jnp.where((lane >= lo_ref[j]) & (lane < hi_ref[j]), lane + base_ref[j], -1)
            pieces.append(jnp.where(pos1 == held, 1.0, 0.0) + jnp.where(pos2 == held, 1.0, 0.0))
        oh = jnp.concatenate(pieces, axis=1).astype(BF16)
        rows = slice(g * per_dot * MOE_SUB, (g + 1) * per_dot * MOE_SUB)
        acc = acc + _dot(oh, buf[slot, rows])
    o_ref[0] = s_ref[0] + mod_ref[0, 0][5:6] * _rms(acc, g_ref[...])


def _moe_combine(s, ys, plan, modtab, g, nlat_t):
    b, l, d = s.shape
    nj = l // TB
    pos_cols, base, lo_off, hi_off = plan[5:]
    row = pl.BlockSpec((1, TB, d), lambda i, j, *_: (i, j, 0))
    return pl.pallas_call(
        functools.partial(_combine_kernel, nj=nj, nsteps=b * nj),
        grid_spec=pltpu.PrefetchScalarGridSpec(
            num_scalar_prefetch=3,
            grid=(b, nj),
            in_specs=[row,
                      pl.BlockSpec((TB, 2), lambda i, j, *_: (i * nj + j, 0)),
                      pl.BlockSpec((1, 1, 6, d), lambda i, j, *_: (i, jnp.where(j < nlat_t, 1, 0), 0, 0)),
                      pl.BlockSpec(g.shape, lambda i, j, *_: (0, 0)),
                      pl.BlockSpec(memory_space=pl.ANY)],
            out_specs=row,
            scratch_shapes=[pltpu.VMEM((2, MOE_SLOTS * MOE_SUB, d), BF16),
                            pltpu.SemaphoreType.DMA((2, MOE_SLOTS))]),
        out_shape=jax.ShapeDtypeStruct((b, l, d), F32),
        compiler_params=_cp("arbitrary", "arbitrary"),
        name="moe_combine",
    )(base, lo_off, hi_off, s, pos_cols, modtab, g, ys)


def _rope_tables(seq, dim):
    t = jnp.arange(seq)
    rows = (t // GRID_W).astype(F32)
    cols = (t % GRID_W).astype(F32)
    quarter = dim // 4
    inv = ROPE_BASE ** (-jnp.arange(quarter, dtype=F32) / quarter)
    ang = jnp.concatenate([rows[:, None] * inv, cols[:, None] * inv], axis=-1)
    c, s = jnp.cos(ang), jnp.sin(ang)
    rep = LANES // dim
    return jnp.tile(jnp.concatenate([c, c], -1), (1, rep)), jnp.tile(jnp.concatenate([-s, s], -1), (1, rep))


def _block_diag(x):
    g, r, c = x.shape[-3:]
    y = jnp.einsum('...grc,gh->...grhc', x, jnp.eye(g, dtype=x.dtype))
    return y.reshape(x.shape[:-3] + (g * r, g * c))


def _s5_params(lam_re, lam_im, log_dt, b_re, b_im, c_re, c_im, nb):
    lam = lax.complex(lam_re, lam_im)
    a_bar = jnp.exp(lam * jnp.exp(log_dt)[..., None])
    b_bar = ((a_bar - 1.0) / lam)[..., None] * lax.complex(b_re, b_im)
    bt = jnp.swapaxes(b_bar, -1, -2)
    bmat = jnp.concatenate([_block_diag(jnp.real(bt)), _block_diag(jnp.imag(bt))], axis=-1).astype(BF16)
    cmat = jnp.concatenate([_block_diag(jnp.swapaxes(c_re, -1, -2)),
                            -_block_diag(jnp.swapaxes(c_im, -1, -2))], axis=-2).astype(BF16)
    ns = a_bar.shape[1] * a_bar.shape[2]
    a_re = jnp.broadcast_to(jnp.real(a_bar).reshape(2, 1, ns), (2, nb, ns))
    a_im = jnp.broadcast_to(jnp.imag(a_bar).reshape(2, 1, ns), (2, nb, ns))
    return bmat, a_re, a_im, cmat


def kernel(x, c, ctx, c_ctx, mod_w, mod_b, norm_g, ev_w_in, ev_conv_w, ev_conv_b, lru_gate_w, lru_gate_b, lru_lam, ret_theta, ret_gn, ev_w_out, ffd_w_gate, ffd_w_up, ffd_w_down, od_w_in, s5_lam_re, s5_lam_im, s5_log_dt, s5_b_re, s5_b_im, s5_c_re, s5_c_im, s5_d, s5_glu_w, s5_glu_b, diff_lam, diff_gn, od_w_out, moe_router_w, moe_router_b, moe_w_gate, moe_w_up, moe_w_down):
    b, seq, d = x.shape
    nctx = ctx.shape[1]
    l = seq + nctx
    n = b * l
    depth = mod_w.shape[0]
    nlat_t = seq // TB
    assert b == SUBLANES and seq % TB == 0 and nctx % TB == 0 and seq % nctx == 0 and l % FFN_TM == 0

    s = jnp.concatenate([x, ctx], axis=1)
    cc = jnp.concatenate([c, c_ctx[None], jnp.zeros((2 * SUBLANES - b - 1, d), F32)], axis=0)
    mt = _modtab(cc, mod_w, mod_b).reshape(depth, 2 * SUBLANES, 6, d)
    modtab = jnp.stack([jnp.broadcast_to(mt[:, b:b + 1], (depth, b, 6, d)), mt[:, :b]], axis=2)

    cos_r, sin_r = _rope_tables(seq, RET_D)
    cos_d, sin_d = _rope_tables(seq, DIFF_DH)

    for layer in range(depth):
        i = layer // 2
        mtab = modtab[layer]
        ng = norm_g[layer]
        if layer % 2 == 0:
            lw = ev_conv_w.shape[-1]
            u3d, zb = _inproj(functools.partial(_inproj_even_kernel, nlat=nlat_t, kscale=RET_D ** -0.5),
                              s, mtab, ng[0:1], ev_w_in[i].astype(BF16), cos_r, sin_r, lw, 5 * lw, nlat_t,
                              "inproj_even")
            gw = _block_diag(lru_gate_w[i]).astype(BF16)
            lf, lb = _lru(u3d.reshape(l * b, lw), ev_conv_w[i].reshape(CONV_W, 1, lw), ev_conv_b[i].reshape(1, lw),
                          gw, lru_gate_b[i].reshape(2, 2, 1, lw), lru_lam[i].reshape(2, 1, lw), b, seq, nctx)
            rf, rb = _retention(jax.nn.log_sigmoid(ret_theta[i].astype(F32)), zb, nlat_t)
            s, h2 = _outproj_even(lf, lb, zb, rf, rb, ret_gn[i].reshape(1, -1),
                                  ev_w_out[i].astype(BF16), s, mtab, ng, nlat_t)
            s = _ffn(h2.reshape(n, d), ffd_w_gate[i].astype(BF16), ffd_w_up[i].astype(BF16),
                     ffd_w_down[i].astype(BF16), s, mtab, ng[3:4], seq)
        else:
            sw = s5_d.shape[-1]
            aw = DIFF_HEADS * 2 * DIFF_DH
            lam_init = 0.8 - 0.6 * math.exp(-0.3 * layer)
            u3d, zb, ub = _inproj(functools.partial(_inproj_odd_kernel, nlat=nlat_t, qscale=DIFF_DH ** -0.5 * math.log2(math.e), s5w=sw),
                                  s, mtab, ng[0:1], od_w_in[i].astype(BF16), cos_d, sin_d, sw, 3 * aw, nlat_t,
                                  "inproj_odd", u_rows=True)
            bmat, a_re, a_im, cmat = _s5_params(s5_lam_re[i], s5_lam_im[i], s5_log_dt[i], s5_b_re[i], s5_b_im[i],
                                                s5_c_re[i], s5_c_im[i], b)
            yf, yb = _s5(u3d.reshape(l * b, sw), bmat, a_re, a_im, cmat, b, seq, nctx)
            lp = diff_lam[i].astype(F32)
            lam = (jnp.exp(jnp.sum(lp[0] * lp[1])) - jnp.exp(jnp.sum(lp[2] * lp[3])) + lam_init).reshape(1)
            att = _attention(lam, zb, diff_gn[i], seq, 1.0 - lam_init)
            rw = jnp.zeros((d, LANES), F32).at[:, :N_EXPERTS].set(moe_router_w[i])
            rbias = jnp.full((1, LANES), NEG, F32).at[0, :N_EXPERTS].set(moe_router_b[i])
            s, h2x, gates = _outproj_odd(yf, yb, ub, att,
                                        s5_d[i].reshape(1, sw), s5_glu_w[i].astype(BF16), s5_glu_b[i].reshape(1, sw),
                                        od_w_out[i].astype(BF16), s, mtab, ng, rw, rbias, nlat_t)
            n_tiles = (2 * n) // MOE_TM + N_EXPERTS + 2
            plan = _moe_plan(gates.reshape(n, LANES), n_tiles)
            ys = _moe(h2x.reshape(n, d + LANES), plan, moe_w_gate[i].astype(BF16), moe_w_up[i].astype(BF16),
                      moe_w_down[i].astype(BF16), n_tiles)
            s = _moe_combine(s, ys, plan, mtab, ng[3:4], nlat_t)
    return s[:, :seq]
```

```python
import functools
import math

import jax
import jax.numpy as jnp
from jax import lax
from jax.experimental import pallas as pl
from jax.experimental.pallas import tpu as pltpu

F32 = jnp.float32
BF16 = jnp.bfloat16

EPS = 1e-6
ROPE_BASE = 10000.0
GRID_W = 64
LRU_C = 8.0
LRU_BLOCKS = 8
CONV_W = 4
CONV_LEFT = 2
RET_HEADS = 4
RET_D = 128
S5_GROUP = 16
S5_STATE = 64
DIFF_HEADS = 6
DIFF_DH = 64
N_EXPERTS = 8
LANES = 128
SUBLANES = 8

TB = 256
LRU_TT = 64
S5_TT = 32
ATTN_TQ = 1024
ATTN_RB = 128
FFN_TM = 768
FFN_TF = 512
CAST_BLOCK_ELEMS = 1024 * 1024
MOE_TM = 256
MOE_ALIGN = 16
MOE_SUB = 128
MOE_SUBS_PER_WIN = (TB + MOE_ALIGN - 1 + MOE_SUB - 1) // MOE_SUB
MOE_SLOTS = (N_EXPERTS * (MOE_ALIGN - 1 + MOE_SUB - 1) + 2 * TB) // MOE_SUB
MOE_GROUP = 6
VMEM_LIMIT = 56 * 1024 * 1024
NEG = -1e30


def _cp(*sem):
    return pltpu.CompilerParams(dimension_semantics=sem, vmem_limit_bytes=VMEM_LIMIT)


def _dot(a, b):
    return jnp.dot(a, b, preferred_element_type=F32)


def _split(x):
    hi = x.astype(BF16)
    lo = (x - hi.astype(F32)).astype(BF16)
    return hi, lo


def _dot3(a, w):
    ah, al = _split(a)
    wh, wl = _split(w)
    return _dot(ah, wh) + (_dot(ah, wl) + _dot(al, wh))


def _rms(x, g):
    return x * lax.rsqrt(jnp.mean(x * x, axis=-1, keepdims=True) + EPS) * g


def _sigmoid(x):
    return 0.5 * jnp.tanh(0.5 * x) + 0.5


def _silu(x):
    return x * _sigmoid(x)


def _gelu(x):
    return 0.5 * x * (1.0 + jnp.tanh(math.sqrt(2.0 / math.pi) * (x + 0.044715 * (x * x * x))))


def _softplus(x):
    return jnp.maximum(x, 0.0) + jnp.log(1.0 + jnp.exp(-jnp.abs(x)))


def _modtab_kernel(c_ref, w_ref, b_ref, o_ref):
    c = c_ref[...]
    o_ref[0] = _dot3(_silu(c), w_ref[0]) + b_ref[0]


def _modtab(cc, mod_w, mod_b):
    depth, d, n = mod_w.shape
    tn = 1536
    return pl.pallas_call(
        _modtab_kernel,
        grid=(depth, n // tn),
        in_specs=[pl.BlockSpec((cc.shape[0], d), lambda l, j: (0, 0)),
                  pl.BlockSpec((1, d, tn), lambda l, j: (l, 0, j)),
                  pl.BlockSpec((1, 1, tn), lambda l, j: (l, 0, j))],
        out_specs=pl.BlockSpec((1, cc.shape[0], tn), lambda l, j: (l, 0, j)),
        out_shape=jax.ShapeDtypeStruct((depth, cc.shape[0], n), F32),
        compiler_params=_cp("arbitrary", "arbitrary"),
        name="modtab",
    )(cc, mod_w, mod_b.reshape(depth, 1, n))


def _norm_mod_in(s_ref, mod_ref, g_ref):
    mod = mod_ref[0, 0]
    h = _rms(s_ref[0], g_ref[...]) * (1.0 + mod[1:2]) + mod[0:1]
    return h.astype(BF16)


def _inproj_even_kernel(s_ref, mod_ref, g_ref, w_ref, cos_ref, sin_ref, u_ref, zb_ref, *, nlat, kscale):
    j = pl.program_id(0)
    hb = _norm_mod_in(s_ref, mod_ref, g_ref)
    lw = RET_HEADS * RET_D

    def proj(i):
        return _dot(hb, w_ref[:, i * lw:(i + 1) * lw])

    u_ref[:, pl.ds(pl.program_id(1), 1), :] = proj(0)[:, None, :]
    zb_ref[0, :, 0:lw] = proj(1).astype(BF16)
    lat = j < nlat
    cos = jnp.where(lat, cos_ref[...], 1.0)
    sin = jnp.where(lat, sin_ref[...], 0.0)
    for sec, scale in ((0, 1.0), (1, kscale)):
        t = proj(2 + sec)
        for hd in range(RET_HEADS):
            th = t[:, hd * RET_D:(hd + 1) * RET_D]
            th = (th * cos + pltpu.roll(th, RET_D // 2, 1) * sin) * scale
            zb_ref[0, :, (1 + sec) * lw + hd * RET_D:(1 + sec) * lw + (hd + 1) * RET_D] = th.astype(BF16)
    zb_ref[0, :, 3 * lw:4 * lw] = proj(4).astype(BF16)
    zb_ref[0, :, 4 * lw:5 * lw] = proj(5).astype(BF16)


def _inproj_odd_kernel(s_ref, mod_ref, g_ref, w_ref, cos_ref, sin_ref, u_ref, zb_ref, ub_ref, *, nlat, qscale, s5w):
    j = pl.program_id(0)
    hb = _norm_mod_in(s_ref, mod_ref, g_ref)
    aw = DIFF_HEADS * 2 * DIFF_DH
    u = _dot(hb, w_ref[:, 0:s5w])
    u_ref[:, pl.ds(pl.program_id(1), 1), :] = u[:, None, :]
    ub_ref[0] = u
    lat = j < nlat
    cos = jnp.where(lat, cos_ref[...], 1.0)
    sin = jnp.where(lat, sin_ref[...], 0.0)
    lane = lax.broadcasted_iota(jnp.int32, (TB, LANES), 1)
    first_half = (lane % DIFF_DH) < (DIFF_DH // 2)
    for sec, scale in ((0, qscale), (1, 1.0)):
        t = _dot(hb, w_ref[:, s5w + sec * aw:s5w + (sec + 1) * aw])
        for hd in range(DIFF_HEADS):
            th = t[:, hd * LANES:(hd + 1) * LANES]
            partner = jnp.where(first_half, pltpu.roll(th, LANES - DIFF_DH // 2, 1), pltpu.roll(th, DIFF_DH // 2, 1))
            th = (th * cos + partner * sin) * scale
            zb_ref[0, :, sec * aw + hd * LANES:sec * aw + (hd + 1) * LANES] = th.astype(BF16)
    zb_ref[0, :, 2 * aw:3 * aw] = _dot(hb, w_ref[:, s5w + 2 * aw:s5w + 3 * aw]).astype(BF16)


def _inproj(body, s, modtab, g, w, cos, sin, uw, zw, nlat, name, u_rows=False):
    b, l, d = s.shape
    nj = l // TB
    return pl.pallas_call(
        body,
        grid=(nj, b),
        in_specs=[pl.BlockSpec((1, TB, d), lambda j, i: (i, j, 0)),
                  pl.BlockSpec((1, 1, 6, d), lambda j, i: (i, jnp.where(j < nlat, 1, 0), 0, 0)),
                  pl.BlockSpec((1, d), lambda j, i: (0, 0)),
                  pl.BlockSpec(w.shape, lambda j, i: (0, 0)),
                  pl.BlockSpec((TB, LANES), lambda j, i: (jnp.minimum(j, nlat - 1), 0)),
                  pl.BlockSpec((TB, LANES), lambda j, i: (jnp.minimum(j, nlat - 1), 0))],
        out_specs=[pl.BlockSpec((TB, b, uw), lambda j, i: (j, 0, 0)),
                   pl.BlockSpec((1, TB, zw), lambda j, i: (i, j, 0))]
        + ([pl.BlockSpec((1, TB, uw), lambda j, i: (i, j, 0))] if u_rows else []),
        out_shape=[jax.ShapeDtypeStruct((l, b, uw), F32),
                   jax.ShapeDtypeStruct((b, l, zw), BF16)]
        + ([jax.ShapeDtypeStruct((b, l, uw), F32)] if u_rows else []),
        compiler_params=_cp("arbitrary", "arbitrary"),
        name=name,
    )(s, modtab, g, w, cos, sin)


def _fwd_chunk(s, nlat_c, nctx_c):
    return jnp.where(s < nctx_c, nlat_c + s, s - nctx_c)


def _bwd_chunk(s, nlat_c, nctx_c):
    return nlat_c + nctx_c - 1 - s


def _lru_kernel(uf_ref, ufl_ref, ufr_ref, ub_ref, ubl_ref, ubr_ref, cw_ref, cb_ref, gw_ref, gb_ref, lam_ref,
                of_ref, ob_ref, pad_scr, a_scr, b_scr, h_scr, o_scr, *, tt, nlat_c, nctx_c, nb):
    s = pl.program_id(0)
    nc = nlat_c + nctx_c

    @pl.when(s == 0)
    def _():
        h_scr[...] = jnp.zeros_like(h_scr)

    rows = tt * nb
    views = ((uf_ref, ufl_ref, ufr_ref, _fwd_chunk(s, nlat_c, nctx_c)),
             (ub_ref, ubl_ref, ubr_ref, _bwd_chunk(s, nlat_c, nctx_c)))
    for d, (u_ref, l_ref, r_ref, c) in enumerate(views):
        first = (c == 0) | (c == nlat_c)
        last = (c == nlat_c - 1) | (c == nc - 1)
        pad_scr[0:CONV_LEFT * nb] = jnp.where(first, 0.0, l_ref[...])
        pad_scr[CONV_LEFT * nb:CONV_LEFT * nb + rows] = u_ref[...]
        pad_scr[CONV_LEFT * nb + rows:(CONV_W - 1) * nb + rows] = jnp.where(last, 0.0, r_ref[...])
        uc = cb_ref[...] + pad_scr[0:rows] * cw_ref[0]
        for k in range(1, CONV_W):
            uc = uc + pad_scr[k * nb:k * nb + rows] * cw_ref[k]
        ucb = uc.astype(BF16)
        r = _sigmoid(_dot(ucb, gw_ref[d, 0]) + gb_ref[d, 0])
        i = _sigmoid(_dot(ucb, gw_ref[d, 1]) + gb_ref[d, 1])
        a = jnp.exp((-LRU_C * _softplus(-lam_ref[d])) * r)
        a_scr[d] = a
        b_scr[d] = jnp.sqrt(1.0 - a * a) * (i * uc)

    def body(t, carry):
        hf, hb = carry
        rf = pl.multiple_of(t * nb, nb)
        rb = pl.multiple_of((tt - 1 - t) * nb, nb)
        hf = a_scr[0, pl.ds(rf, nb), :] * hf + b_scr[0, pl.ds(rf, nb), :]
        hb = a_scr[1, pl.ds(rb, nb), :] * hb + b_scr[1, pl.ds(rb, nb), :]
        for g in range(ng):
            o_scr[0, g, pl.ds(rf, nb), :] = hf[:, g * LANES:(g + 1) * LANES]
            o_scr[1, g, pl.ds(rb, nb), :] = hb[:, g * LANES:(g + 1) * LANES]
        return hf, hb

    ng = o_scr.shape[1]
    hf, hb = lax.fori_loop(0, tt, body, (h_scr[0], h_scr[1]), unroll=True)
    h_scr[0] = hf
    h_scr[1] = hb
    for bi in range(nb):
        for g in range(ng):
            of_ref[bi, :, g * LANES:(g + 1) * LANES] = o_scr[0, g, pl.ds(bi, tt, stride=nb), :]
            ob_ref[bi, :, g * LANES:(g + 1) * LANES] = o_scr[1, g, pl.ds(bi, tt, stride=nb), :]


def _lru(u_tm, conv_w, conv_b, gate_w, gate_b, lam, nb, nlat, nctx):
    rows_total, c = u_tm.shape
    tt = LRU_TT
    nlat_c, nctx_c = nlat // tt, nctx // tt
    nc = nlat_c + nctx_c
    rows = tt * nb
    fwd = functools.partial(_fwd_chunk, nlat_c=nlat_c, nctx_c=nctx_c)
    bwd = functools.partial(_bwd_chunk, nlat_c=nlat_c, nctx_c=nctx_c)
    lrows, rrows = CONV_LEFT * nb, (CONV_W - 1 - CONV_LEFT) * nb
    nl, nr = rows // lrows, rows // rrows

    def cur(f):
        return pl.BlockSpec((rows, c), lambda s: (f(s), 0))

    def left(f):
        return pl.BlockSpec((lrows, c), lambda s: (jnp.maximum(f(s) * nl - 1, 0), 0))

    def right(f):
        return pl.BlockSpec((rrows, c), lambda s: (jnp.minimum((f(s) + 1) * nr, nc * nr - 1), 0))

    def whole(a):
        return pl.BlockSpec(a.shape, lambda s: (0,) * a.ndim)

    return pl.pallas_call(
        functools.partial(_lru_kernel, tt=tt, nlat_c=nlat_c, nctx_c=nctx_c, nb=nb),
        grid=(nc,),
        in_specs=[cur(fwd), left(fwd), right(fwd), cur(bwd), left(bwd), right(bwd),
                  whole(conv_w), whole(conv_b), whole(gate_w), whole(gate_b), whole(lam)],
        out_specs=[pl.BlockSpec((nb, tt, c), lambda s: (0, fwd(s), 0)),
                   pl.BlockSpec((nb, tt, c), lambda s: (0, bwd(s), 0))],
        out_shape=[jax.ShapeDtypeStruct((nb, rows_total // nb, c), F32)] * 2,
        scratch_shapes=[pltpu.VMEM((rows + (CONV_W - 1) * nb, c), F32),
                        pltpu.VMEM((2, rows, c), F32),
                        pltpu.VMEM((2, rows, c), F32),
                        pltpu.VMEM((2, nb, c), F32),
                        pltpu.VMEM((2, c // LANES, rows, LANES), F32)],
        compiler_params=_cp("arbitrary"),
        name="rglru",
    )(u_tm, u_tm, u_tm, u_tm, u_tm, u_tm, conv_w, conv_b, gate_w, gate_b, lam)


def _ret_kernel(lg_ref, qf_ref, kf_ref, vf_ref, qb_ref, kb_ref, vb_ref, of_ref, ob_ref, s_scr, d_scr, *, c):
    s = pl.program_id(1)
    ii = lax.broadcasted_iota(jnp.int32, (c, c), 0).astype(F32)
    jj = lax.broadcasted_iota(jnp.int32, (c, c), 1).astype(F32)

    @pl.when(s == 0)
    def _():
        s_scr[...] = jnp.zeros_like(s_scr)
        for d in range(2):
            diff = (ii - jj) if d == 0 else (jj - ii)
            for h in range(RET_HEADS):
                d_scr[d, h] = jnp.where(diff >= 0, jnp.exp(jnp.maximum(diff, 0.0) * lg_ref[d, h]), 0.0)

    ri = lax.broadcasted_iota(jnp.int32, (c, 1), 0).astype(F32)
    views = ((qf_ref, kf_ref, vf_ref, of_ref), (qb_ref, kb_ref, vb_ref, ob_ref))
    for d, (q_ref, k_ref, v_ref, o_ref) in enumerate(views):
        for h in range(RET_HEADS):
            lg = lg_ref[d, h]
            cols = slice(h * RET_D, (h + 1) * RET_D)
            q, k, v = q_ref[0, :, cols], k_ref[0, :, cols], v_ref[0, :, cols]
            sc = lax.dot_general(q, k, (((1,), (1,)), ((), ())), preferred_element_type=F32) * d_scr[d, h]
            inner = _dot(sc.astype(BF16), v)
            st = s_scr[d, h]
            qdec = jnp.exp(((ri + 1.0) if d == 0 else (c - ri)) * lg)
            cross = _dot(q, st.astype(BF16)) * qdec
            o_ref[0, :, cols] = inner + cross
            kdec = jnp.exp(((c - 1.0 - ri) if d == 0 else ri) * lg)
            kd = (k.astype(F32) * kdec).T.astype(BF16)
            gc = jnp.exp(jnp.zeros((RET_D, RET_D), F32) + c * lg)
            s_scr[d, h] = gc * st + _dot(kd, v)


def _retention(log_g, zb, nlat):
    b, l, _ = zb.shape
    nj = l // TB
    w = RET_HEADS * RET_D

    def fwd(s):
        return jnp.where(s < nj - nlat, nlat + s, s - (nj - nlat))

    def bwd(s):
        return nj - 1 - s

    def col(f, i):
        return pl.BlockSpec((1, TB, w), lambda bi, s: (bi, f(s), i))

    def out(f):
        return pl.BlockSpec((1, TB, w), lambda bi, s: (bi, f(s), 0))

    return pl.pallas_call(
        functools.partial(_ret_kernel, c=TB),
        grid=(b, nj),
        in_specs=[pl.BlockSpec(memory_space=pltpu.SMEM),
                  col(fwd, 1), col(fwd, 2), col(fwd, 3), col(bwd, 1), col(bwd, 2), col(bwd, 3)],
        out_specs=[out(fwd), out(bwd)],
        out_shape=[jax.ShapeDtypeStruct((b, l, w), F32)] * 2,
        scratch_shapes=[pltpu.VMEM((2, RET_HEADS, RET_D, RET_D), F32),
                        pltpu.VMEM((2, RET_HEADS, TB, TB), F32)],
        compiler_params=_cp("arbitrary", "arbitrary"),
        name="retention",
    )(log_g, zb, zb, zb, zb, zb, zb)


def _s5_kernel(uf_ref, ub_ref, bm_ref, ar_ref, ai_ref, cm_ref, yf_ref, yb_ref, bu_scr, h_scr, y_scr, *, tt, nb, ns):
    s = pl.program_id(0)

    @pl.when(s == 0)
    def _():
        h_scr[...] = jnp.zeros_like(h_scr)

    cg = 512
    for d, (u_ref, y_ref) in enumerate(((uf_ref, yf_ref), (ub_ref, yb_ref))):
        bu_scr[...] = _dot(u_ref[...].astype(BF16), bm_ref[d])
        for g in range(ns // cg):
            re = slice(g * cg, (g + 1) * cg)
            im = slice(ns + g * cg, ns + (g + 1) * cg)
            ar = ar_ref[d, :, re]
            ai = ai_ref[d, :, re]

            def body(i, carry, re=re, im=im, ar=ar, ai=ai, d=d):
                hr, hi = carry
                t = i if d == 0 else tt - 1 - i
                r0 = pl.multiple_of(t * nb, nb)
                nr = ar * hr - ai * hi + bu_scr[pl.ds(r0, nb), re]
                ni = ar * hi + ai * hr + bu_scr[pl.ds(r0, nb), im]
                bu_scr[pl.ds(r0, nb), re] = nr
                bu_scr[pl.ds(r0, nb), im] = ni
                return nr, ni

            hr, hi = lax.fori_loop(0, tt, body, (h_scr[d, :, re], h_scr[d, :, im]), unroll=True)
            h_scr[d, :, re] = hr
            h_scr[d, :, im] = hi
        y = _dot(bu_scr[...].astype(BF16), cm_ref[d])
        for g in range(y_scr.shape[0]):
            y_scr[g] = y[:, g * LANES:(g + 1) * LANES]
        for bi in range(nb):
            for g in range(y_scr.shape[0]):
                y_ref[bi, :, g * LANES:(g + 1) * LANES] = y_scr[g, pl.ds(bi, tt, stride=nb), :]


def _s5(u_tm, bmat, a_re, a_im, cmat, nb, nlat, nctx):
    rows_total, c = u_tm.shape
    tt = S5_TT
    nlat_c, nctx_c = nlat // tt, nctx // tt
    rows = tt * nb
    ns = a_re.shape[-1]
    fwd = functools.partial(_fwd_chunk, nlat_c=nlat_c, nctx_c=nctx_c)
    bwd = functools.partial(_bwd_chunk, nlat_c=nlat_c, nctx_c=nctx_c)

    def cur(f):
        return pl.BlockSpec((rows, c), lambda s: (f(s), 0))

    def whole(a):
        return pl.BlockSpec(a.shape, lambda s: (0,) * a.ndim)

    return pl.pallas_call(
        functools.partial(_s5_kernel, tt=tt, nb=nb, ns=ns),
        grid=(nlat_c + nctx_c,),
        in_specs=[cur(fwd), cur(bwd), whole(bmat), whole(a_re), whole(a_im), whole(cmat)],
        out_specs=[pl.BlockSpec((nb, tt, c), lambda s: (0, fwd(s), 0)),
                   pl.BlockSpec((nb, tt, c), lambda s: (0, bwd(s), 0))],
        out_shape=[jax.ShapeDtypeStruct((nb, rows_total // nb, c), F32)] * 2,
        scratch_shapes=[pltpu.VMEM((rows, 2 * ns), F32), pltpu.VMEM((2, nb, 2 * ns), F32),
                        pltpu.VMEM((c // LANES, rows, LANES), F32)],
        compiler_params=_cp("arbitrary"),
        name="s5",
    )(u_tm, u_tm, bmat, a_re, a_im, cmat)


def _attn_kernel(lam_ref, q_ref, k_ref, v_ref, gn_ref, o_ref, *, out_scale):
    tq, tk = q_ref.shape[1], k_ref.shape[1]
    q = q_ref[0]
    lane = lax.broadcasted_iota(jnp.int32, (tq, LANES), 1)
    zero = jnp.zeros_like(q)
    qm = (jnp.where(lane < DIFF_DH, q, zero), jnp.where(lane >= DIFF_DH, q, zero))
    kc = k_ref[0]
    va = jnp.concatenate([v_ref[0], jnp.ones((tk, LANES), BF16)], axis=1)
    rb = min(ATTN_RB, tq)
    for r in range(tq // rb):
        rows = slice(r * rb, (r + 1) * rb)
        outs = []
        for m in range(2):
            sc = lax.dot_general(qm[m][rows], kc, (((1,), (1,)), ((), ())), preferred_element_type=F32)
            p = jnp.exp2(sc - jnp.max(sc, axis=1, keepdims=True)).astype(BF16)
            a = _dot(p, va)
            outs.append(a[:, :LANES] / a[:, LANES:LANES + 1])
        att = outs[0] - lam_ref[0] * outs[1]
        att = att * lax.rsqrt(jnp.mean(att * att, axis=-1, keepdims=True) + EPS) * (gn_ref[0] * out_scale)
        o_ref[0, rows] = att.astype(BF16)


def _attention(lam, zb, gn, nlat, out_scale):
    b, l, _ = zb.shape
    h = DIFF_HEADS
    nctx = l - nlat
    tq = ATTN_TQ if nlat % ATTN_TQ == 0 else TB
    gn3 = gn.reshape(h, 1, LANES)

    def call(tq_, nq, kv_rows, q0, kv0, name):
        return pl.pallas_call(
            functools.partial(_attn_kernel, out_scale=out_scale),
            grid=(b, h, nq),
            in_specs=[pl.BlockSpec(memory_space=pltpu.SMEM),
                      pl.BlockSpec((1, tq_, LANES), lambda bi, hi, qi: (bi, q0 + qi, hi)),
                      pl.BlockSpec((1, kv_rows, LANES), lambda bi, hi, qi: (bi, kv0, h + hi)),
                      pl.BlockSpec((1, kv_rows, LANES), lambda bi, hi, qi: (bi, kv0, 2 * h + hi)),
                      pl.BlockSpec((1, 1, LANES), lambda bi, hi, qi: (hi, 0, 0))],
            out_specs=pl.BlockSpec((1, tq_, LANES), lambda bi, hi, qi: (bi, qi, hi)),
            out_shape=jax.ShapeDtypeStruct((b, nq * tq_, h * LANES), BF16),
            compiler_params=_cp("arbitrary", "arbitrary", "arbitrary"),
            name=name,
        )(lam, zb, zb, zb, gn3)

    att_x = call(tq, nlat // tq, l, 0, 0, "diffattn_latent")
    att_c = call(nctx, 1, nctx, nlat // nctx, nlat // nctx, "diffattn_context")
    return jnp.concatenate([att_x, att_c], axis=1)


def _out_tail(parts, w_ref, s_ref, mod, ng_ref, xo_ref):
    y, r0 = None, 0
    for p in parts:
        n = p.shape[1]
        t = _dot(p.astype(BF16), w_ref[r0:r0 + n, :])
        y = t if y is None else y + t
        r0 += n
    xn = s_ref[0] + mod[2:3] * _rms(y, ng_ref[1:2])
    xo_ref[0] = xn
    return _rms(xn, ng_ref[2:3]) * (1.0 + mod[4:5]) + mod[3:4]


def _outproj_even_kernel(lf_ref, lb_ref, zg_ref, zo_ref, rf_ref, rb_ref, gn_ref, w_ref, s_ref, mod_ref, ng_ref,
                         xo_ref, h2_ref):
    parts = [(lf_ref[0] + lb_ref[0]) * _gelu(zg_ref[0].astype(F32))]
    r = rf_ref[0] + rb_ref[0]
    og = _silu(zo_ref[0].astype(F32))
    for h in range(RET_HEADS):
        cols = slice(h * RET_D, (h + 1) * RET_D)
        y = r[:, cols]
        y = y - jnp.mean(y, axis=-1, keepdims=True)
        y = y * lax.rsqrt(jnp.mean(y * y, axis=-1, keepdims=True) + EPS) * gn_ref[:, cols]
        parts.append(y * og[:, cols])
    h2 = _out_tail(parts, w_ref, s_ref, mod_ref[0, 0], ng_ref, xo_ref)
    h2_ref[0] = h2.astype(BF16)


def _outproj_odd_kernel(yf_ref, yb_ref, u_ref, att_ref, d_ref, gw_ref, gb_ref, w_ref, s_ref, mod_ref, ng_ref,
                        rw_ref, rb_ref, xo_ref, h2_ref, gates_ref):
    z = _gelu(yf_ref[0] + yb_ref[0] + d_ref[...] * u_ref[0])
    s5o = z * _sigmoid(_dot(z.astype(BF16), gw_ref[...]) + gb_ref[...])
    h2 = _out_tail([s5o, att_ref[0]], w_ref, s_ref, mod_ref[0, 0], ng_ref, xo_ref)
    d = h2.shape[1]
    h2_ref[0, :, :d] = h2.astype(BF16)
    logits = _dot3(h2, rw_ref[...]) + rb_ref[...]
    lane = lax.broadcasted_iota(jnp.int32, logits.shape, 1).astype(F32)
    m1 = jnp.max(logits, axis=1, keepdims=True)
    i1 = jnp.min(jnp.where(logits == m1, lane, float(LANES)), axis=1, keepdims=True)
    rest = jnp.where(lane == i1, 2.0 * NEG, logits)
    m2 = jnp.max(rest, axis=1, keepdims=True)
    i2 = jnp.min(jnp.where(rest == m2, lane, float(LANES)), axis=1, keepdims=True)
    e = jnp.exp(m2 - m1)
    p1 = 1.0 / (1.0 + e)
    gates = jnp.where(lane == i1, p1, 0.0) + jnp.where(lane == i2, e * p1, 0.0)
    ids = jnp.where(lane == float(N_EXPERTS), i1, 0.0) + jnp.where(lane == float(N_EXPERTS + 1), i2, 0.0)
    gates_ref[0] = gates + ids
    ghi = gates.astype(BF16).astype(F32)
    glo = (gates - ghi).astype(BF16).astype(F32)
    h2_ref[0, :, d:] = (ghi + pltpu.roll(glo, N_EXPERTS, 1)).astype(BF16)


def _whole2(a):
    return pl.BlockSpec(a.shape, lambda i, j: (0,) * a.ndim)


def _mod_spec(d, nlat_t):
    return pl.BlockSpec((1, 1, 6, d), lambda i, j: (i, jnp.where(j < nlat_t, 1, 0), 0, 0))


def _outproj_even(lf, lb, zb, rf, rb, gn, w, s, modtab, ng, nlat_t):
    b, l, d = s.shape
    cw = RET_HEADS * RET_D
    tm = pl.BlockSpec((1, TB, cw), lambda i, j: (i, j, 0))
    row = pl.BlockSpec((1, TB, d), lambda i, j: (i, j, 0))
    return pl.pallas_call(
        _outproj_even_kernel,
        grid=(b, l // TB),
        in_specs=[tm, tm,
                  pl.BlockSpec((1, TB, cw), lambda i, j: (i, j, 0)),
                  pl.BlockSpec((1, TB, cw), lambda i, j: (i, j, 4)),
                  pl.BlockSpec((1, TB, cw), lambda i, j: (i, j, 0)),
                  pl.BlockSpec((1, TB, cw), lambda i, j: (i, j, 0)),
                  _whole2(gn), _whole2(w), row, _mod_spec(d, nlat_t), _whole2(ng)],
        out_specs=[row, row],
        out_shape=[jax.ShapeDtypeStruct((b, l, d), F32), jax.ShapeDtypeStruct((b, l, d), BF16)],
        compiler_params=_cp("arbitrary", "arbitrary"),
        name="outproj_even",
    )(lf, lb, zb, zb, rf, rb, gn, w, s, modtab, ng)


def _outproj_odd(yf, yb, u, att, dskip, glu_w, glu_b, w, s, modtab, ng, rw, rb, nlat_t):
    b, l, d = s.shape
    sw = dskip.shape[1]
    aw = att.shape[2]
    tm = pl.BlockSpec((1, TB, sw), lambda i, j: (i, j, 0))
    row = pl.BlockSpec((1, TB, d), lambda i, j: (i, j, 0))
    return pl.pallas_call(
        _outproj_odd_kernel,
        grid=(b, l // TB),
        in_specs=[tm, tm, tm,
                  pl.BlockSpec((1, TB, aw), lambda i, j: (i, j, 0)),
                  _whole2(dskip), _whole2(glu_w), _whole2(glu_b), _whole2(w), row, _mod_spec(d, nlat_t),
                  _whole2(ng), _whole2(rw), _whole2(rb)],
        out_specs=[row, pl.BlockSpec((1, TB, d + LANES), lambda i, j: (i, j, 0)),
                   pl.BlockSpec((1, TB, LANES), lambda i, j: (i, j, 0))],
        out_shape=[jax.ShapeDtypeStruct((b, l, d), F32), jax.ShapeDtypeStruct((b, l, d + LANES), BF16),
                   jax.ShapeDtypeStruct((b, l, LANES), F32)],
        compiler_params=_cp("arbitrary", "arbitrary"),
        name="outproj_odd",
    )(yf, yb, u, att, dskip, glu_w, glu_b, w, s, modtab, ng, rw, rb)


def _cast_kernel(x_ref, o_ref):
    o_ref[...] = x_ref[...].astype(BF16)


def _to_bf16(w):
    e, r, c = w.shape
    rows = max(m for m in range(TB, max(CAST_BLOCK_ELEMS // c, TB) + 1, TB) if r % m == 0)
    spec = pl.BlockSpec((1, rows, c), lambda i, j: (i, j, 0))
    return pl.pallas_call(
        _cast_kernel,
        grid=(e, r // rows),
        in_specs=[spec],
        out_specs=spec,
        out_shape=jax.ShapeDtypeStruct(w.shape, BF16),
        compiler_params=_cp("arbitrary", "arbitrary"),
        name="cast_bf16",
    )(w)


def _ffn_kernel(x_ref, wg_ref, wu_ref, wd_ref, s_ref, mod_ref, g_ref, o_ref, acc_ref, *, tpb, seq):
    i, f = pl.program_id(0), pl.program_id(1)

    @pl.when(f == 0)
    def _():
        acc_ref[...] = jnp.zeros_like(acc_ref)

    x = x_ref[...]
    act = _silu(_dot(x, wg_ref[...])) * _dot(x, wu_ref[...])
    acc_ref[...] += _dot(act.astype(BF16), wd_ref[...])

    @pl.when(f == pl.num_programs(1) - 1)
    def _():
        tm = acc_ref.shape[0]
        row = lax.rem(i, tpb) * tm + lax.broadcasted_iota(jnp.int32, (tm, 1), 0)
        mod = mod_ref[0]
        gate = jnp.where(row < seq, mod[1, 5:6], mod[0, 5:6])
        o_ref[...] = s_ref[...] + gate * _rms(acc_ref[...], g_ref[...])


def _ffn(x, wg, wu, wd, s, modtab, g, seq):
    b, l, d = s.shape
    n = b * l
    ff = wg.shape[1]
    tm, tf = FFN_TM, FFN_TF
    tpb = l // tm
    rows = pl.BlockSpec((tm, d), lambda i, f: (i, 0))
    out = pl.pallas_call(
        functools.partial(_ffn_kernel, tpb=tpb, seq=seq),
        grid=(n // tm, ff // tf),
        in_specs=[rows,
                  pl.BlockSpec((d, tf), lambda i, f: (0, f)),
                  pl.BlockSpec((d, tf), lambda i, f: (0, f)),
                  pl.BlockSpec((tf, d), lambda i, f: (f, 0)),
                  rows,
                  pl.BlockSpec((1, 2, 6, d), lambda i, f: (i // tpb, 0, 0, 0)),
                  pl.BlockSpec(g.shape, lambda i, f: (0, 0))],
        out_specs=rows,
        out_shape=jax.ShapeDtypeStruct((n, d), F32),
        scratch_shapes=[pltpu.VMEM((tm, d), F32)],
        compiler_params=_cp("arbitrary", "arbitrary"),
        name="dense_ffn",
    )(x, wg, wu, wd, s.reshape(n, d), modtab, g)
    return out.reshape(b, l, d)


def _moe_plan(route, n_tiles):
    n = route.shape[0]
    nb = n // TB
    ne = N_EXPERTS
    e1 = route[:, ne].astype(jnp.int32)
    e2 = route[:, ne + 1].astype(jnp.int32)
    ex = jnp.arange(ne, dtype=jnp.int32)
    sel = ((e1[:, None] == ex) | (e2[:, None] == ex)).astype(jnp.int32)
    incl = jnp.cumsum(sel, axis=0)
    count = incl[-1]
    gsize = (count + MOE_TM - 1) // MOE_TM * MOE_TM
    gend = jnp.cumsum(gsize)
    goff = gend - gsize
    posm = goff[None, :] + incl - sel
    pos1 = jnp.sum(jnp.where(e1[:, None] == ex, posm, 0), axis=1)
    pos2 = jnp.sum(jnp.where(e2[:, None] == ex, posm, 0), axis=1)
    pos_cols = jnp.stack([pos1, pos2], axis=1)
    pos_rows = jnp.stack([pos1.reshape(nb, TB), pos2.reshape(nb, TB)], axis=1)
    start = posm[::TB]
    last = (goff + count)[None]
    end = jnp.concatenate([start[1:], last], axis=0)
    r0 = jnp.arange(n_tiles, dtype=jnp.int32) * MOE_TM
    tile_valid = (r0 < gend[-1]).astype(jnp.int32)
    tile_exp = jnp.minimum(jnp.sum((r0[:, None] >= gend[None, :]).astype(jnp.int32), axis=1), ne - 1)
    st, en = start.T[tile_exp], end.T[tile_exp]
    lim = jnp.minimum(r0 + MOE_TM, last[0][tile_exp])
    tb_lo = jnp.clip(jnp.sum((en <= r0[:, None]).astype(jnp.int32), axis=1), 0, nb - 1)
    tb_hi = jnp.clip(jnp.sum((st < lim[:, None]).astype(jnp.int32), axis=1) - 1, 0, nb - 1)
    a0 = start // MOE_ALIGN * MOE_ALIGN
    lo_off = start - a0
    hi_off = lo_off + (end - start)
    k = jnp.arange(MOE_SUBS_PER_WIN, dtype=jnp.int32) * MOE_SUB
    c_valid = ((end > start)[..., None] & (hi_off[..., None] > k)).reshape(nb, -1)
    c_base = (a0[..., None] + k).reshape(nb, -1)
    c_lo = jnp.clip(lo_off[..., None] - k, 0, MOE_SUB).reshape(nb, -1)
    c_hi = jnp.clip(hi_off[..., None] - k, 0, MOE_SUB).reshape(nb, -1)
    slot = jnp.cumsum(c_valid.astype(jnp.int32), axis=1) - 1
    put = (slot[..., None] == jnp.arange(MOE_SLOTS, dtype=jnp.int32)) & c_valid[..., None]

    def place(v):
        return jnp.sum(jnp.where(put, v[..., None], 0), axis=1).reshape(-1)

    return (tile_exp, tile_valid, tb_lo, tb_hi, pos_rows, pos_cols, place(c_base), place(c_lo), place(c_hi))


def _moe_kernel(te_ref, tv_ref, lo_ref, hi_ref, pos_ref, h_hbm, wg_ref, wu_ref, wd_ref, y_ref, xbuf, xs_ref, sem,
                *, n_tiles, tf):
    i = pl.program_id(0)
    d, ff = wg_ref.shape[1], wg_ref.shape[2]
    grp = MOE_GROUP
    nb = pos_ref.shape[0]
    slot = lax.rem(i, 2)

    def first_blk(t, g):
        return jnp.minimum(lo_ref[t] + g * grp, nb - grp)

    def grp_copy(fb, sl):
        return pltpu.make_async_copy(h_hbm.at[pl.ds(pl.multiple_of(fb * TB, TB), grp * TB)], xbuf.at[sl], sem.at[sl])

    valid = tv_ref[i] == 1

    @pl.when((i == 0) & valid)
    def _():
        grp_copy(first_blk(0, 0), 0).start()

    @pl.when(valid)
    def _():
        lo, hi = lo_ref[i], hi_ref[i]
        rid = i * MOE_TM + lax.broadcasted_iota(jnp.int32, (MOE_TM, TB), 0)

        def gather(g):
            fb = first_blk(i, g)
            pieces = []
            for k in range(grp):
                tb = fb + k
                p = jnp.where((tb >= lo + g * grp) & (tb <= hi), pos_ref[tb], -1)
                pieces.append(jnp.where(rid == p[0:1], 1.0, 0.0) + jnp.where(rid == p[1:2], 1.0, 0.0))
            return _dot(jnp.concatenate(pieces, axis=1).astype(BF16), xbuf[slot])

        grp_copy(first_blk(i, 0), slot).wait()
        xs_ref[...] = gather(0)

        def more(g, c):
            cp = grp_copy(first_blk(i, g), slot)
            cp.start()
            cp.wait()
            xs_ref[...] += gather(g)
            return c

        lax.fori_loop(1, (hi - lo) // grp + 1, more, 0)
        nxt = jnp.minimum(i + 1, n_tiles - 1)

        @pl.when((i + 1 < n_tiles) & (tv_ref[nxt] == 1))
        def _():
            grp_copy(first_blk(nxt, 0), 1 - slot).start()

        xs = xs_ref[...]
        x = xs[:, :d].astype(BF16)
        ext = xs[:, d:]
        lane = lax.broadcasted_iota(jnp.int32, ext.shape, 1)
        e_t = te_ref[i]
        gate = jnp.sum(jnp.where((lane == e_t) | (lane == e_t + N_EXPERTS), ext, 0.0), axis=1, keepdims=True)
        acc = None
        for f in range(ff // tf):
            cols = slice(f * tf, (f + 1) * tf)
            act = _silu(_dot(x, wg_ref[0, :, cols])) * _dot(x, wu_ref[0, :, cols])
            t = _dot(act.astype(BF16), wd_ref[0, cols, :])
            acc = t if acc is None else acc + t
        y_ref[...] = (acc * gate).astype(BF16)

    @pl.when(jnp.logical_not(valid))
    def _():
        y_ref[...] = jnp.zeros_like(y_ref)


def _moe(h2x, plan, wg, wu, wd, n_tiles):
    n, dx = h2x.shape
    ne, d, ff = wg.shape
    tile_exp, tile_valid, tb_lo, tb_hi, pos_rows = plan[:5]

    def wspec(shape):
        return pl.BlockSpec(shape, lambda i, te, tv, lo, hi: (te[i], 0, 0), pipeline_mode=pl.Buffered(1))

    return pl.pallas_call(
        functools.partial(_moe_kernel, n_tiles=n_tiles, tf=FFN_TF),
        grid_spec=pltpu.PrefetchScalarGridSpec(
            num_scalar_prefetch=4,
            grid=(n_tiles,),
            in_specs=[pl.BlockSpec(pos_rows.shape, lambda i, te, tv, lo, hi: (0, 0, 0)),
                      pl.BlockSpec(memory_space=pl.ANY),
                      wspec((1, d, ff)), wspec((1, d, ff)), wspec((1, ff, d))],
            out_specs=pl.BlockSpec((MOE_TM, d), lambda i, te, tv, lo, hi: (i, 0)),
            scratch_shapes=[pltpu.VMEM((2, MOE_GROUP * TB, dx), BF16), pltpu.VMEM((MOE_TM, dx), F32),
                            pltpu.SemaphoreType.DMA((2,))]),
        out_shape=jax.ShapeDtypeStruct((n_tiles * MOE_TM, d), BF16),
        compiler_params=_cp("arbitrary"),
        name="moe_ffn",
    )(tile_exp, tile_valid, tb_lo, tb_hi, pos_rows, h2x, wg, wu, wd)


def _combine_kernel(base_ref, lo_ref, hi_ref, s_ref, pos_ref, mod_ref, g_ref, ys_hbm, o_ref, buf, sem, *, nj, nsteps):
    step = pl.program_id(0) * nj + pl.program_id(1)
    slot = lax.rem(step, 2)

    def win_copy(t, k, sl):
        a = pl.multiple_of(base_ref[t * MOE_SLOTS + k], MOE_ALIGN)
        return pltpu.make_async_copy(ys_hbm.at[pl.ds(a, MOE_SUB)], buf.at[sl, pl.ds(k * MOE_SUB, MOE_SUB)],
                                     sem.at[sl, k])

    @pl.when(step == 0)
    def _():
        for k in range(MOE_SLOTS):
            win_copy(0, k, 0).start()

    @pl.when(step + 1 < nsteps)
    def _():
        for k in range(MOE_SLOTS):
            win_copy(step + 1, k, 1 - slot).start()

    pos = pos_ref[...]
    pos1 = jnp.broadcast_to(pos[:, 0:1], (TB, MOE_SUB))
    pos2 = jnp.broadcast_to(pos[:, 1:2], (TB, MOE_SUB))
    lane = lax.broadcasted_iota(jnp.int32, (1, MOE_SUB), 1)
    acc = jnp.zeros(s_ref.shape[1:], F32)
    per_dot = 2 * LANES // MOE_SUB
    for g in range(MOE_SLOTS // per_dot):
        pieces = []
        for k in range(g * per_dot, (g + 1) * per_dot):
            win_copy(step, k, slot).wait()
            j = step * MOE_SLOTS + k
            held = jnp.where((lane >= lo_ref[j]) & (lane < hi_ref[j]), lane + base_ref[j], -1)
            pieces.append(jnp.where(pos1 == held, 1.0, 0.0) + jnp.where(pos2 == held, 1.0, 0.0))
        oh = jnp.concatenate(pieces, axis=1).astype(BF16)
        rows = slice(g * per_dot * MOE_SUB, (g + 1) * per_dot * MOE_SUB)
        acc = acc + _dot(oh, buf[slot, rows])
    o_ref[0] = s_ref[0] + mod_ref[0, 0][5:6] * _rms(acc, g_ref[...])


def _moe_combine(s, ys, plan, modtab, g, nlat_t):
    b, l, d = s.shape
    nj = l // TB
    pos_cols, base, lo_off, hi_off = plan[5:]
    row = pl.BlockSpec((1, TB, d), lambda i, j, *_: (i, j, 0))
    return pl.pallas_call(
        functools.partial(_combine_kernel, nj=nj, nsteps=b * nj),
        grid_spec=pltpu.PrefetchScalarGridSpec(
            num_scalar_prefetch=3,
            grid=(b, nj),
            in_specs=[row,
                      pl.BlockSpec((TB, 2), lambda i, j, *_: (i * nj + j, 0)),
                      pl.BlockSpec((1, 1, 6, d), lambda i, j, *_: (i, jnp.where(j < nlat_t, 1, 0), 0, 0)),
                      pl.BlockSpec(g.shape, lambda i, j, *_: (0, 0)),
                      pl.BlockSpec(memory_space=pl.ANY)],
            out_specs=row,
            scratch_shapes=[pltpu.VMEM((2, MOE_SLOTS * MOE_SUB, d), BF16),
                            pltpu.SemaphoreType.DMA((2, MOE_SLOTS))]),
        out_shape=jax.ShapeDtypeStruct((b, l, d), F32),
        compiler_params=_cp("arbitrary", "arbitrary"),
        name="moe_combine",
    )(base, lo_off, hi_off, s, pos_cols, modtab, g, ys)


def _rope_tables(seq, dim):
    t = jnp.arange(seq)
    rows = (t // GRID_W).astype(F32)
    cols = (t % GRID_W).astype(F32)
    quarter = dim // 4
    inv = ROPE_BASE ** (-jnp.arange(quarter, dtype=F32) / quarter)
    ang = jnp.concatenate([rows[:, None] * inv, cols[:, None] * inv], axis=-1)
    c, s = jnp.cos(ang), jnp.sin(ang)
    rep = LANES // dim
    return jnp.tile(jnp.concatenate([c, c], -1), (1, rep)), jnp.tile(jnp.concatenate([-s, s], -1), (1, rep))


def _block_diag(x):
    g, r, c = x.shape[-3:]
    y = jnp.einsum('...grc,gh->...grhc', x, jnp.eye(g, dtype=x.dtype))
    return y.reshape(x.shape[:-3] + (g * r, g * c))


def _s5_params(lam_re, lam_im, log_dt, b_re, b_im, c_re, c_im, nb):
    lam = lax.complex(lam_re, lam_im)
    a_bar = jnp.exp(lam * jnp.exp(log_dt)[..., None])
    b_bar = ((a_bar - 1.0) / lam)[..., None] * lax.complex(b_re, b_im)
    bt = jnp.swapaxes(b_bar, -1, -2)
    bmat = jnp.concatenate([_block_diag(jnp.real(bt)), _block_diag(jnp.imag(bt))], axis=-1).astype(BF16)
    cmat = jnp.concatenate([_block_diag(jnp.swapaxes(c_re, -1, -2)),
                            -_block_diag(jnp.swapaxes(c_im, -1, -2))], axis=-2).astype(BF16)
    ns = a_bar.shape[1] * a_bar.shape[2]
    a_re = jnp.broadcast_to(jnp.real(a_bar).reshape(2, 1, ns), (2, nb, ns))
    a_im = jnp.broadcast_to(jnp.imag(a_bar).reshape(2, 1, ns), (2, nb, ns))
    return bmat, a_re, a_im, cmat


def kernel(x, c, ctx, c_ctx, mod_w, mod_b, norm_g, ev_w_in, ev_conv_w, ev_conv_b, lru_gate_w, lru_gate_b, lru_lam, ret_theta, ret_gn, ev_w_out, ffd_w_gate, ffd_w_up, ffd_w_down, od_w_in, s5_lam_re, s5_lam_im, s5_log_dt, s5_b_re, s5_b_im, s5_c_re, s5_c_im, s5_d, s5_glu_w, s5_glu_b, diff_lam, diff_gn, od_w_out, moe_router_w, moe_router_b, moe_w_gate, moe_w_up, moe_w_down):
    b, seq, d = x.shape
    nctx = ctx.shape[1]
    l = seq + nctx
    n = b * l
    depth = mod_w.shape[0]
    nlat_t = seq // TB
    assert b == SUBLANES and seq % TB == 0 and nctx % TB == 0 and seq % nctx == 0 and l % FFN_TM == 0

    s = jnp.concatenate([x, ctx], axis=1)
    cc = jnp.concatenate([c, c_ctx[None], jnp.zeros((2 * SUBLANES - b - 1, d), F32)], axis=0)
    mt = _modtab(cc, mod_w, mod_b).reshape(depth, 2 * SUBLANES, 6, d)
    modtab = jnp.stack([jnp.broadcast_to(mt[:, b:b + 1], (depth, b, 6, d)), mt[:, :b]], axis=2)

    cos_r, sin_r = _rope_tables(seq, RET_D)
    cos_d, sin_d = _rope_tables(seq, DIFF_DH)

    for layer in range(depth):
        i = layer // 2
        mtab = modtab[layer]
        ng = norm_g[layer]
        if layer % 2 == 0:
            lw = ev_conv_w.shape[-1]
            u3d, zb = _inproj(functools.partial(_inproj_even_kernel, nlat=nlat_t, kscale=RET_D ** -0.5),
                              s, mtab, ng[0:1], ev_w_in[i].astype(BF16), cos_r, sin_r, lw, 5 * lw, nlat_t,
                              "inproj_even")
            gw = _block_diag(lru_gate_w[i]).astype(BF16)
            lf, lb = _lru(u3d.reshape(l * b, lw), ev_conv_w[i].reshape(CONV_W, 1, lw), ev_conv_b[i].reshape(1, lw),
                          gw, lru_gate_b[i].reshape(2, 2, 1, lw), lru_lam[i].reshape(2, 1, lw), b, seq, nctx)
            rf, rb = _retention(jax.nn.log_sigmoid(ret_theta[i].astype(F32)), zb, nlat_t)
            s, h2 = _outproj_even(lf, lb, zb, rf, rb, ret_gn[i].reshape(1, -1),
                                  ev_w_out[i].astype(BF16), s, mtab, ng, nlat_t)
            s = _ffn(h2.reshape(n, d), ffd_w_gate[i].astype(BF16), ffd_w_up[i].astype(BF16),
                     ffd_w_down[i].astype(BF16), s, mtab, ng[3:4], seq)
        else:
            sw = s5_d.shape[-1]
            aw = DIFF_HEADS * 2 * DIFF_DH
            lam_init = 0.8 - 0.6 * math.exp(-0.3 * layer)
            u3d, zb, ub = _inproj(functools.partial(_inproj_odd_kernel, nlat=nlat_t, qscale=DIFF_DH ** -0.5 * math.log2(math.e), s5w=sw),
                                  s, mtab, ng[0:1], od_w_in[i].astype(BF16), cos_d, sin_d, sw, 3 * aw, nlat_t,
                                  "inproj_odd", u_rows=True)
            bmat, a_re, a_im, cmat = _s5_params(s5_lam_re[i], s5_lam_im[i], s5_log_dt[i], s5_b_re[i], s5_b_im[i],
                                                s5_c_re[i], s5_c_im[i], b)
            yf, yb = _s5(u3d.reshape(l * b, sw), bmat, a_re, a_im, cmat, b, seq, nctx)
            lp = diff_lam[i].astype(F32)
            lam = (jnp.exp(jnp.sum(lp[0] * lp[1])) - jnp.exp(jnp.sum(lp[2] * lp[3])) + lam_init).reshape(1)
            att = _attention(lam, zb, diff_gn[i], seq, 1.0 - lam_init)
            rw = jnp.zeros((d, LANES), F32).at[:, :N_EXPERTS].set(moe_router_w[i])
            rbias = jnp.full((1, LANES), NEG, F32).at[0, :N_EXPERTS].set(moe_router_b[i])
            s, h2x, gates = _outproj_odd(yf, yb, ub, att,
                                        s5_d[i].reshape(1, sw), s5_glu_w[i].astype(BF16), s5_glu_b[i].reshape(1, sw),
                                        od_w_out[i].astype(BF16), s, mtab, ng, rw, rbias, nlat_t)
            n_tiles = (2 * n) // MOE_TM + N_EXPERTS + 2
            plan = _moe_plan(gates.reshape(n, LANES), n_tiles)
            ys = _moe(h2x.reshape(n, d + LANES), plan, _to_bf16(moe_w_gate[i]), _to_bf16(moe_w_up[i]),
                      _to_bf16(moe_w_down[i]), n_tiles)
            s = _moe_combine(s, ys, plan, mtab, ng[3:4], nlat_t)
    return s[:, :seq]
```

```python
import functools
import math

import jax
import jax.numpy as jnp
from jax import lax
from jax.experimental import pallas as pl
from jax.experimental.pallas import tpu as pltpu

F32 = jnp.float32
BF16 = jnp.bfloat16

EPS = 1e-6
ROPE_BASE = 10000.0
GRID_W = 64
LRU_C = 8.0
LRU_BLOCKS = 8
CONV_W = 4
CONV_LEFT = 2
RET_HEADS = 4
RET_D = 128
S5_GROUP = 16
S5_STATE = 64
DIFF_HEADS = 6
DIFF_DH = 64
N_EXPERTS = 8
LANES = 128
SUBLANES = 8

TB = 256
LRU_TT = 64
S5_TT = 32
ATTN_TQ = 1024
ATTN_RB = 128
FFN_TM = 768
FFN_TF = 512
CAST_BLOCK_ELEMS = 1024 * 1024
MOE_TM = 256
MOE_ALIGN = 16
MOE_SUB = 128
MOE_SUBS_PER_WIN = (TB + MOE_ALIGN - 1 + MOE_SUB - 1) // MOE_SUB
MOE_SLOTS = (N_EXPERTS * (MOE_ALIGN - 1 + MOE_SUB - 1) + 2 * TB) // MOE_SUB
MOE_GROUP = 6
VMEM_LIMIT = 56 * 1024 * 1024
NEG = -1e30


def _cp(*sem):
    return pltpu.CompilerParams(dimension_semantics=sem, vmem_limit_bytes=VMEM_LIMIT)


def _dot(a, b):
    return jnp.dot(a, b, preferred_element_type=F32)


def _split(x):
    hi = x.astype(BF16)
    lo = (x - hi.astype(F32)).astype(BF16)
    return hi, lo


def _dot3(a, w):
    ah, al = _split(a)
    wh, wl = _split(w)
    return _dot(ah, wh) + (_dot(ah, wl) + _dot(al, wh))


def _rms(x, g):
    return x * lax.rsqrt(jnp.mean(x * x, axis=-1, keepdims=True) + EPS) * g


def _sigmoid(x):
    return 0.5 * jnp.tanh(0.5 * x) + 0.5


def _silu(x):
    return x * _sigmoid(x)


def _gelu(x):
    return 0.5 * x * (1.0 + jnp.tanh(math.sqrt(2.0 / math.pi) * (x + 0.044715 * (x * x * x))))


def _softplus(x):
    return jnp.maximum(x, 0.0) + jnp.log(1.0 + jnp.exp(-jnp.abs(x)))


def _modtab_kernel(c_ref, w_ref, b_ref, o_ref):
    c = c_ref[...]
    o_ref[0] = _dot3(_silu(c), w_ref[0]) + b_ref[0]


def _modtab(cc, mod_w, mod_b):
    depth, d, n = mod_w.shape
    tn = 1536
    return pl.pallas_call(
        _modtab_kernel,
        grid=(depth, n // tn),
        in_specs=[pl.BlockSpec((cc.shape[0], d), lambda l, j: (0, 0)),
                  pl.BlockSpec((1, d, tn), lambda l, j: (l, 0, j)),
                  pl.BlockSpec((1, 1, tn), lambda l, j: (l, 0, j))],
        out_specs=pl.BlockSpec((1, cc.shape[0], tn), lambda l, j: (l, 0, j)),
        out_shape=jax.ShapeDtypeStruct((depth, cc.shape[0], n), F32),
        compiler_params=_cp("arbitrary", "arbitrary"),
        name="modtab",
    )(cc, mod_w, mod_b.reshape(depth, 1, n))


def _norm_mod_in(s_ref, mod_ref, g_ref):
    mod = mod_ref[0, 0]
    h = _rms(s_ref[0], g_ref[...]) * (1.0 + mod[1:2]) + mod[0:1]
    return h.astype(BF16)


def _inproj_even_kernel(s_ref, mod_ref, g_ref, w_ref, cos_ref, sin_ref, u_ref, zb_ref, *, nlat, kscale):
    j = pl.program_id(0)
    hb = _norm_mod_in(s_ref, mod_ref, g_ref)
    lw = RET_HEADS * RET_D

    def proj(i):
        return _dot(hb, w_ref[:, i * lw:(i + 1) * lw])

    u_ref[:, pl.ds(pl.program_id(1), 1), :] = proj(0)[:, None, :]
    zb_ref[0, :, 0:lw] = proj(1).astype(BF16)
    lat = j < nlat
    cos = jnp.where(lat, cos_ref[...], 1.0)
    sin = jnp.where(lat, sin_ref[...], 0.0)
    for sec, scale in ((0, 1.0), (1, kscale)):
        t = proj(2 + sec)
        for hd in range(RET_HEADS):
            th = t[:, hd * RET_D:(hd + 1) * RET_D]
            th = (th * cos + pltpu.roll(th, RET_D // 2, 1) * sin) * scale
            zb_ref[0, :, (1 + sec) * lw + hd * RET_D:(1 + sec) * lw + (hd + 1) * RET_D] = th.astype(BF16)
    zb_ref[0, :, 3 * lw:4 * lw] = proj(4).astype(BF16)
    zb_ref[0, :, 4 * lw:5 * lw] = proj(5).astype(BF16)


def _inproj_odd_kernel(s_ref, mod_ref, g_ref, w_ref, cos_ref, sin_ref, u_ref, zb_ref, ub_ref, *, nlat, qscale, s5w):
    j = pl.program_id(0)
    hb = _norm_mod_in(s_ref, mod_ref, g_ref)
    aw = DIFF_HEADS * 2 * DIFF_DH
    u = _dot(hb, w_ref[:, 0:s5w])
    u_ref[:, pl.ds(pl.program_id(1), 1), :] = u[:, None, :]
    ub_ref[0] = u
    lat = j < nlat
    cos = jnp.where(lat, cos_ref[...], 1.0)
    sin = jnp.where(lat, sin_ref[...], 0.0)
    lane = lax.broadcasted_iota(jnp.int32, (TB, LANES), 1)
    first_half = (lane % DIFF_DH) < (DIFF_DH // 2)
    for sec, scale in ((0, qscale), (1, 1.0)):
        t = _dot(hb, w_ref[:, s5w + sec * aw:s5w + (sec + 1) * aw])
        for hd in range(DIFF_HEADS):
            th = t[:, hd * LANES:(hd + 1) * LANES]
            partner = jnp.where(first_half, pltpu.roll(th, LANES - DIFF_DH // 2, 1), pltpu.roll(th, DIFF_DH // 2, 1))
            th = (th * cos + partner * sin) * scale
            zb_ref[0, :, sec * aw + hd * LANES:sec * aw + (hd + 1) * LANES] = th.astype(BF16)
    zb_ref[0, :, 2 * aw:3 * aw] = _dot(hb, w_ref[:, s5w + 2 * aw:s5w + 3 * aw]).astype(BF16)


def _inproj(body, s, modtab, g, w, cos, sin, uw, zw, nlat, name, u_rows=False):
    b, l, d = s.shape
    nj = l // TB
    return pl.pallas_call(
        body,
        grid=(nj, b),
        in_specs=[pl.BlockSpec((1, TB, d), lambda j, i: (i, j, 0)),
                  pl.BlockSpec((1, 1, 6, d), lambda j, i: (i, jnp.where(j < nlat, 1, 0), 0, 0)),
                  pl.BlockSpec((1, d), lambda j, i: (0, 0)),
                  pl.BlockSpec(w.shape, lambda j, i: (0, 0)),
                  pl.BlockSpec((TB, LANES), lambda j, i: (jnp.minimum(j, nlat - 1), 0)),
                  pl.BlockSpec((TB, LANES), lambda j, i: (jnp.minimum(j, nlat - 1), 0))],
        out_specs=[pl.BlockSpec((TB, b, uw), lambda j, i: (j, 0, 0)),
                   pl.BlockSpec((1, TB, zw), lambda j, i: (i, j, 0))]
        + ([pl.BlockSpec((1, TB, uw), lambda j, i: (i, j, 0))] if u_rows else []),
        out_shape=[jax.ShapeDtypeStruct((l, b, uw), F32),
                   jax.ShapeDtypeStruct((b, l, zw), BF16)]
        + ([jax.ShapeDtypeStruct((b, l, uw), F32)] if u_rows else []),
        compiler_params=_cp("arbitrary", "arbitrary"),
        name=name,
    )(s, modtab, g, w, cos, sin)


def _fwd_chunk(s, nlat_c, nctx_c):
    return jnp.where(s < nctx_c, nlat_c + s, s - nctx_c)


def _bwd_chunk(s, nlat_c, nctx_c):
    return nlat_c + nctx_c - 1 - s


def _lru_kernel(uf_ref, ufl_ref, ufr_ref, ub_ref, ubl_ref, ubr_ref, cw_ref, cb_ref, gw_ref, gb_ref, lam_ref,
                of_ref, ob_ref, pad_scr, a_scr, b_scr, h_scr, o_scr, *, tt, nlat_c, nctx_c, nb):
    s = pl.program_id(0)
    nc = nlat_c + nctx_c

    @pl.when(s == 0)
    def _():
        h_scr[...] = jnp.zeros_like(h_scr)

    rows = tt * nb
    views = ((uf_ref, ufl_ref, ufr_ref, _fwd_chunk(s, nlat_c, nctx_c)),
             (ub_ref, ubl_ref, ubr_ref, _bwd_chunk(s, nlat_c, nctx_c)))
    for d, (u_ref, l_ref, r_ref, c) in enumerate(views):
        first = (c == 0) | (c == nlat_c)
        last = (c == nlat_c - 1) | (c == nc - 1)
        pad_scr[0:CONV_LEFT * nb] = jnp.where(first, 0.0, l_ref[...])
        pad_scr[CONV_LEFT * nb:CONV_LEFT * nb + rows] = u_ref[...]
        pad_scr[CONV_LEFT * nb + rows:(CONV_W - 1) * nb + rows] = jnp.where(last, 0.0, r_ref[...])
        uc = cb_ref[...] + pad_scr[0:rows] * cw_ref[0]
        for k in range(1, CONV_W):
            uc = uc + pad_scr[k * nb:k * nb + rows] * cw_ref[k]
        ucb = uc.astype(BF16)
        r = _sigmoid(_dot(ucb, gw_ref[d, 0]) + gb_ref[d, 0])
        i = _sigmoid(_dot(ucb, gw_ref[d, 1]) + gb_ref[d, 1])
        a = jnp.exp((-LRU_C * _softplus(-lam_ref[d])) * r)
        a_scr[d] = a
        b_scr[d] = jnp.sqrt(1.0 - a * a) * (i * uc)

    def body(t, carry):
        hf, hb = carry
        rf = pl.multiple_of(t * nb, nb)
        rb = pl.multiple_of((tt - 1 - t) * nb, nb)
        hf = a_scr[0, pl.ds(rf, nb), :] * hf + b_scr[0, pl.ds(rf, nb), :]
        hb = a_scr[1, pl.ds(rb, nb), :] * hb + b_scr[1, pl.ds(rb, nb), :]
        for g in range(ng):
            o_scr[0, g, pl.ds(rf, nb), :] = hf[:, g * LANES:(g + 1) * LANES]
            o_scr[1, g, pl.ds(rb, nb), :] = hb[:, g * LANES:(g + 1) * LANES]
        return hf, hb

    ng = o_scr.shape[1]
    hf, hb = lax.fori_loop(0, tt, body, (h_scr[0], h_scr[1]), unroll=True)
    h_scr[0] = hf
    h_scr[1] = hb
    for bi in range(nb):
        for g in range(ng):
            of_ref[bi, :, g * LANES:(g + 1) * LANES] = o_scr[0, g, pl.ds(bi, tt, stride=nb), :]
            ob_ref[bi, :, g * LANES:(g + 1) * LANES] = o_scr[1, g, pl.ds(bi, tt, stride=nb), :]


def _lru(u_tm, conv_w, conv_b, gate_w, gate_b, lam, nb, nlat, nctx):
    rows_total, c = u_tm.shape
    tt = LRU_TT
    nlat_c, nctx_c = nlat // tt, nctx // tt
    nc = nlat_c + nctx_c
    rows = tt * nb
    fwd = functools.partial(_fwd_chunk, nlat_c=nlat_c, nctx_c=nctx_c)
    bwd = functools.partial(_bwd_chunk, nlat_c=nlat_c, nctx_c=nctx_c)
    lrows, rrows = CONV_LEFT * nb, (CONV_W - 1 - CONV_LEFT) * nb
    nl, nr = rows // lrows, rows // rrows

    def cur(f):
        return pl.BlockSpec((rows, c), lambda s: (f(s), 0))

    def left(f):
        return pl.BlockSpec((lrows, c), lambda s: (jnp.maximum(f(s) * nl - 1, 0), 0))

    def right(f):
        return pl.BlockSpec((rrows, c), lambda s: (jnp.minimum((f(s) + 1) * nr, nc * nr - 1), 0))

    def whole(a):
        return pl.BlockSpec(a.shape, lambda s: (0,) * a.ndim)

    return pl.pallas_call(
        functools.partial(_lru_kernel, tt=tt, nlat_c=nlat_c, nctx_c=nctx_c, nb=nb),
        grid=(nc,),
        in_specs=[cur(fwd), left(fwd), right(fwd), cur(bwd), left(bwd), right(bwd),
                  whole(conv_w), whole(conv_b), whole(gate_w), whole(gate_b), whole(lam)],
        out_specs=[pl.BlockSpec((nb, tt, c), lambda s: (0, fwd(s), 0)),
                   pl.BlockSpec((nb, tt, c), lambda s: (0, bwd(s), 0))],
        out_shape=[jax.ShapeDtypeStruct((nb, rows_total // nb, c), F32)] * 2,
        scratch_shapes=[pltpu.VMEM((rows + (CONV_W - 1) * nb, c), F32),
                        pltpu.VMEM((2, rows, c), F32),
                        pltpu.VMEM((2, rows, c), F32),
                        pltpu.VMEM((2, nb, c), F32),
                        pltpu.VMEM((2, c // LANES, rows, LANES), F32)],
        compiler_params=_cp("arbitrary"),
        name="rglru",
    )(u_tm, u_tm, u_tm, u_tm, u_tm, u_tm, conv_w, conv_b, gate_w, gate_b, lam)


def _ret_kernel(lg_ref, qf_ref, kf_ref, vf_ref, qb_ref, kb_ref, vb_ref, of_ref, ob_ref, s_scr, d_scr, *, c):
    s = pl.program_id(1)
    ii = lax.broadcasted_iota(jnp.int32, (c, c), 0).astype(F32)
    jj = lax.broadcasted_iota(jnp.int32, (c, c), 1).astype(F32)

    @pl.when(s == 0)
    def _():
        s_scr[...] = jnp.zeros_like(s_scr)
        for d in range(2):
            diff = (ii - jj) if d == 0 else (jj - ii)
            for h in range(RET_HEADS):
                d_scr[d, h] = jnp.where(diff >= 0, jnp.exp(jnp.maximum(diff, 0.0) * lg_ref[d, h]), 0.0)

    ri = lax.broadcasted_iota(jnp.int32, (c, 1), 0).astype(F32)
    views = ((qf_ref, kf_ref, vf_ref, of_ref), (qb_ref, kb_ref, vb_ref, ob_ref))
    for d, (q_ref, k_ref, v_ref, o_ref) in enumerate(views):
        for h in range(RET_HEADS):
            lg = lg_ref[d, h]
            cols = slice(h * RET_D, (h + 1) * RET_D)
            q, k, v = q_ref[0, :, cols], k_ref[0, :, cols], v_ref[0, :, cols]
            sc = lax.dot_general(q, k, (((1,), (1,)), ((), ())), preferred_element_type=F32) * d_scr[d, h]
            inner = _dot(sc.astype(BF16), v)
            st = s_scr[d, h]
            qdec = jnp.exp(((ri + 1.0) if d == 0 else (c - ri)) * lg)
            cross = _dot(q, st.astype(BF16)) * qdec
            o_ref[0, :, cols] = inner + cross
            kdec = jnp.exp(((c - 1.0 - ri) if d == 0 else ri) * lg)
            kd = (k.astype(F32) * kdec).T.astype(BF16)
            gc = jnp.exp(jnp.zeros((RET_D, RET_D), F32) + c * lg)
            s_scr[d, h] = gc * st + _dot(kd, v)


def _retention(log_g, zb, nlat):
    b, l, _ = zb.shape
    nj = l // TB
    w = RET_HEADS * RET_D

    def fwd(s):
        return jnp.where(s < nj - nlat, nlat + s, s - (nj - nlat))

    def bwd(s):
        return nj - 1 - s

    def col(f, i):
        return pl.BlockSpec((1, TB, w), lambda bi, s: (bi, f(s), i))

    def out(f):
        return pl.BlockSpec((1, TB, w), lambda bi, s: (bi, f(s), 0))

    return pl.pallas_call(
        functools.partial(_ret_kernel, c=TB),
        grid=(b, nj),
        in_specs=[pl.BlockSpec(memory_space=pltpu.SMEM),
                  col(fwd, 1), col(fwd, 2), col(fwd, 3), col(bwd, 1), col(bwd, 2), col(bwd, 3)],
        out_specs=[out(fwd), out(bwd)],
        out_shape=[jax.ShapeDtypeStruct((b, l, w), F32)] * 2,
        scratch_shapes=[pltpu.VMEM((2, RET_HEADS, RET_D, RET_D), F32),
                        pltpu.VMEM((2, RET_HEADS, TB, TB), F32)],
        compiler_params=_cp("arbitrary", "arbitrary"),
        name="retention",
    )(log_g, zb, zb, zb, zb, zb, zb)


def _s5_kernel(uf_ref, ub_ref, bm_ref, ar_ref, ai_ref, cm_ref, yf_ref, yb_ref, bu_scr, h_scr, y_scr, *, tt, nb, ns):
    s = pl.program_id(0)

    @pl.when(s == 0)
    def _():
        h_scr[...] = jnp.zeros_like(h_scr)

    cg = 512
    for d, (u_ref, y_ref) in enumerate(((uf_ref, yf_ref), (ub_ref, yb_ref))):
        bu_scr[...] = _dot(u_ref[...].astype(BF16), bm_ref[d])
        for g in range(ns // cg):
            re = slice(g * cg, (g + 1) * cg)
            im = slice(ns + g * cg, ns + (g + 1) * cg)
            ar = ar_ref[d, :, re]
            ai = ai_ref[d, :, re]

            def body(i, carry, re=re, im=im, ar=ar, ai=ai, d=d):
                hr, hi = carry
                t = i if d == 0 else tt - 1 - i
                r0 = pl.multiple_of(t * nb, nb)
                nr = ar * hr - ai * hi + bu_scr[pl.ds(r0, nb), re]
                ni = ar * hi + ai * hr + bu_scr[pl.ds(r0, nb), im]
                bu_scr[pl.ds(r0, nb), re] = nr
                bu_scr[pl.ds(r0, nb), im] = ni
                return nr, ni

            hr, hi = lax.fori_loop(0, tt, body, (h_scr[d, :, re], h_scr[d, :, im]), unroll=True)
            h_scr[d, :, re] = hr
            h_scr[d, :, im] = hi
        y = _dot(bu_scr[...].astype(BF16), cm_ref[d])
        for g in range(y_scr.shape[0]):
            y_scr[g] = y[:, g * LANES:(g + 1) * LANES]
        for bi in range(nb):
            for g in range(y_scr.shape[0]):
                y_ref[bi, :, g * LANES:(g + 1) * LANES] = y_scr[g, pl.ds(bi, tt, stride=nb), :]


def _s5(u_tm, bmat, a_re, a_im, cmat, nb, nlat, nctx):
    rows_total, c = u_tm.shape
    tt = S5_TT
    nlat_c, nctx_c = nlat // tt, nctx // tt
    rows = tt * nb
    ns = a_re.shape[-1]
    fwd = functools.partial(_fwd_chunk, nlat_c=nlat_c, nctx_c=nctx_c)
    bwd = functools.partial(_bwd_chunk, nlat_c=nlat_c, nctx_c=nctx_c)

    def cur(f):
        return pl.BlockSpec((rows, c), lambda s: (f(s), 0))

    def whole(a):
        return pl.BlockSpec(a.shape, lambda s: (0,) * a.ndim)

    return pl.pallas_call(
        functools.partial(_s5_kernel, tt=tt, nb=nb, ns=ns),
        grid=(nlat_c + nctx_c,),
        in_specs=[cur(fwd), cur(bwd), whole(bmat), whole(a_re), whole(a_im), whole(cmat)],
        out_specs=[pl.BlockSpec((nb, tt, c), lambda s: (0, fwd(s), 0)),
                   pl.BlockSpec((nb, tt, c), lambda s: (0, bwd(s), 0))],
        out_shape=[jax.ShapeDtypeStruct((nb, rows_total // nb, c), F32)] * 2,
        scratch_shapes=[pltpu.VMEM((rows, 2 * ns), F32), pltpu.VMEM((2, nb, 2 * ns), F32),
                        pltpu.VMEM((c // LANES, rows, LANES), F32)],
        compiler_params=_cp("arbitrary"),
        name="s5",
    )(u_tm, u_tm, bmat, a_re, a_im, cmat)


def _attn_kernel(lam_ref, q_ref, k_ref, v_ref, gn_ref, o_ref, *, out_scale):
    tq, tk = q_ref.shape[1], k_ref.shape[1]
    q = q_ref[0]
    lane = lax.broadcasted_iota(jnp.int32, (tq, LANES), 1)
    zero = jnp.zeros_like(q)
    qm = (jnp.where(lane < DIFF_DH, q, zero), jnp.where(lane >= DIFF_DH, q, zero))
    kc = k_ref[0]
    va = jnp.concatenate([v_ref[0], jnp.ones((tk, LANES), BF16)], axis=1)
    rb = min(ATTN_RB, tq)
    for r in range(tq // rb):
        rows = slice(r * rb, (r + 1) * rb)
        outs = []
        for m in range(2):
            sc = lax.dot_general(qm[m][rows], kc, (((1,), (1,)), ((), ())), preferred_element_type=F32)
            p = jnp.exp2(sc - jnp.max(sc, axis=1, keepdims=True)).astype(BF16)
            a = _dot(p, va)
            outs.append(a[:, :LANES] / a[:, LANES:LANES + 1])
        att = outs[0] - lam_ref[0] * outs[1]
        att = att * lax.rsqrt(jnp.mean(att * att, axis=-1, keepdims=True) + EPS) * (gn_ref[0] * out_scale)
        o_ref[0, rows] = att.astype(BF16)


def _attention(lam, zb, gn, nlat, out_scale):
    b, l, _ = zb.shape
    h = DIFF_HEADS
    nctx = l - nlat
    tq = ATTN_TQ if nlat % ATTN_TQ == 0 else TB
    gn3 = gn.reshape(h, 1, LANES)

    def call(tq_, nq, kv_rows, q0, kv0, name):
        return pl.pallas_call(
            functools.partial(_attn_kernel, out_scale=out_scale),
            grid=(b, h, nq),
            in_specs=[pl.BlockSpec(memory_space=pltpu.SMEM),
                      pl.BlockSpec((1, tq_, LANES), lambda bi, hi, qi: (bi, q0 + qi, hi)),
                      pl.BlockSpec((1, kv_rows, LANES), lambda bi, hi, qi: (bi, kv0, h + hi)),
                      pl.BlockSpec((1, kv_rows, LANES), lambda bi, hi, qi: (bi, kv0, 2 * h + hi)),
                      pl.BlockSpec((1, 1, LANES), lambda bi, hi, qi: (hi, 0, 0))],
            out_specs=pl.BlockSpec((1, tq_, LANES), lambda bi, hi, qi: (bi, qi, hi)),
            out_shape=jax.ShapeDtypeStruct((b, nq * tq_, h * LANES), BF16),
            compiler_params=_cp("arbitrary", "arbitrary", "arbitrary"),
            name=name,
        )(lam, zb, zb, zb, gn3)

    att_x = call(tq, nlat // tq, l, 0, 0, "diffattn_latent")
    att_c = call(nctx, 1, nctx, nlat // nctx, nlat // nctx, "diffattn_context")
    return jnp.concatenate([att_x, att_c], axis=1)


def _out_tail(parts, w_ref, s_ref, mod, ng_ref, xo_ref):
    y, r0 = None, 0
    for p in parts:
        n = p.shape[1]
        t = _dot(p.astype(BF16), w_ref[r0:r0 + n, :])
        y = t if y is None else y + t
        r0 += n
    xn = s_ref[0] + mod[2:3] * _rms(y, ng_ref[1:2])
    xo_ref[0] = xn
    return _rms(xn, ng_ref[2:3]) * (1.0 + mod[4:5]) + mod[3:4]


def _outproj_even_kernel(lf_ref, lb_ref, zg_ref, zo_ref, rf_ref, rb_ref, gn_ref, w_ref, s_ref, mod_ref, ng_ref,
                         xo_ref, h2_ref):
    parts = [(lf_ref[0] + lb_ref[0]) * _gelu(zg_ref[0].astype(F32))]
    r = rf_ref[0] + rb_ref[0]
    og = _silu(zo_ref[0].astype(F32))
    for h in range(RET_HEADS):
        cols = slice(h * RET_D, (h + 1) * RET_D)
        y = r[:, cols]
        y = y - jnp.mean(y, axis=-1, keepdims=True)
        y = y * lax.rsqrt(jnp.mean(y * y, axis=-1, keepdims=True) + EPS) * gn_ref[:, cols]
        parts.append(y * og[:, cols])
    h2 = _out_tail(parts, w_ref, s_ref, mod_ref[0, 0], ng_ref, xo_ref)
    h2_ref[0] = h2.astype(BF16)


def _outproj_odd_kernel(yf_ref, yb_ref, u_ref, att_ref, d_ref, gw_ref, gb_ref, w_ref, s_ref, mod_ref, ng_ref,
                        rw_ref, rb_ref, xo_ref, h2_ref, gates_ref):
    z = _gelu(yf_ref[0] + yb_ref[0] + d_ref[...] * u_ref[0])
    s5o = z * _sigmoid(_dot(z.astype(BF16), gw_ref[...]) + gb_ref[...])
    h2 = _out_tail([s5o, att_ref[0]], w_ref, s_ref, mod_ref[0, 0], ng_ref, xo_ref)
    d = h2.shape[1]
    h2_ref[0, :, :d] = h2.astype(BF16)
    logits = _dot3(h2, rw_ref[...]) + rb_ref[...]
    lane = lax.broadcasted_iota(jnp.int32, logits.shape, 1).astype(F32)
    m1 = jnp.max(logits, axis=1, keepdims=True)
    i1 = jnp.min(jnp.where(logits == m1, lane, float(LANES)), axis=1, keepdims=True)
    rest = jnp.where(lane == i1, 2.0 * NEG, logits)
    m2 = jnp.max(rest, axis=1, keepdims=True)
    i2 = jnp.min(jnp.where(rest == m2, lane, float(LANES)), axis=1, keepdims=True)
    e = jnp.exp(m2 - m1)
    p1 = 1.0 / (1.0 + e)
    gates = jnp.where(lane == i1, p1, 0.0) + jnp.where(lane == i2, e * p1, 0.0)
    ids = jnp.where(lane == float(N_EXPERTS), i1, 0.0) + jnp.where(lane == float(N_EXPERTS + 1), i2, 0.0)
    gates_ref[0] = gates + ids
    ghi = gates.astype(BF16).astype(F32)
    glo = (gates - ghi).astype(BF16).astype(F32)
    h2_ref[0, :, d:] = (ghi + pltpu.roll(glo, N_EXPERTS, 1)).astype(BF16)


def _whole2(a):
    return pl.BlockSpec(a.shape, lambda i, j: (0,) * a.ndim)


def _mod_spec(d, nlat_t):
    return pl.BlockSpec((1, 1, 6, d), lambda i, j: (i, jnp.where(j < nlat_t, 1, 0), 0, 0))


def _outproj_even(lf, lb, zb, rf, rb, gn, w, s, modtab, ng, nlat_t):
    b, l, d = s.shape
    cw = RET_HEADS * RET_D
    tm = pl.BlockSpec((1, TB, cw), lambda i, j: (i, j, 0))
    row = pl.BlockSpec((1, TB, d), lambda i, j: (i, j, 0))
    return pl.pallas_call(
        _outproj_even_kernel,
        grid=(b, l // TB),
        in_specs=[tm, tm,
                  pl.BlockSpec((1, TB, cw), lambda i, j: (i, j, 0)),
                  pl.BlockSpec((1, TB, cw), lambda i, j: (i, j, 4)),
                  pl.BlockSpec((1, TB, cw), lambda i, j: (i, j, 0)),
                  pl.BlockSpec((1, TB, cw), lambda i, j: (i, j, 0)),
                  _whole2(gn), _whole2(w), row, _mod_spec(d, nlat_t), _whole2(ng)],
        out_specs=[row, row],
        out_shape=[jax.ShapeDtypeStruct((b, l, d), F32), jax.ShapeDtypeStruct((b, l, d), BF16)],
        compiler_params=_cp("arbitrary", "arbitrary"),
        name="outproj_even",
    )(lf, lb, zb, zb, rf, rb, gn, w, s, modtab, ng)


def _outproj_odd(yf, yb, u, att, dskip, glu_w, glu_b, w, s, modtab, ng, rw, rb, nlat_t):
    b, l, d = s.shape
    sw = dskip.shape[1]
    aw = att.shape[2]
    tm = pl.BlockSpec((1, TB, sw), lambda i, j: (i, j, 0))
    row = pl.BlockSpec((1, TB, d), lambda i, j: (i, j, 0))
    return pl.pallas_call(
        _outproj_odd_kernel,
        grid=(b, l // TB),
        in_specs=[tm, tm, tm,
                  pl.BlockSpec((1, TB, aw), lambda i, j: (i, j, 0)),
                  _whole2(dskip), _whole2(glu_w), _whole2(glu_b), _whole2(w), row, _mod_spec(d, nlat_t),
                  _whole2(ng), _whole2(rw), _whole2(rb)],
        out_specs=[row, pl.BlockSpec((1, TB, d + LANES), lambda i, j: (i, j, 0)),
                   pl.BlockSpec((1, TB, LANES), lambda i, j: (i, j, 0))],
        out_shape=[jax.ShapeDtypeStruct((b, l, d), F32), jax.ShapeDtypeStruct((b, l, d + LANES), BF16),
                   jax.ShapeDtypeStruct((b, l, LANES), F32)],
        compiler_params=_cp("arbitrary", "arbitrary"),
        name="outproj_odd",
    )(yf, yb, u, att, dskip, glu_w, glu_b, w, s, modtab, ng, rw, rb)


def _cast_kernel(x_ref, o_ref):
    o_ref[...] = x_ref[...].astype(BF16)


def _to_bf16(w, layer):
    _, e, r, c = w.shape
    rows = max(m for m in range(TB, max(CAST_BLOCK_ELEMS // c, TB) + 1, TB) if r % m == 0)
    return pl.pallas_call(
        _cast_kernel,
        grid=(e, r // rows),
        in_specs=[pl.BlockSpec((None, 1, rows, c), lambda i, j: (layer, i, j, 0))],
        out_specs=pl.BlockSpec((1, rows, c), lambda i, j: (i, j, 0)),
        out_shape=jax.ShapeDtypeStruct((e, r, c), BF16),
        compiler_params=_cp("arbitrary", "arbitrary"),
        name="cast_bf16",
    )(w)


def _ffn_kernel(x_ref, wg_ref, wu_ref, wd_ref, s_ref, mod_ref, g_ref, o_ref, *, tpb, seq, tf):
    i = pl.program_id(0)
    x = x_ref[...]
    tm, ff = x.shape[0], wg_ref.shape[1]
    acc = None
    for f in range(ff // tf):
        cols = slice(f * tf, (f + 1) * tf)
        act = _silu(_dot(x, wg_ref[:, cols])) * _dot(x, wu_ref[:, cols])
        t = _dot(act.astype(BF16), wd_ref[cols, :])
        acc = t if acc is None else acc + t
    row = lax.rem(i, tpb) * tm + lax.broadcasted_iota(jnp.int32, (tm, 1), 0)
    mod = mod_ref[0]
    gate = jnp.where(row < seq, mod[1, 5:6], mod[0, 5:6])
    o_ref[...] = s_ref[...] + gate * _rms(acc, g_ref[...])


def _ffn(x, wg, wu, wd, s, modtab, g, seq):
    b, l, d = s.shape
    n = b * l
    tm = FFN_TM
    tpb = l // tm
    rows = pl.BlockSpec((tm, d), lambda i: (i, 0))

    def whole(a):
        return pl.BlockSpec(a.shape, lambda i: (0, 0), pipeline_mode=pl.Buffered(1))

    out = pl.pallas_call(
        functools.partial(_ffn_kernel, tpb=tpb, seq=seq, tf=FFN_TF),
        grid=(n // tm,),
        in_specs=[rows, whole(wg), whole(wu), whole(wd), rows,
                  pl.BlockSpec((1, 2, 6, d), lambda i: (i // tpb, 0, 0, 0)),
                  pl.BlockSpec(g.shape, lambda i: (0, 0))],
        out_specs=rows,
        out_shape=jax.ShapeDtypeStruct((n, d), F32),
        compiler_params=_cp("arbitrary"),
        name="dense_ffn",
    )(x, wg, wu, wd, s.reshape(n, d), modtab, g)
    return out.reshape(b, l, d)


def _moe_plan(route, n_tiles):
    n = route.shape[0]
    nb = n // TB
    ne = N_EXPERTS
    e1 = route[:, ne].astype(jnp.int32)
    e2 = route[:, ne + 1].astype(jnp.int32)
    ex = jnp.arange(ne, dtype=jnp.int32)
    sel = ((e1[:, None] == ex) | (e2[:, None] == ex)).astype(jnp.int32)
    incl = jnp.cumsum(sel, axis=0)
    count = incl[-1]
    gsize = (count + MOE_TM - 1) // MOE_TM * MOE_TM
    gend = jnp.cumsum(gsize)
    goff = gend - gsize
    posm = goff[None, :] + incl - sel
    pos1 = jnp.sum(jnp.where(e1[:, None] == ex, posm, 0), axis=1)
    pos2 = jnp.sum(jnp.where(e2[:, None] == ex, posm, 0), axis=1)
    pos_cols = jnp.stack([pos1, pos2], axis=1)
    pos_rows = jnp.stack([pos1.reshape(nb, TB), pos2.reshape(nb, TB)], axis=1)
    start = posm[::TB]
    last = (goff + count)[None]
    end = jnp.concatenate([start[1:], last], axis=0)
    r0 = jnp.arange(n_tiles, dtype=jnp.int32) * MOE_TM
    tile_valid = (r0 < gend[-1]).astype(jnp.int32)
    tile_exp = jnp.minimum(jnp.sum((r0[:, None] >= gend[None, :]).astype(jnp.int32), axis=1), ne - 1)
    st, en = start.T[tile_exp], end.T[tile_exp]
    lim = jnp.minimum(r0 + MOE_TM, last[0][tile_exp])
    tb_lo = jnp.clip(jnp.sum((en <= r0[:, None]).astype(jnp.int32), axis=1), 0, nb - 1)
    tb_hi = jnp.clip(jnp.sum((st < lim[:, None]).astype(jnp.int32), axis=1) - 1, 0, nb - 1)
    a0 = start // MOE_ALIGN * MOE_ALIGN
    lo_off = start - a0
    hi_off = lo_off + (end - start)
    k = jnp.arange(MOE_SUBS_PER_WIN, dtype=jnp.int32) * MOE_SUB
    c_valid = ((end > start)[..., None] & (hi_off[..., None] > k)).reshape(nb, -1)
    c_base = (a0[..., None] + k).reshape(nb, -1)
    c_lo = jnp.clip(lo_off[..., None] - k, 0, MOE_SUB).reshape(nb, -1)
    c_hi = jnp.clip(hi_off[..., None] - k, 0, MOE_SUB).reshape(nb, -1)
    slot = jnp.cumsum(c_valid.astype(jnp.int32), axis=1) - 1
    put = (slot[..., None] == jnp.arange(MOE_SLOTS, dtype=jnp.int32)) & c_valid[..., None]

    def place(v):
        return jnp.sum(jnp.where(put, v[..., None], 0), axis=1).reshape(-1)

    return (tile_exp, tile_valid, tb_lo, tb_hi, pos_rows, pos_cols, place(c_base), place(c_lo), place(c_hi))


def _moe_kernel(te_ref, tv_ref, lo_ref, hi_ref, pos_ref, h_hbm, wg_ref, wu_ref, wd_ref, y_ref, xbuf, xs_ref, sem,
                *, n_tiles, tf):
    i = pl.program_id(0)
    d, ff = wg_ref.shape[1], wg_ref.shape[2]
    grp = MOE_GROUP
    nb = pos_ref.shape[0]
    slot = lax.rem(i, 2)

    def first_blk(t, g):
        return jnp.minimum(lo_ref[t] + g * grp, nb - grp)

    def grp_copy(fb, sl):
        return pltpu.make_async_copy(h_hbm.at[pl.ds(pl.multiple_of(fb * TB, TB), grp * TB)], xbuf.at[sl], sem.at[sl])

    valid = tv_ref[i] == 1

    @pl.when((i == 0) & valid)
    def _():
        grp_copy(first_blk(0, 0), 0).start()

    @pl.when(valid)
    def _():
        lo, hi = lo_ref[i], hi_ref[i]
        rid = i * MOE_TM + lax.broadcasted_iota(jnp.int32, (MOE_TM, TB), 0)

        def gather(g):
            fb = first_blk(i, g)
            pieces = []
            for k in range(grp):
                tb = fb + k
                p = jnp.where((tb >= lo + g * grp) & (tb <= hi), pos_ref[tb], -1)
                pieces.append(jnp.where(rid == p[0:1], 1.0, 0.0) + jnp.where(rid == p[1:2], 1.0, 0.0))
            return _dot(jnp.concatenate(pieces, axis=1).astype(BF16), xbuf[slot])

        grp_copy(first_blk(i, 0), slot).wait()
        xs_ref[...] = gather(0)

        def more(g, c):
            cp = grp_copy(first_blk(i, g), slot)
            cp.start()
            cp.wait()
            xs_ref[...] += gather(g)
            return c

        lax.fori_loop(1, (hi - lo) // grp + 1, more, 0)
        nxt = jnp.minimum(i + 1, n_tiles - 1)

        @pl.when((i + 1 < n_tiles) & (tv_ref[nxt] == 1))
        def _():
            grp_copy(first_blk(nxt, 0), 1 - slot).start()

        xs = xs_ref[...]
        x = xs[:, :d].astype(BF16)
        ext = xs[:, d:]
        lane = lax.broadcasted_iota(jnp.int32, ext.shape, 1)
        e_t = te_ref[i]
        gate = jnp.sum(jnp.where((lane == e_t) | (lane == e_t + N_EXPERTS), ext, 0.0), axis=1, keepdims=True)
        acc = None
        for f in range(ff // tf):
            cols = slice(f * tf, (f + 1) * tf)
            act = _silu(_dot(x, wg_ref[0, :, cols])) * _dot(x, wu_ref[0, :, cols])
            t = _dot(act.astype(BF16), wd_ref[0, cols, :])
            acc = t if acc is None else acc + t
        y_ref[...] = (acc * gate).astype(BF16)

    @pl.when(jnp.logical_not(valid))
    def _():
        y_ref[...] = jnp.zeros_like(y_ref)


def _moe(h2x, plan, wg, wu, wd, n_tiles):
    n, dx = h2x.shape
    ne, d, ff = wg.shape
    tile_exp, tile_valid, tb_lo, tb_hi, pos_rows = plan[:5]

    def wspec(shape):
        return pl.BlockSpec(shape, lambda i, te, tv, lo, hi: (te[i], 0, 0), pipeline_mode=pl.Buffered(1))

    return pl.pallas_call(
        functools.partial(_moe_kernel, n_tiles=n_tiles, tf=FFN_TF),
        grid_spec=pltpu.PrefetchScalarGridSpec(
            num_scalar_prefetch=4,
            grid=(n_tiles,),
            in_specs=[pl.BlockSpec(pos_rows.shape, lambda i, te, tv, lo, hi: (0, 0, 0)),
                      pl.BlockSpec(memory_space=pl.ANY),
                      wspec((1, d, ff)), wspec((1, d, ff)), wspec((1, ff, d))],
            out_specs=pl.BlockSpec((MOE_TM, d), lambda i, te, tv, lo, hi: (i, 0)),
            scratch_shapes=[pltpu.VMEM((2, MOE_GROUP * TB, dx), BF16), pltpu.VMEM((MOE_TM, dx), F32),
                            pltpu.SemaphoreType.DMA((2,))]),
        out_shape=jax.ShapeDtypeStruct((n_tiles * MOE_TM, d), BF16),
        compiler_params=_cp("arbitrary"),
        name="moe_ffn",
    )(tile_exp, tile_valid, tb_lo, tb_hi, pos_rows, h2x, wg, wu, wd)


def _combine_kernel(base_ref, lo_ref, hi_ref, s_ref, pos_ref, mod_ref, g_ref, ys_hbm, o_ref, buf, sem,
                    *, nj, nsteps, nkeep):
    step = pl.program_id(0) * nj + pl.program_id(1)
    slot = lax.rem(step, 2)

    def win_copy(t, k, sl):
        a = pl.multiple_of(base_ref[t * MOE_SLOTS + k], MOE_ALIGN)
        return pltpu.make_async_copy(ys_hbm.at[pl.ds(a, MOE_SUB)], buf.at[sl, pl.ds(k * MOE_SUB, MOE_SUB)],
                                     sem.at[sl, k])

    @pl.when(step == 0)
    def _():
        for k in range(MOE_SLOTS):
            win_copy(0, k, 0).start()

    @pl.when(step + 1 < nsteps)
    def _():
        for k in range(MOE_SLOTS):
            win_copy(step + 1, k, 1 - slot).start()

    pos = pos_ref[...]
    pos1 = jnp.broadcast_to(pos[:, 0:1], (TB, MOE_SUB))
    pos2 = jnp.broadcast_to(pos[:, 1:2], (TB, MOE_SUB))
    lane = lax.broadcasted_iota(jnp.int32, (1, MOE_SUB), 1)
    acc = jnp.zeros(s_ref.shape[1:], F32)
    per_dot = 2 * LANES // MOE_SUB
    for g in range(MOE_SLOTS // per_dot):
        pieces = []
        for k in range(g * per_dot, (g + 1) * per_dot):
            win_copy(step, k, slot).wait()
            j = step * MOE_SLOTS + k
            held = jnp.where((lane >= lo_ref[j]) & (lane < hi_ref[j]), lane + base_ref[j], -1)
            pieces.append(jnp.where(pos1 == held, 1.0, 0.0) + jnp.where(pos2 == held, 1.0, 0.0))
        oh = jnp.concatenate(pieces, axis=1).astype(BF16)
        rows = slice(g * per_dot * MOE_SUB, (g + 1) * per_dot * MOE_SUB)
        acc = acc + _dot(oh, buf[slot, rows])
    out = s_ref[0] + mod_ref[0, 0][5:6] * _rms(acc, g_ref[...])

    @pl.when(pl.program_id(1) < nkeep)
    def _():
        o_ref[0] = out


def _moe_combine(s, ys, plan, modtab, g, nlat_t, latent_only):
    b, l, d = s.shape
    nj = l // TB
    pos_cols, base, lo_off, hi_off = plan[5:]
    nkeep = nlat_t if latent_only else nj
    row = pl.BlockSpec((1, TB, d), lambda i, j, *_: (i, j, 0))
    return pl.pallas_call(
        functools.partial(_combine_kernel, nj=nj, nsteps=b * nj, nkeep=nkeep),
        grid_spec=pltpu.PrefetchScalarGridSpec(
            num_scalar_prefetch=3,
            grid=(b, nj),
            in_specs=[row,
                      pl.BlockSpec((TB, 2), lambda i, j, *_: (i * nj + j, 0)),
                      pl.BlockSpec((1, 1, 6, d), lambda i, j, *_: (i, jnp.where(j < nlat_t, 1, 0), 0, 0)),
                      pl.BlockSpec(g.shape, lambda i, j, *_: (0, 0)),
                      pl.BlockSpec(memory_space=pl.ANY)],
            out_specs=pl.BlockSpec((1, TB, d), lambda i, j, *_: (i, jnp.minimum(j, nkeep - 1), 0)),
            scratch_shapes=[pltpu.VMEM((2, MOE_SLOTS * MOE_SUB, d), BF16),
                            pltpu.SemaphoreType.DMA((2, MOE_SLOTS))]),
        out_shape=jax.ShapeDtypeStruct((b, nkeep * TB, d), F32),
        compiler_params=_cp("arbitrary", "arbitrary"),
        name="moe_combine",
    )(base, lo_off, hi_off, s, pos_cols, modtab, g, ys)


def _rope_tables(seq, dim):
    t = jnp.arange(seq)
    rows = (t // GRID_W).astype(F32)
    cols = (t % GRID_W).astype(F32)
    quarter = dim // 4
    inv = ROPE_BASE ** (-jnp.arange(quarter, dtype=F32) / quarter)
    ang = jnp.concatenate([rows[:, None] * inv, cols[:, None] * inv], axis=-1)
    c, s = jnp.cos(ang), jnp.sin(ang)
    rep = LANES // dim
    return jnp.tile(jnp.concatenate([c, c], -1), (1, rep)), jnp.tile(jnp.concatenate([-s, s], -1), (1, rep))


def _block_diag(x):
    g, r, c = x.shape[-3:]
    y = jnp.einsum('...grc,gh->...grhc', x, jnp.eye(g, dtype=x.dtype))
    return y.reshape(x.shape[:-3] + (g * r, g * c))


def _s5_params(lam_re, lam_im, log_dt, b_re, b_im, c_re, c_im, nb):
    lam = lax.complex(lam_re, lam_im)
    a_bar = jnp.exp(lam * jnp.exp(log_dt)[..., None])
    b_bar = ((a_bar - 1.0) / lam)[..., None] * lax.complex(b_re, b_im)
    bt = jnp.swapaxes(b_bar, -1, -2)
    bmat = jnp.concatenate([_block_diag(jnp.real(bt)), _block_diag(jnp.imag(bt))], axis=-1).astype(BF16)
    cmat = jnp.concatenate([_block_diag(jnp.swapaxes(c_re, -1, -2)),
                            -_block_diag(jnp.swapaxes(c_im, -1, -2))], axis=-2).astype(BF16)
    ns = a_bar.shape[1] * a_bar.shape[2]
    a_re = jnp.broadcast_to(jnp.real(a_bar).reshape(2, 1, ns), (2, nb, ns))
    a_im = jnp.broadcast_to(jnp.imag(a_bar).reshape(2, 1, ns), (2, nb, ns))
    return bmat, a_re, a_im, cmat


def kernel(x, c, ctx, c_ctx, mod_w, mod_b, norm_g, ev_w_in, ev_conv_w, ev_conv_b, lru_gate_w, lru_gate_b, lru_lam, ret_theta, ret_gn, ev_w_out, ffd_w_gate, ffd_w_up, ffd_w_down, od_w_in, s5_lam_re, s5_lam_im, s5_log_dt, s5_b_re, s5_b_im, s5_c_re, s5_c_im, s5_d, s5_glu_w, s5_glu_b, diff_lam, diff_gn, od_w_out, moe_router_w, moe_router_b, moe_w_gate, moe_w_up, moe_w_down):
    b, seq, d = x.shape
    nctx = ctx.shape[1]
    l = seq + nctx
    n = b * l
    depth = mod_w.shape[0]
    nlat_t = seq // TB
    assert b == SUBLANES and seq % TB == 0 and nctx % TB == 0 and seq % nctx == 0 and l % FFN_TM == 0

    s = jnp.concatenate([x, ctx], axis=1)
    cc = jnp.concatenate([c, c_ctx[None], jnp.zeros((2 * SUBLANES - b - 1, d), F32)], axis=0)
    mt = _modtab(cc, mod_w, mod_b).reshape(depth, 2 * SUBLANES, 6, d)
    modtab = jnp.stack([jnp.broadcast_to(mt[:, b:b + 1], (depth, b, 6, d)), mt[:, :b]], axis=2)

    cos_r, sin_r = _rope_tables(seq, RET_D)
    cos_d, sin_d = _rope_tables(seq, DIFF_DH)

    for layer in range(depth):
        i = layer // 2
        mtab = modtab[layer]
        ng = norm_g[layer]
        if layer % 2 == 0:
            lw = ev_conv_w.shape[-1]
            u3d, zb = _inproj(functools.partial(_inproj_even_kernel, nlat=nlat_t, kscale=RET_D ** -0.5),
                              s, mtab, ng[0:1], ev_w_in[i].astype(BF16), cos_r, sin_r, lw, 5 * lw, nlat_t,
                              "inproj_even")
            gw = _block_diag(lru_gate_w[i]).astype(BF16)
            lf, lb = _lru(u3d.reshape(l * b, lw), ev_conv_w[i].reshape(CONV_W, 1, lw), ev_conv_b[i].reshape(1, lw),
                          gw, lru_gate_b[i].reshape(2, 2, 1, lw), lru_lam[i].reshape(2, 1, lw), b, seq, nctx)
            rf, rb = _retention(jax.nn.log_sigmoid(ret_theta[i].astype(F32)), zb, nlat_t)
            s, h2 = _outproj_even(lf, lb, zb, rf, rb, ret_gn[i].reshape(1, -1),
                                  ev_w_out[i].astype(BF16), s, mtab, ng, nlat_t)
            s = _ffn(h2.reshape(n, d), ffd_w_gate[i].astype(BF16), ffd_w_up[i].astype(BF16),
                     ffd_w_down[i].astype(BF16), s, mtab, ng[3:4], seq)
        else:
            sw = s5_d.shape[-1]
            aw = DIFF_HEADS * 2 * DIFF_DH
            lam_init = 0.8 - 0.6 * math.exp(-0.3 * layer)
            u3d, zb, ub = _inproj(functools.partial(_inproj_odd_kernel, nlat=nlat_t, qscale=DIFF_DH ** -0.5 * math.log2(math.e), s5w=sw),
                                  s, mtab, ng[0:1], od_w_in[i].astype(BF16), cos_d, sin_d, sw, 3 * aw, nlat_t,
                                  "inproj_odd", u_rows=True)
            bmat, a_re, a_im, cmat = _s5_params(s5_lam_re[i], s5_lam_im[i], s5_log_dt[i], s5_b_re[i], s5_b_im[i],
                                                s5_c_re[i], s5_c_im[i], b)
            yf, yb = _s5(u3d.reshape(l * b, sw), bmat, a_re, a_im, cmat, b, seq, nctx)
            lp = diff_lam[i].astype(F32)
            lam = (jnp.exp(jnp.sum(lp[0] * lp[1])) - jnp.exp(jnp.sum(lp[2] * lp[3])) + lam_init).reshape(1)
            att = _attention(lam, zb, diff_gn[i], seq, 1.0 - lam_init)
            rw = jnp.zeros((d, LANES), F32).at[:, :N_EXPERTS].set(moe_router_w[i])
            rbias = jnp.full((1, LANES), NEG, F32).at[0, :N_EXPERTS].set(moe_router_b[i])
            s, h2x, gates = _outproj_odd(yf, yb, ub, att,
                                        s5_d[i].reshape(1, sw), s5_glu_w[i].astype(BF16), s5_glu_b[i].reshape(1, sw),
                                        od_w_out[i].astype(BF16), s, mtab, ng, rw, rbias, nlat_t)
            n_tiles = (2 * n) // MOE_TM + N_EXPERTS + 2
            plan = _moe_plan(gates.reshape(n, LANES), n_tiles)
            ys = _moe(h2x.reshape(n, d + LANES), plan, _to_bf16(moe_w_gate, i), _to_bf16(moe_w_up, i),
                      _to_bf16(moe_w_down, i), n_tiles)
            s = _moe_combine(s, ys, plan, mtab, ng[3:4], nlat_t, latent_only=layer == depth - 1)
    return s[:, :seq]
```

```python
import functools
import math

import jax
import jax.numpy as jnp
from jax import lax
from jax.experimental import pallas as pl
from jax.experimental.pallas import tpu as pltpu

F32 = jnp.float32
BF16 = jnp.bfloat16

EPS = 1e-6
ROPE_BASE = 10000.0
GRID_W = 64
LRU_C = 8.0
LRU_BLOCKS = 8
CONV_W = 4
CONV_LEFT = 2
RET_HEADS = 4
RET_D = 128
S5_GROUP = 16
S5_STATE = 64
DIFF_HEADS = 6
DIFF_DH = 64
N_EXPERTS = 8
LANES = 128
SUBLANES = 8

TB = 256
LRU_TT = 64
S5_TT = 32
ATTN_TQ = 1024
ATTN_RB = 128
FFN_TM = 768
FFN_TF = 512
CAST_BLOCK_ELEMS = 1024 * 1024
MOE_TM = 256
MOE_ALIGN = 16
MOE_SUB = 128
MOE_SUBS_PER_WIN = (TB + MOE_ALIGN - 1 + MOE_SUB - 1) // MOE_SUB
MOE_SLOTS = (N_EXPERTS * (MOE_ALIGN - 1 + MOE_SUB - 1) + 2 * TB) // MOE_SUB
MOE_GROUP = 6
VMEM_LIMIT = 56 * 1024 * 1024
NEG = -1e30


def _cp(*sem):
    return pltpu.CompilerParams(dimension_semantics=sem, vmem_limit_bytes=VMEM_LIMIT)


def _dot(a, b):
    return jnp.dot(a, b, preferred_element_type=F32)


def _split(x):
    hi = x.astype(BF16)
    lo = (x - hi.astype(F32)).astype(BF16)
    return hi, lo


def _dot3(a, w):
    ah, al = _split(a)
    wh, wl = _split(w)
    return _dot(ah, wh) + (_dot(ah, wl) + _dot(al, wh))


def _rms(x, g):
    return x * lax.rsqrt(jnp.mean(x * x, axis=-1, keepdims=True) + EPS) * g


def _sigmoid(x):
    return 0.5 * jnp.tanh(0.5 * x) + 0.5


def _silu(x):
    return x * _sigmoid(x)


def _gelu(x):
    return 0.5 * x * (1.0 + jnp.tanh(math.sqrt(2.0 / math.pi) * (x + 0.044715 * (x * x * x))))


def _softplus(x):
    return jnp.maximum(x, 0.0) + jnp.log(1.0 + jnp.exp(-jnp.abs(x)))


def _modtab_kernel(c_ref, w_ref, b_ref, o_ref):
    c = c_ref[...]
    o_ref[0] = _dot3(_silu(c), w_ref[0]) + b_ref[0]


def _modtab(cc, mod_w, mod_b):
    depth, d, n = mod_w.shape
    tn = 1536
    return pl.pallas_call(
        _modtab_kernel,
        grid=(depth, n // tn),
        in_specs=[pl.BlockSpec((cc.shape[0], d), lambda l, j: (0, 0)),
                  pl.BlockSpec((1, d, tn), lambda l, j: (l, 0, j)),
                  pl.BlockSpec((1, 1, tn), lambda l, j: (l, 0, j))],
        out_specs=pl.BlockSpec((1, cc.shape[0], tn), lambda l, j: (l, 0, j)),
        out_shape=jax.ShapeDtypeStruct((depth, cc.shape[0], n), F32),
        compiler_params=_cp("arbitrary", "arbitrary"),
        name="modtab",
    )(cc, mod_w, mod_b.reshape(depth, 1, n))


def _norm_mod_in(s_ref, mod_ref, g_ref):
    mod = mod_ref[0, 0]
    h = _rms(s_ref[0], g_ref[...]) * (1.0 + mod[1:2]) + mod[0:1]
    return h.astype(BF16)


def _inproj_even_kernel(s_ref, mod_ref, g_ref, w_ref, cos_ref, sin_ref, u_ref, zb_ref, *, nlat, kscale):
    j = pl.program_id(0)
    hb = _norm_mod_in(s_ref, mod_ref, g_ref)
    lw = RET_HEADS * RET_D

    def proj(i):
        return _dot(hb, w_ref[:, i * lw:(i + 1) * lw])

    u_ref[:, pl.ds(pl.program_id(1), 1), :] = proj(0)[:, None, :]
    zb_ref[0, :, 0:lw] = proj(1).astype(BF16)
    lat = j < nlat
    cos = jnp.where(lat, cos_ref[...], 1.0)
    sin = jnp.where(lat, sin_ref[...], 0.0)
    for sec, scale in ((0, 1.0), (1, kscale)):
        t = proj(2 + sec)
        for hd in range(RET_HEADS):
            th = t[:, hd * RET_D:(hd + 1) * RET_D]
            th = (th * cos + pltpu.roll(th, RET_D // 2, 1) * sin) * scale
            zb_ref[0, :, (1 + sec) * lw + hd * RET_D:(1 + sec) * lw + (hd + 1) * RET_D] = th.astype(BF16)
    zb_ref[0, :, 3 * lw:4 * lw] = proj(4).astype(BF16)
    zb_ref[0, :, 4 * lw:5 * lw] = proj(5).astype(BF16)


def _inproj_odd_kernel(s_ref, mod_ref, g_ref, w_ref, cos_ref, sin_ref, u_ref, zb_ref, ub_ref, *, nlat, qscale, s5w):
    j = pl.program_id(0)
    hb = _norm_mod_in(s_ref, mod_ref, g_ref)
    aw = DIFF_HEADS * 2 * DIFF_DH
    u = _dot(hb, w_ref[:, 0:s5w])
    u_ref[:, pl.ds(pl.program_id(1), 1), :] = u[:, None, :]
    ub_ref[0] = u
    lat = j < nlat
    cos = jnp.where(lat, cos_ref[...], 1.0)
    sin = jnp.where(lat, sin_ref[...], 0.0)
    lane = lax.broadcasted_iota(jnp.int32, (TB, LANES), 1)
    first_half = (lane % DIFF_DH) < (DIFF_DH // 2)
    for sec, scale in ((0, qscale), (1, 1.0)):
        t = _dot(hb, w_ref[:, s5w + sec * aw:s5w + (sec + 1) * aw])
        for hd in range(DIFF_HEADS):
            th = t[:, hd * LANES:(hd + 1) * LANES]
            partner = jnp.where(first_half, pltpu.roll(th, LANES - DIFF_DH // 2, 1), pltpu.roll(th, DIFF_DH // 2, 1))
            th = (th * cos + partner * sin) * scale
            zb_ref[0, :, sec * aw + hd * LANES:sec * aw + (hd + 1) * LANES] = th.astype(BF16)
    zb_ref[0, :, 2 * aw:3 * aw] = _dot(hb, w_ref[:, s5w + 2 * aw:s5w + 3 * aw]).astype(BF16)


def _inproj(body, s, modtab, g, w, cos, sin, uw, zw, nlat, name, u_rows=False):
    b, l, d = s.shape
    nj = l // TB
    return pl.pallas_call(
        body,
        grid=(nj, b),
        in_specs=[pl.BlockSpec((1, TB, d), lambda j, i: (i, j, 0)),
                  pl.BlockSpec((1, 1, 6, d), lambda j, i: (i, jnp.where(j < nlat, 1, 0), 0, 0)),
                  pl.BlockSpec((1, d), lambda j, i: (0, 0)),
                  pl.BlockSpec(w.shape, lambda j, i: (0, 0)),
                  pl.BlockSpec((TB, LANES), lambda j, i: (jnp.minimum(j, nlat - 1), 0)),
                  pl.BlockSpec((TB, LANES), lambda j, i: (jnp.minimum(j, nlat - 1), 0))],
        out_specs=[pl.BlockSpec((TB, b, uw), lambda j, i: (j, 0, 0)),
                   pl.BlockSpec((1, TB, zw), lambda j, i: (i, j, 0))]
        + ([pl.BlockSpec((1, TB, uw), lambda j, i: (i, j, 0))] if u_rows else []),
        out_shape=[jax.ShapeDtypeStruct((l, b, uw), F32),
                   jax.ShapeDtypeStruct((b, l, zw), BF16)]
        + ([jax.ShapeDtypeStruct((b, l, uw), F32)] if u_rows else []),
        compiler_params=_cp("arbitrary", "arbitrary"),
        name=name,
    )(s, modtab, g, w, cos, sin)


def _fwd_chunk(s, nlat_c, nctx_c):
    return jnp.where(s < nctx_c, nlat_c + s, s - nctx_c)


def _bwd_chunk(s, nlat_c, nctx_c):
    return nlat_c + nctx_c - 1 - s


def _lru_kernel(uf_ref, ufl_ref, ufr_ref, ub_ref, ubl_ref, ubr_ref, cw_ref, cb_ref, gw_ref, gb_ref, lam_ref,
                of_ref, ob_ref, pad_scr, a_scr, b_scr, h_scr, o_scr, *, tt, nlat_c, nctx_c, nb):
    s = pl.program_id(0)
    nc = nlat_c + nctx_c

    @pl.when(s == 0)
    def _():
        h_scr[...] = jnp.zeros_like(h_scr)

    rows = tt * nb
    views = ((uf_ref, ufl_ref, ufr_ref, _fwd_chunk(s, nlat_c, nctx_c)),
             (ub_ref, ubl_ref, ubr_ref, _bwd_chunk(s, nlat_c, nctx_c)))
    for d, (u_ref, l_ref, r_ref, c) in enumerate(views):
        first = (c == 0) | (c == nlat_c)
        last = (c == nlat_c - 1) | (c == nc - 1)
        pad_scr[0:CONV_LEFT * nb] = jnp.where(first, 0.0, l_ref[...])
        pad_scr[CONV_LEFT * nb:CONV_LEFT * nb + rows] = u_ref[...]
        pad_scr[CONV_LEFT * nb + rows:(CONV_W - 1) * nb + rows] = jnp.where(last, 0.0, r_ref[...])
        uc = cb_ref[...] + pad_scr[0:rows] * cw_ref[0]
        for k in range(1, CONV_W):
            uc = uc + pad_scr[k * nb:k * nb + rows] * cw_ref[k]
        ucb = uc.astype(BF16)
        r = _sigmoid(_dot(ucb, gw_ref[d, 0]) + gb_ref[d, 0])
        i = _sigmoid(_dot(ucb, gw_ref[d, 1]) + gb_ref[d, 1])
        a = jnp.exp((-LRU_C * _softplus(-lam_ref[d])) * r)
        a_scr[d] = a
        b_scr[d] = jnp.sqrt(1.0 - a * a) * (i * uc)

    def body(t, carry):
        hf, hb = carry
        rf = pl.multiple_of(t * nb, nb)
        rb = pl.multiple_of((tt - 1 - t) * nb, nb)
        hf = a_scr[0, pl.ds(rf, nb), :] * hf + b_scr[0, pl.ds(rf, nb), :]
        hb = a_scr[1, pl.ds(rb, nb), :] * hb + b_scr[1, pl.ds(rb, nb), :]
        for g in range(ng):
            o_scr[0, g, pl.ds(rf, nb), :] = hf[:, g * LANES:(g + 1) * LANES]
            o_scr[1, g, pl.ds(rb, nb), :] = hb[:, g * LANES:(g + 1) * LANES]
        return hf, hb

    ng = o_scr.shape[1]
    hf, hb = lax.fori_loop(0, tt, body, (h_scr[0], h_scr[1]), unroll=True)
    h_scr[0] = hf
    h_scr[1] = hb
    for bi in range(nb):
        for g in range(ng):
            of_ref[bi, :, g * LANES:(g + 1) * LANES] = o_scr[0, g, pl.ds(bi, tt, stride=nb), :]
            ob_ref[bi, :, g * LANES:(g + 1) * LANES] = o_scr[1, g, pl.ds(bi, tt, stride=nb), :]


def _lru(u_tm, conv_w, conv_b, gate_w, gate_b, lam, nb, nlat, nctx):
    rows_total, c = u_tm.shape
    tt = LRU_TT
    nlat_c, nctx_c = nlat // tt, nctx // tt
    nc = nlat_c + nctx_c
    rows = tt * nb
    fwd = functools.partial(_fwd_chunk, nlat_c=nlat_c, nctx_c=nctx_c)
    bwd = functools.partial(_bwd_chunk, nlat_c=nlat_c, nctx_c=nctx_c)
    lrows, rrows = CONV_LEFT * nb, (CONV_W - 1 - CONV_LEFT) * nb
    nl, nr = rows // lrows, rows // rrows

    def cur(f):
        return pl.BlockSpec((rows, c), lambda s: (f(s), 0))

    def left(f):
        return pl.BlockSpec((lrows, c), lambda s: (jnp.maximum(f(s) * nl - 1, 0), 0))

    def right(f):
        return pl.BlockSpec((rrows, c), lambda s: (jnp.minimum((f(s) + 1) * nr, nc * nr - 1), 0))

    def whole(a):
        return pl.BlockSpec(a.shape, lambda s: (0,) * a.ndim)

    return pl.pallas_call(
        functools.partial(_lru_kernel, tt=tt, nlat_c=nlat_c, nctx_c=nctx_c, nb=nb),
        grid=(nc,),
        in_specs=[cur(fwd), left(fwd), right(fwd), cur(bwd), left(bwd), right(bwd),
                  whole(conv_w), whole(conv_b), whole(gate_w), whole(gate_b), whole(lam)],
        out_specs=[pl.BlockSpec((nb, tt, c), lambda s: (0, fwd(s), 0)),
                   pl.BlockSpec((nb, tt, c), lambda s: (0, bwd(s), 0))],
        out_shape=[jax.ShapeDtypeStruct((nb, rows_total // nb, c), F32)] * 2,
        scratch_shapes=[pltpu.VMEM((rows + (CONV_W - 1) * nb, c), F32),
                        pltpu.VMEM((2, rows, c), F32),
                        pltpu.VMEM((2, rows, c), F32),
                        pltpu.VMEM((2, nb, c), F32),
                        pltpu.VMEM((2, c // LANES, rows, LANES), F32)],
        compiler_params=_cp("arbitrary"),
        name="rglru",
    )(u_tm, u_tm, u_tm, u_tm, u_tm, u_tm, conv_w, conv_b, gate_w, gate_b, lam)


def _ret_kernel(lg_ref, qf_ref, kf_ref, vf_ref, qb_ref, kb_ref, vb_ref, of_ref, ob_ref, s_scr, d_scr, *, c):
    s = pl.program_id(1)
    ii = lax.broadcasted_iota(jnp.int32, (c, c), 0).astype(F32)
    jj = lax.broadcasted_iota(jnp.int32, (c, c), 1).astype(F32)

    @pl.when(s == 0)
    def _():
        s_scr[...] = jnp.zeros_like(s_scr)
        for d in range(2):
            diff = (ii - jj) if d == 0 else (jj - ii)
            for h in range(RET_HEADS):
                d_scr[d, h] = jnp.where(diff >= 0, jnp.exp(jnp.maximum(diff, 0.0) * lg_ref[d, h]), 0.0)

    ri = lax.broadcasted_iota(jnp.int32, (c, 1), 0).astype(F32)
    views = ((qf_ref, kf_ref, vf_ref, of_ref), (qb_ref, kb_ref, vb_ref, ob_ref))
    for d, (q_ref, k_ref, v_ref, o_ref) in enumerate(views):
        for h in range(RET_HEADS):
            lg = lg_ref[d, h]
            cols = slice(h * RET_D, (h + 1) * RET_D)
            q, k, v = q_ref[0, :, cols], k_ref[0, :, cols], v_ref[0, :, cols]
            sc = lax.dot_general(q, k, (((1,), (1,)), ((), ())), preferred_element_type=F32) * d_scr[d, h]
            inner = _dot(sc.astype(BF16), v)
            st = s_scr[d, h]
            qdec = jnp.exp(((ri + 1.0) if d == 0 else (c - ri)) * lg)
            cross = _dot(q, st.astype(BF16)) * qdec
            o_ref[0, :, cols] = inner + cross
            kdec = jnp.exp(((c - 1.0 - ri) if d == 0 else ri) * lg)
            kd = (k.astype(F32) * kdec).T.astype(BF16)
            gc = jnp.exp(jnp.zeros((RET_D, RET_D), F32) + c * lg)
            s_scr[d, h] = gc * st + _dot(kd, v)


def _retention(log_g, zb, nlat):
    b, l, _ = zb.shape
    nj = l // TB
    w = RET_HEADS * RET_D

    def fwd(s):
        return jnp.where(s < nj - nlat, nlat + s, s - (nj - nlat))

    def bwd(s):
        return nj - 1 - s

    def col(f, i):
        return pl.BlockSpec((1, TB, w), lambda bi, s: (bi, f(s), i))

    def out(f):
        return pl.BlockSpec((1, TB, w), lambda bi, s: (bi, f(s), 0))

    return pl.pallas_call(
        functools.partial(_ret_kernel, c=TB),
        grid=(b, nj),
        in_specs=[pl.BlockSpec(memory_space=pltpu.SMEM),
                  col(fwd, 1), col(fwd, 2), col(fwd, 3), col(bwd, 1), col(bwd, 2), col(bwd, 3)],
        out_specs=[out(fwd), out(bwd)],
        out_shape=[jax.ShapeDtypeStruct((b, l, w), F32)] * 2,
        scratch_shapes=[pltpu.VMEM((2, RET_HEADS, RET_D, RET_D), F32),
                        pltpu.VMEM((2, RET_HEADS, TB, TB), F32)],
        compiler_params=_cp("arbitrary", "arbitrary"),
        name="retention",
    )(log_g, zb, zb, zb, zb, zb, zb)


def _s5_kernel(uf_ref, ub_ref, bm_ref, ar_ref, ai_ref, cm_ref, yf_ref, yb_ref, bu_scr, h_scr, y_scr, *, tt, nb, ns):
    s = pl.program_id(0)

    @pl.when(s == 0)
    def _():
        h_scr[...] = jnp.zeros_like(h_scr)

    cg = 512
    for d, (u_ref, y_ref) in enumerate(((uf_ref, yf_ref), (ub_ref, yb_ref))):
        bu_scr[...] = _dot(u_ref[...].astype(BF16), bm_ref[d])
        for g in range(ns // cg):
            re = slice(g * cg, (g + 1) * cg)
            im = slice(ns + g * cg, ns + (g + 1) * cg)
            ar = ar_ref[d, :, re]
            ai = ai_ref[d, :, re]

            def body(i, carry, re=re, im=im, ar=ar, ai=ai, d=d):
                hr, hi = carry
                t = i if d == 0 else tt - 1 - i
                r0 = pl.multiple_of(t * nb, nb)
                nr = ar * hr - ai * hi + bu_scr[pl.ds(r0, nb), re]
                ni = ar * hi + ai * hr + bu_scr[pl.ds(r0, nb), im]
                bu_scr[pl.ds(r0, nb), re] = nr
                bu_scr[pl.ds(r0, nb), im] = ni
                return nr, ni

            hr, hi = lax.fori_loop(0, tt, body, (h_scr[d, :, re], h_scr[d, :, im]), unroll=True)
            h_scr[d, :, re] = hr
            h_scr[d, :, im] = hi
        y = _dot(bu_scr[...].astype(BF16), cm_ref[d])
        for g in range(y_scr.shape[0]):
            y_scr[g] = y[:, g * LANES:(g + 1) * LANES]
        for bi in range(nb):
            for g in range(y_scr.shape[0]):
                y_ref[bi, :, g * LANES:(g + 1) * LANES] = y_scr[g, pl.ds(bi, tt, stride=nb), :]


def _s5(u_tm, bmat, a_re, a_im, cmat, nb, nlat, nctx):
    rows_total, c = u_tm.shape
    tt = S5_TT
    nlat_c, nctx_c = nlat // tt, nctx // tt
    rows = tt * nb
    ns = a_re.shape[-1]
    fwd = functools.partial(_fwd_chunk, nlat_c=nlat_c, nctx_c=nctx_c)
    bwd = functools.partial(_bwd_chunk, nlat_c=nlat_c, nctx_c=nctx_c)

    def cur(f):
        return pl.BlockSpec((rows, c), lambda s: (f(s), 0))

    def whole(a):
        return pl.BlockSpec(a.shape, lambda s: (0,) * a.ndim)

    return pl.pallas_call(
        functools.partial(_s5_kernel, tt=tt, nb=nb, ns=ns),
        grid=(nlat_c + nctx_c,),
        in_specs=[cur(fwd), cur(bwd), whole(bmat), whole(a_re), whole(a_im), whole(cmat)],
        out_specs=[pl.BlockSpec((nb, tt, c), lambda s: (0, fwd(s), 0)),
                   pl.BlockSpec((nb, tt, c), lambda s: (0, bwd(s), 0))],
        out_shape=[jax.ShapeDtypeStruct((nb, rows_total // nb, c), F32)] * 2,
        scratch_shapes=[pltpu.VMEM((rows, 2 * ns), F32), pltpu.VMEM((2, nb, 2 * ns), F32),
                        pltpu.VMEM((c // LANES, rows, LANES), F32)],
        compiler_params=_cp("arbitrary"),
        name="s5",
    )(u_tm, u_tm, bmat, a_re, a_im, cmat)


def _attn_kernel(lam_ref, q_ref, k_ref, v_ref, gn_ref, o_ref, va_scr, *, out_scale):
    tq, tk = q_ref.shape[1], k_ref.shape[1]
    q = q_ref[0]
    lane = lax.broadcasted_iota(jnp.int32, (tq, LANES), 1)
    zero = jnp.zeros_like(q)
    qm = (jnp.where(lane < DIFF_DH, q, zero), jnp.where(lane >= DIFF_DH, q, zero))
    kc = k_ref[0]

    @pl.when(pl.program_id(2) == 0)
    def _():
        va_scr[:, :LANES] = v_ref[0]
        va_scr[:, LANES:] = jnp.ones((tk, LANES), BF16)

    va = va_scr[...]
    rb = min(ATTN_RB, tq)
    for r in range(tq // rb):
        rows = slice(r * rb, (r + 1) * rb)
        outs = []
        for m in range(2):
            sc = lax.dot_general(qm[m][rows], kc, (((1,), (1,)), ((), ())), preferred_element_type=F32)
            p = jnp.exp2(sc - jnp.max(sc, axis=1, keepdims=True)).astype(BF16)
            a = _dot(p, va)
            outs.append(a[:, :LANES] / a[:, LANES:LANES + 1])
        att = outs[0] - lam_ref[0] * outs[1]
        att = att * lax.rsqrt(jnp.mean(att * att, axis=-1, keepdims=True) + EPS) * (gn_ref[0] * out_scale)
        o_ref[0, rows] = att.astype(BF16)


def _attention(lam, zb, gn, nlat, out_scale):
    b, l, _ = zb.shape
    h = DIFF_HEADS
    nctx = l - nlat
    tq = ATTN_TQ if nlat % ATTN_TQ == 0 else TB
    gn3 = gn.reshape(h, 1, LANES)

    def call(tq_, nq, kv_rows, q0, kv0, name):
        return pl.pallas_call(
            functools.partial(_attn_kernel, out_scale=out_scale),
            grid=(b, h, nq),
            in_specs=[pl.BlockSpec(memory_space=pltpu.SMEM),
                      pl.BlockSpec((1, tq_, LANES), lambda bi, hi, qi: (bi, q0 + qi, hi)),
                      pl.BlockSpec((1, kv_rows, LANES), lambda bi, hi, qi: (bi, kv0, h + hi)),
                      pl.BlockSpec((1, kv_rows, LANES), lambda bi, hi, qi: (bi, kv0, 2 * h + hi)),
                      pl.BlockSpec((1, 1, LANES), lambda bi, hi, qi: (hi, 0, 0))],
            out_specs=pl.BlockSpec((1, tq_, LANES), lambda bi, hi, qi: (bi, qi, hi)),
            out_shape=jax.ShapeDtypeStruct((b, nq * tq_, h * LANES), BF16),
            scratch_shapes=[pltpu.VMEM((kv_rows, 2 * LANES), BF16)],
            compiler_params=_cp("arbitrary", "arbitrary", "arbitrary"),
            name=name,
        )(lam, zb, zb, zb, gn3)

    att_x = call(tq, nlat // tq, l, 0, 0, "diffattn_latent")
    att_c = call(nctx, 1, nctx, nlat // nctx, nlat // nctx, "diffattn_context")
    return jnp.concatenate([att_x, att_c], axis=1)


def _out_tail(parts, w_ref, s_ref, mod, ng_ref, xo_ref):
    y, r0 = None, 0
    for p in parts:
        n = p.shape[1]
        t = _dot(p.astype(BF16), w_ref[r0:r0 + n, :])
        y = t if y is None else y + t
        r0 += n
    xn = s_ref[0] + mod[2:3] * _rms(y, ng_ref[1:2])
    xo_ref[0] = xn
    return _rms(xn, ng_ref[2:3]) * (1.0 + mod[4:5]) + mod[3:4]


def _outproj_even_kernel(lf_ref, lb_ref, zg_ref, zo_ref, rf_ref, rb_ref, gn_ref, w_ref, s_ref, mod_ref, ng_ref,
                         xo_ref, h2_ref):
    parts = [(lf_ref[0] + lb_ref[0]) * _gelu(zg_ref[0].astype(F32))]
    r = rf_ref[0] + rb_ref[0]
    og = _silu(zo_ref[0].astype(F32))
    for h in range(RET_HEADS):
        cols = slice(h * RET_D, (h + 1) * RET_D)
        y = r[:, cols]
        y = y - jnp.mean(y, axis=-1, keepdims=True)
        y = y * lax.rsqrt(jnp.mean(y * y, axis=-1, keepdims=True) + EPS) * gn_ref[:, cols]
        parts.append(y * og[:, cols])
    h2 = _out_tail(parts, w_ref, s_ref, mod_ref[0, 0], ng_ref, xo_ref)
    h2_ref[0] = h2.astype(BF16)


def _outproj_odd_kernel(yf_ref, yb_ref, u_ref, att_ref, d_ref, gw_ref, gb_ref, w_ref, s_ref, mod_ref, ng_ref,
                        rw_ref, rb_ref, xo_ref, h2_ref, gates_ref):
    z = _gelu(yf_ref[0] + yb_ref[0] + d_ref[...] * u_ref[0])
    s5o = z * _sigmoid(_dot(z.astype(BF16), gw_ref[...]) + gb_ref[...])
    h2 = _out_tail([s5o, att_ref[0]], w_ref, s_ref, mod_ref[0, 0], ng_ref, xo_ref)
    d = h2.shape[1]
    h2_ref[0, :, :d] = h2.astype(BF16)
    logits = _dot3(h2, rw_ref[...]) + rb_ref[...]
    lane = lax.broadcasted_iota(jnp.int32, logits.shape, 1).astype(F32)
    m1 = jnp.max(logits, axis=1, keepdims=True)
    i1 = jnp.min(jnp.where(logits == m1, lane, float(LANES)), axis=1, keepdims=True)
    rest = jnp.where(lane == i1, 2.0 * NEG, logits)
    m2 = jnp.max(rest, axis=1, keepdims=True)
    i2 = jnp.min(jnp.where(rest == m2, lane, float(LANES)), axis=1, keepdims=True)
    e = jnp.exp(m2 - m1)
    p1 = 1.0 / (1.0 + e)
    gates = jnp.where(lane == i1, p1, 0.0) + jnp.where(lane == i2, e * p1, 0.0)
    ids = jnp.where(lane == float(N_EXPERTS), i1, 0.0) + jnp.where(lane == float(N_EXPERTS + 1), i2, 0.0)
    gates_ref[0] = gates + ids
    ghi = gates.astype(BF16).astype(F32)
    glo = (gates - ghi).astype(BF16).astype(F32)
    h2_ref[0, :, d:] = (ghi + pltpu.roll(glo, N_EXPERTS, 1)).astype(BF16)


def _whole2(a):
    return pl.BlockSpec(a.shape, lambda i, j: (0,) * a.ndim)


def _mod_spec(d, nlat_t):
    return pl.BlockSpec((1, 1, 6, d), lambda i, j: (i, jnp.where(j < nlat_t, 1, 0), 0, 0))


def _outproj_even(lf, lb, zb, rf, rb, gn, w, s, modtab, ng, nlat_t):
    b, l, d = s.shape
    cw = RET_HEADS * RET_D
    tm = pl.BlockSpec((1, TB, cw), lambda i, j: (i, j, 0))
    row = pl.BlockSpec((1, TB, d), lambda i, j: (i, j, 0))
    return pl.pallas_call(
        _outproj_even_kernel,
        grid=(b, l // TB),
        in_specs=[tm, tm,
                  pl.BlockSpec((1, TB, cw), lambda i, j: (i, j, 0)),
                  pl.BlockSpec((1, TB, cw), lambda i, j: (i, j, 4)),
                  pl.BlockSpec((1, TB, cw), lambda i, j: (i, j, 0)),
                  pl.BlockSpec((1, TB, cw), lambda i, j: (i, j, 0)),
                  _whole2(gn), _whole2(w), row, _mod_spec(d, nlat_t), _whole2(ng)],
        out_specs=[row, row],
        out_shape=[jax.ShapeDtypeStruct((b, l, d), F32), jax.ShapeDtypeStruct((b, l, d), BF16)],
        compiler_params=_cp("arbitrary", "arbitrary"),
        name="outproj_even",
    )(lf, lb, zb, zb, rf, rb, gn, w, s, modtab, ng)


def _outproj_odd(yf, yb, u, att, dskip, glu_w, glu_b, w, s, modtab, ng, rw, rb, nlat_t):
    b, l, d = s.shape
    sw = dskip.shape[1]
    aw = att.shape[2]
    tm = pl.BlockSpec((1, TB, sw), lambda i, j: (i, j, 0))
    row = pl.BlockSpec((1, TB, d), lambda i, j: (i, j, 0))
    return pl.pallas_call(
        _outproj_odd_kernel,
        grid=(b, l // TB),
        in_specs=[tm, tm, tm,
                  pl.BlockSpec((1, TB, aw), lambda i, j: (i, j, 0)),
                  _whole2(dskip), _whole2(glu_w), _whole2(glu_b), _whole2(w), row, _mod_spec(d, nlat_t),
                  _whole2(ng), _whole2(rw), _whole2(rb)],
        out_specs=[row, pl.BlockSpec((1, TB, d + LANES), lambda i, j: (i, j, 0)),
                   pl.BlockSpec((1, TB, LANES), lambda i, j: (i, j, 0))],
        out_shape=[jax.ShapeDtypeStruct((b, l, d), F32), jax.ShapeDtypeStruct((b, l, d + LANES), BF16),
                   jax.ShapeDtypeStruct((b, l, LANES), F32)],
        compiler_params=_cp("arbitrary", "arbitrary"),
        name="outproj_odd",
    )(yf, yb, u, att, dskip, glu_w, glu_b, w, s, modtab, ng, rw, rb)


def _cast_kernel(x_ref, o_ref):
    o_ref[...] = x_ref[...].astype(BF16)


def _to_bf16(w, layer):
    _, e, r, c = w.shape
    rows = max(m for m in range(TB, max(CAST_BLOCK_ELEMS // c, TB) + 1, TB) if r % m == 0)
    return pl.pallas_call(
        _cast_kernel,
        grid=(e, r // rows),
        in_specs=[pl.BlockSpec((None, 1, rows, c), lambda i, j: (layer, i, j, 0))],
        out_specs=pl.BlockSpec((1, rows, c), lambda i, j: (i, j, 0)),
        out_shape=jax.ShapeDtypeStruct((e, r, c), BF16),
        compiler_params=_cp("arbitrary", "arbitrary"),
        name="cast_bf16",
    )(w)


def _ffn_kernel(x_ref, wg_ref, wu_ref, wd_ref, s_ref, mod_ref, g_ref, o_ref, *, tpb, seq, tf):
    i = pl.program_id(0)
    x = x_ref[...]
    tm, ff = x.shape[0], wg_ref.shape[1]
    acc = None
    for f in range(ff // tf):
        cols = slice(f * tf, (f + 1) * tf)
        act = _silu(_dot(x, wg_ref[:, cols])) * _dot(x, wu_ref[:, cols])
        t = _dot(act.astype(BF16), wd_ref[cols, :])
        acc = t if acc is None else acc + t
    row = lax.rem(i, tpb) * tm + lax.broadcasted_iota(jnp.int32, (tm, 1), 0)
    mod = mod_ref[0]
    gate = jnp.where(row < seq, mod[1, 5:6], mod[0, 5:6])
    o_ref[...] = s_ref[...] + gate * _rms(acc, g_ref[...])


def _ffn(x, wg, wu, wd, s, modtab, g, seq):
    b, l, d = s.shape
    n = b * l
    tm = FFN_TM
    tpb = l // tm
    rows = pl.BlockSpec((tm, d), lambda i: (i, 0))

    def whole(a):
        return pl.BlockSpec(a.shape, lambda i: (0, 0), pipeline_mode=pl.Buffered(1))

    out = pl.pallas_call(
        functools.partial(_ffn_kernel, tpb=tpb, seq=seq, tf=FFN_TF),
        grid=(n // tm,),
        in_specs=[rows, whole(wg), whole(wu), whole(wd), rows,
                  pl.BlockSpec((1, 2, 6, d), lambda i: (i // tpb, 0, 0, 0)),
                  pl.BlockSpec(g.shape, lambda i: (0, 0))],
        out_specs=rows,
        out_shape=jax.ShapeDtypeStruct((n, d), F32),
        compiler_params=_cp("arbitrary"),
        name="dense_ffn",
    )(x, wg, wu, wd, s.reshape(n, d), modtab, g)
    return out.reshape(b, l, d)


def _moe_plan(route, n_tiles):
    n = route.shape[0]
    nb = n // TB
    ne = N_EXPERTS
    e1 = route[:, ne].astype(jnp.int32)
    e2 = route[:, ne + 1].astype(jnp.int32)
    ex = jnp.arange(ne, dtype=jnp.int32)
    sel = ((e1[:, None] == ex) | (e2[:, None] == ex)).astype(jnp.int32)
    incl = jnp.cumsum(sel, axis=0)
    count = incl[-1]
    gsize = (count + MOE_TM - 1) // MOE_TM * MOE_TM
    gend = jnp.cumsum(gsize)
    goff = gend - gsize
    posm = goff[None, :] + incl - sel
    pos1 = jnp.sum(jnp.where(e1[:, None] == ex, posm, 0), axis=1)
    pos2 = jnp.sum(jnp.where(e2[:, None] == ex, posm, 0), axis=1)
    pos_cols = jnp.stack([pos1, pos2], axis=1)
    pos_rows = jnp.stack([pos1.reshape(nb, TB), pos2.reshape(nb, TB)], axis=1)
    start = posm[::TB]
    last = (goff + count)[None]
    end = jnp.concatenate([start[1:], last], axis=0)
    r0 = jnp.arange(n_tiles, dtype=jnp.int32) * MOE_TM
    tile_valid = (r0 < gend[-1]).astype(jnp.int32)
    tile_exp = jnp.minimum(jnp.sum((r0[:, None] >= gend[None, :]).astype(jnp.int32), axis=1), ne - 1)
    st, en = start.T[tile_exp], end.T[tile_exp]
    lim = jnp.minimum(r0 + MOE_TM, last[0][tile_exp])
    tb_lo = jnp.clip(jnp.sum((en <= r0[:, None]).astype(jnp.int32), axis=1), 0, nb - 1)
    tb_hi = jnp.clip(jnp.sum((st < lim[:, None]).astype(jnp.int32), axis=1) - 1, 0, nb - 1)
    a0 = start // MOE_ALIGN * MOE_ALIGN
    lo_off = start - a0
    hi_off = lo_off + (end - start)
    k = jnp.arange(MOE_SUBS_PER_WIN, dtype=jnp.int32) * MOE_SUB
    c_valid = ((end > start)[..., None] & (hi_off[..., None] > k)).reshape(nb, -1)
    c_base = (a0[..., None] + k).reshape(nb, -1)
    c_lo = jnp.clip(lo_off[..., None] - k, 0, MOE_SUB).reshape(nb, -1)
    c_hi = jnp.clip(hi_off[..., None] - k, 0, MOE_SUB).reshape(nb, -1)
    slot = jnp.cumsum(c_valid.astype(jnp.int32), axis=1) - 1
    put = (slot[..., None] == jnp.arange(MOE_SLOTS, dtype=jnp.int32)) & c_valid[..., None]

    def place(v):
        return jnp.sum(jnp.where(put, v[..., None], 0), axis=1).reshape(-1)

    return (tile_exp, tile_valid, tb_lo, tb_hi, pos_rows, pos_cols, place(c_base), place(c_lo), place(c_hi))


def _moe_kernel(te_ref, tv_ref, lo_ref, hi_ref, pos_ref, h_hbm, wg_ref, wu_ref, wd_ref, y_ref, xbuf, xs_ref, sem,
                *, n_tiles, tf):
    i = pl.program_id(0)
    d, ff = wg_ref.shape[1], wg_ref.shape[2]
    grp = MOE_GROUP
    nb = pos_ref.shape[0]
    slot = lax.rem(i, 2)

    def first_blk(t, g):
        return jnp.minimum(lo_ref[t] + g * grp, nb - grp)

    def grp_copy(fb, sl):
        return pltpu.make_async_copy(h_hbm.at[pl.ds(pl.multiple_of(fb * TB, TB), grp * TB)], xbuf.at[sl], sem.at[sl])

    valid = tv_ref[i] == 1

    @pl.when((i == 0) & valid)
    def _():
        grp_copy(first_blk(0, 0), 0).start()

    @pl.when(valid)
    def _():
        lo, hi = lo_ref[i], hi_ref[i]
        rid = i * MOE_TM + lax.broadcasted_iota(jnp.int32, (MOE_TM, TB), 0)

        def gather(g):
            fb = first_blk(i, g)
            pieces = []
            for k in range(grp):
                tb = fb + k
                p = jnp.where((tb >= lo + g * grp) & (tb <= hi), pos_ref[tb], -1)
                pieces.append(jnp.where(rid == p[0:1], 1.0, 0.0) + jnp.where(rid == p[1:2], 1.0, 0.0))
            return _dot(jnp.concatenate(pieces, axis=1).astype(BF16), xbuf[slot])

        grp_copy(first_blk(i, 0), slot).wait()
        xs_ref[...] = gather(0)

        def more(g, c):
            cp = grp_copy(first_blk(i, g), slot)
            cp.start()
            cp.wait()
            xs_ref[...] += gather(g)
            return c

        lax.fori_loop(1, (hi - lo) // grp + 1, more, 0)
        nxt = jnp.minimum(i + 1, n_tiles - 1)

        @pl.when((i + 1 < n_tiles) & (tv_ref[nxt] == 1))
        def _():
            grp_copy(first_blk(nxt, 0), 1 - slot).start()

        xs = xs_ref[...]
        x = xs[:, :d].astype(BF16)
        ext = xs[:, d:]
        lane = lax.broadcasted_iota(jnp.int32, ext.shape, 1)
        e_t = te_ref[i]
        gate = jnp.sum(jnp.where((lane == e_t) | (lane == e_t + N_EXPERTS), ext, 0.0), axis=1, keepdims=True)
        acc = None
        for f in range(ff // tf):
            cols = slice(f * tf, (f + 1) * tf)
            act = _silu(_dot(x, wg_ref[0, :, cols])) * _dot(x, wu_ref[0, :, cols])
            t = _dot(act.astype(BF16), wd_ref[0, cols, :])
            acc = t if acc is None else acc + t
        y_ref[...] = (acc * gate).astype(BF16)

    @pl.when(jnp.logical_not(valid))
    def _():
        y_ref[...] = jnp.zeros_like(y_ref)


def _moe(h2x, plan, wg, wu, wd, n_tiles):
    n, dx = h2x.shape
    ne, d, ff = wg.shape
    tile_exp, tile_valid, tb_lo, tb_hi, pos_rows = plan[:5]

    def wspec(shape):
        return pl.BlockSpec(shape, lambda i, te, tv, lo, hi: (te[i], 0, 0), pipeline_mode=pl.Buffered(1))

    return pl.pallas_call(
        functools.partial(_moe_kernel, n_tiles=n_tiles, tf=FFN_TF),
        grid_spec=pltpu.PrefetchScalarGridSpec(
            num_scalar_prefetch=4,
            grid=(n_tiles,),
            in_specs=[pl.BlockSpec(pos_rows.shape, lambda i, te, tv, lo, hi: (0, 0, 0)),
                      pl.BlockSpec(memory_space=pl.ANY),
                      wspec((1, d, ff)), wspec((1, d, ff)), wspec((1, ff, d))],
            out_specs=pl.BlockSpec((MOE_TM, d), lambda i, te, tv, lo, hi: (i, 0)),
            scratch_shapes=[pltpu.VMEM((2, MOE_GROUP * TB, dx), BF16), pltpu.VMEM((MOE_TM, dx), F32),
                            pltpu.SemaphoreType.DMA((2,))]),
        out_shape=jax.ShapeDtypeStruct((n_tiles * MOE_TM, d), BF16),
        compiler_params=_cp("arbitrary"),
        name="moe_ffn",
    )(tile_exp, tile_valid, tb_lo, tb_hi, pos_rows, h2x, wg, wu, wd)


def _combine_kernel(base_ref, lo_ref, hi_ref, s_ref, pos_ref, mod_ref, g_ref, ys_hbm, o_ref, buf, sem,
                    *, nj, nsteps, nkeep):
    step = pl.program_id(0) * nj + pl.program_id(1)
    slot = lax.rem(step, 2)

    def win_copy(t, k, sl):
        a = pl.multiple_of(base_ref[t * MOE_SLOTS + k], MOE_ALIGN)
        return pltpu.make_async_copy(ys_hbm.at[pl.ds(a, MOE_SUB)], buf.at[sl, pl.ds(k * MOE_SUB, MOE_SUB)],
                                     sem.at[sl, k])

    @pl.when(step == 0)
    def _():
        for k in range(MOE_SLOTS):
            win_copy(0, k, 0).start()

    @pl.when(step + 1 < nsteps)
    def _():
        for k in range(MOE_SLOTS):
            win_copy(step + 1, k, 1 - slot).start()

    pos = pos_ref[...]
    pos1 = jnp.broadcast_to(pos[:, 0:1], (TB, MOE_SUB))
    pos2 = jnp.broadcast_to(pos[:, 1:2], (TB, MOE_SUB))
    lane = lax.broadcasted_iota(jnp.int32, (1, MOE_SUB), 1)
    acc = jnp.zeros(s_ref.shape[1:], F32)
    per_dot = 2 * LANES // MOE_SUB
    for g in range(MOE_SLOTS // per_dot):
        pieces = []
        for k in range(g * per_dot, (g + 1) * per_dot):
            win_copy(step, k, slot).wait()
            j = step * MOE_SLOTS + k
            held = jnp.where((lane >= lo_ref[j]) & (lane < hi_ref[j]), lane + base_ref[j], -1)
            pieces.append(jnp.where(pos1 == held, 1.0, 0.0) + jnp.where(pos2 == held, 1.0, 0.0))
        oh = jnp.concatenate(pieces, axis=1).astype(BF16)
        rows = slice(g * per_dot * MOE_SUB, (g + 1) * per_dot * MOE_SUB)
        acc = acc + _dot(oh, buf[slot, rows])
    out = s_ref[0] + mod_ref[0, 0][5:6] * _rms(acc, g_ref[...])

    @pl.when(pl.program_id(1) < nkeep)
    def _():
        o_ref[0] = out


def _moe_combine(s, ys, plan, modtab, g, nlat_t, latent_only):
    b, l, d = s.shape
    nj = l // TB
    pos_cols, base, lo_off, hi_off = plan[5:]
    nkeep = nlat_t if latent_only else nj
    row = pl.BlockSpec((1, TB, d), lambda i, j, *_: (i, j, 0))
    return pl.pallas_call(
        functools.partial(_combine_kernel, nj=nj, nsteps=b * nj, nkeep=nkeep),
        grid_spec=pltpu.PrefetchScalarGridSpec(
            num_scalar_prefetch=3,
            grid=(b, nj),
            in_specs=[row,
                      pl.BlockSpec((TB, 2), lambda i, j, *_: (i * nj + j, 0)),
                      pl.BlockSpec((1, 1, 6, d), lambda i, j, *_: (i, jnp.where(j < nlat_t, 1, 0), 0, 0)),
                      pl.BlockSpec(g.shape, lambda i, j, *_: (0, 0)),
                      pl.BlockSpec(memory_space=pl.ANY)],
            out_specs=pl.BlockSpec((1, TB, d), lambda i, j, *_: (i, jnp.minimum(j, nkeep - 1), 0)),
            scratch_shapes=[pltpu.VMEM((2, MOE_SLOTS * MOE_SUB, d), BF16),
                            pltpu.SemaphoreType.DMA((2, MOE_SLOTS))]),
        out_shape=jax.ShapeDtypeStruct((b, nkeep * TB, d), F32),
        compiler_params=_cp("arbitrary", "arbitrary"),
        name="moe_combine",
    )(base, lo_off, hi_off, s, pos_cols, modtab, g, ys)


def _rope_tables(seq, dim):
    t = jnp.arange(seq)
    rows = (t // GRID_W).astype(F32)
    cols = (t % GRID_W).astype(F32)
    quarter = dim // 4
    inv = ROPE_BASE ** (-jnp.arange(quarter, dtype=F32) / quarter)
    ang = jnp.concatenate([rows[:, None] * inv, cols[:, None] * inv], axis=-1)
    c, s = jnp.cos(ang), jnp.sin(ang)
    rep = LANES // dim
    return jnp.tile(jnp.concatenate([c, c], -1), (1, rep)), jnp.tile(jnp.concatenate([-s, s], -1), (1, rep))


def _block_diag(x):
    g, r, c = x.shape[-3:]
    y = jnp.einsum('...grc,gh->...grhc', x, jnp.eye(g, dtype=x.dtype))
    return y.reshape(x.shape[:-3] + (g * r, g * c))


def _s5_params(lam_re, lam_im, log_dt, b_re, b_im, c_re, c_im, nb):
    lam = lax.complex(lam_re, lam_im)
    a_bar = jnp.exp(lam * jnp.exp(log_dt)[..., None])
    b_bar = ((a_bar - 1.0) / lam)[..., None] * lax.complex(b_re, b_im)
    bt = jnp.swapaxes(b_bar, -1, -2)
    bmat = jnp.concatenate([_block_diag(jnp.real(bt)), _block_diag(jnp.imag(bt))], axis=-1).astype(BF16)
    cmat = jnp.concatenate([_block_diag(jnp.swapaxes(c_re, -1, -2)),
                            -_block_diag(jnp.swapaxes(c_im, -1, -2))], axis=-2).astype(BF16)
    ns = a_bar.shape[1] * a_bar.shape[2]
    a_re = jnp.broadcast_to(jnp.real(a_bar).reshape(2, 1, ns), (2, nb, ns))
    a_im = jnp.broadcast_to(jnp.imag(a_bar).reshape(2, 1, ns), (2, nb, ns))
    return bmat, a_re, a_im, cmat


def kernel(x, c, ctx, c_ctx, mod_w, mod_b, norm_g, ev_w_in, ev_conv_w, ev_conv_b, lru_gate_w, lru_gate_b, lru_lam, ret_theta, ret_gn, ev_w_out, ffd_w_gate, ffd_w_up, ffd_w_down, od_w_in, s5_lam_re, s5_lam_im, s5_log_dt, s5_b_re, s5_b_im, s5_c_re, s5_c_im, s5_d, s5_glu_w, s5_glu_b, diff_lam, diff_gn, od_w_out, moe_router_w, moe_router_b, moe_w_gate, moe_w_up, moe_w_down):
    b, seq, d = x.shape
    nctx = ctx.shape[1]
    l = seq + nctx
    n = b * l
    depth = mod_w.shape[0]
    nlat_t = seq // TB
    assert b == SUBLANES and seq % TB == 0 and nctx % TB == 0 and seq % nctx == 0 and l % FFN_TM == 0

    s = jnp.concatenate([x, ctx], axis=1)
    cc = jnp.concatenate([c, c_ctx[None], jnp.zeros((2 * SUBLANES - b - 1, d), F32)], axis=0)
    mt = _modtab(cc, mod_w, mod_b).reshape(depth, 2 * SUBLANES, 6, d)
    modtab = jnp.stack([jnp.broadcast_to(mt[:, b:b + 1], (depth, b, 6, d)), mt[:, :b]], axis=2)

    cos_r, sin_r = _rope_tables(seq, RET_D)
    cos_d, sin_d = _rope_tables(seq, DIFF_DH)

    for layer in range(depth):
        i = layer // 2
        mtab = modtab[layer]
        ng = norm_g[layer]
        if layer % 2 == 0:
            lw = ev_conv_w.shape[-1]
            u3d, zb = _inproj(functools.partial(_inproj_even_kernel, nlat=nlat_t, kscale=RET_D ** -0.5),
                              s, mtab, ng[0:1], ev_w_in[i].astype(BF16), cos_r, sin_r, lw, 5 * lw, nlat_t,
                              "inproj_even")
            gw = _block_diag(lru_gate_w[i]).astype(BF16)
            lf, lb = _lru(u3d.reshape(l * b, lw), ev_conv_w[i].reshape(CONV_W, 1, lw), ev_conv_b[i].reshape(1, lw),
                          gw, lru_gate_b[i].reshape(2, 2, 1, lw), lru_lam[i].reshape(2, 1, lw), b, seq, nctx)
            rf, rb = _retention(jax.nn.log_sigmoid(ret_theta[i].astype(F32)), zb, nlat_t)
            s, h2 = _outproj_even(lf, lb, zb, rf, rb, ret_gn[i].reshape(1, -1),
                                  ev_w_out[i].astype(BF16), s, mtab, ng, nlat_t)
            s = _ffn(h2.reshape(n, d), ffd_w_gate[i].astype(BF16), ffd_w_up[i].astype(BF16),
                     ffd_w_down[i].astype(BF16), s, mtab, ng[3:4], seq)
        else:
            sw = s5_d.shape[-1]
            aw = DIFF_HEADS * 2 * DIFF_DH
            lam_init = 0.8 - 0.6 * math.exp(-0.3 * layer)
            u3d, zb, ub = _inproj(functools.partial(_inproj_odd_kernel, nlat=nlat_t, qscale=DIFF_DH ** -0.5 * math.log2(math.e), s5w=sw),
                                  s, mtab, ng[0:1], od_w_in[i].astype(BF16), cos_d, sin_d, sw, 3 * aw, nlat_t,
                                  "inproj_odd", u_rows=True)
            bmat, a_re, a_im, cmat = _s5_params(s5_lam_re[i], s5_lam_im[i], s5_log_dt[i], s5_b_re[i], s5_b_im[i],
                                                s5_c_re[i], s5_c_im[i], b)
            yf, yb = _s5(u3d.reshape(l * b, sw), bmat, a_re, a_im, cmat, b, seq, nctx)
            lp = diff_lam[i].astype(F32)
            lam = (jnp.exp(jnp.sum(lp[0] * lp[1])) - jnp.exp(jnp.sum(lp[2] * lp[3])) + lam_init).reshape(1)
            att = _attention(lam, zb, diff_gn[i], seq, 1.0 - lam_init)
            rw = jnp.zeros((d, LANES), F32).at[:, :N_EXPERTS].set(moe_router_w[i])
            rbias = jnp.full((1, LANES), NEG, F32).at[0, :N_EXPERTS].set(moe_router_b[i])
            s, h2x, gates = _outproj_odd(yf, yb, ub, att,
                                        s5_d[i].reshape(1, sw), s5_glu_w[i].astype(BF16), s5_glu_b[i].reshape(1, sw),
                                        od_w_out[i].astype(BF16), s, mtab, ng, rw, rbias, nlat_t)
            n_tiles = (2 * n) // MOE_TM + N_EXPERTS + 2
            plan = _moe_plan(gates.reshape(n, LANES), n_tiles)
            ys = _moe(h2x.reshape(n, d + LANES), plan, _to_bf16(moe_w_gate, i), _to_bf16(moe_w_up, i),
                      _to_bf16(moe_w_down, i), n_tiles)
            s = _moe_combine(s, ys, plan, mtab, ng[3:4], nlat_t, latent_only=layer == depth - 1)
    return s[:, :seq]
```
